```python
import jax, jax.numpy as jnp
from jax import lax
import numpy as np

D_MODEL = 2048
BATCH = 1
SEQ = 8192
DEPTH = 4

N_MIXERS = 2
N_ATTN_LAYERS = (DEPTH + N_MIXERS - 1) // N_MIXERS
N_RNN_LAYERS = DEPTH // N_MIXERS

GRID_W = 64
ROPE_THETA = 10000.0

HEAD_DIM = 128
N_Q_HEADS = D_MODEL // HEAD_DIM
N_KV_HEADS = 4
GQA_GROUP = N_Q_HEADS // N_KV_HEADS
Q_BLOCK = 128
ROPE_AXIS_DIM = HEAD_DIM // 2
ROPE_FREQS = ROPE_AXIS_DIM // 2
QKV_DIM = (N_Q_HEADS + 2 * N_KV_HEADS) * HEAD_DIM

D_RNN = D_MODEL
RNN_BLOCK_W = 256
RNN_BLOCKS = D_RNN // RNN_BLOCK_W
CONV_W = 4
CONV_LEFT = 2
CONV_RIGHT = CONV_W - 1 - CONV_LEFT
LRU_C = 8.0

D_FF = -(-8 * D_MODEL // (3 * 256)) * 256

EPS = 1e-6

kernel_name = "hybrid_axial_gqa_rglru_encoder"


def rms_norm(x, g):
    xf = x.astype(jnp.float32)
    y = xf * lax.rsqrt(jnp.mean(xf * xf, axis=-1, keepdims=True) + EPS)
    return (y * g.astype(jnp.float32)).astype(x.dtype)


def axial_rope_tables(seq_len):
    rows_n = seq_len // GRID_W
    freqs = ROPE_THETA ** (-jnp.arange(ROPE_FREQS, dtype=jnp.float32) / ROPE_FREQS)
    row_ang = jnp.arange(rows_n, dtype=jnp.float32)[:, None, None] * freqs
    col_ang = jnp.arange(GRID_W, dtype=jnp.float32)[None, :, None] * freqs
    shape = (rows_n, GRID_W, ROPE_FREQS)
    ang = jnp.stack([jnp.broadcast_to(row_ang, shape), jnp.broadcast_to(col_ang, shape)], axis=-2)
    ang = ang.reshape(seq_len, 2, ROPE_FREQS)
    return jnp.cos(ang), jnp.sin(ang)


def apply_axial_rope(x, cos, sin):
    xr = x.astype(jnp.float32).reshape(*x.shape[:-1], 2, 2, ROPE_FREQS)
    x1, x2 = xr[..., 0, :], xr[..., 1, :]
    c = cos[None, :, None]
    s = sin[None, :, None]
    out = jnp.stack([x1 * c - x2 * s, x2 * c + x1 * s], axis=-2)
    return out.reshape(x.shape).astype(x.dtype)


def attention_mixer(h, w_qkv, q_gain, k_gain, w_o, cos, sin):
    b, s, _ = h.shape
    qkv = h @ w_qkv
    q, k, v = jnp.split(qkv, [N_Q_HEADS * HEAD_DIM, (N_Q_HEADS + N_KV_HEADS) * HEAD_DIM], axis=-1)
    q = q.reshape(b, s, N_Q_HEADS, HEAD_DIM)
    k = k.reshape(b, s, N_KV_HEADS, HEAD_DIM)
    v = v.reshape(b, s, N_KV_HEADS, HEAD_DIM)
    q = apply_axial_rope(rms_norm(q, q_gain), cos, sin) * (HEAD_DIM ** -0.5)
    k = apply_axial_rope(rms_norm(k, k_gain), cos, sin)
    n_blocks = s // Q_BLOCK
    qb = q.reshape(b, n_blocks, Q_BLOCK, N_KV_HEADS, GQA_GROUP, HEAD_DIM).transpose(1, 0, 2, 3, 4, 5)

    def one_block(q_blk):
        sc = jnp.einsum('bqkgd,bskd->bkgqs', q_blk, k).astype(jnp.float32)
        p = jax.nn.softmax(sc, axis=-1).astype(v.dtype)
        return jnp.einsum('bkgqs,bskd->bqkgd', p, v)

    o = lax.map(one_block, qb)
    o = o.transpose(1, 0, 2, 3, 4, 5).reshape(b, s, N_Q_HEADS * HEAD_DIM)
    return o @ w_o


def depthwise_conv_centred(x, w, bias):
    y = lax.conv_general_dilated(
        x, w[:, None, :], window_strides=(1,), padding=[(CONV_LEFT, CONV_RIGHT)],
        dimension_numbers=('NWC', 'WIO', 'NWC'), feature_group_count=x.shape[-1])
    return y + bias


def block_diag_linear(x, w, bias):
    b, s, _ = x.shape
    xb = x.reshape(b, s, RNN_BLOCKS, RNN_BLOCK_W)
    return jnp.einsum('bsni,nij->bsnj', xb, w).reshape(b, s, D_RNN) + bias


def rg_lru(x, w_a, b_a, w_i, b_i, lam, reverse):
    xf = x.astype(jnp.float32)
    r = jax.nn.sigmoid(block_diag_linear(xf, w_a, b_a).astype(jnp.float32))
    i = jax.nn.sigmoid(block_diag_linear(xf, w_i, b_i).astype(jnp.float32))
    log_a = -LRU_C * r * jax.nn.softplus(-lam.astype(jnp.float32))
    a = jnp.exp(log_a)
    u = jnp.sqrt(-jnp.expm1(2.0 * log_a)) * (i * xf)

    def combine(e1, e2):
        a1, b1 = e1
        a2, b2 = e2
        return a1 * a2, a2 * b1 + b2

    _, hs = lax.associative_scan(combine, (a, u), axis=1, reverse=reverse)
    return hs


def recurrent_mixer(h, w_in, conv_w, conv_b, w_a, b_a, w_i, b_i, lam, w_out):
    xy = h @ w_in
    xb, yb = jnp.split(xy, 2, axis=-1)
    yb = jax.nn.gelu(yb, approximate=True)
    xb = depthwise_conv_centred(xb, conv_w, conv_b)
    h_fwd = rg_lru(xb, w_a[0], b_a[0], w_i[0], b_i[0], lam[0], reverse=False)
    h_bwd = rg_lru(xb, w_a[1], b_a[1], w_i[1], b_i[1], lam[1], reverse=True)
    out = (h_fwd + h_bwd).astype(h.dtype) * yb
    return out @ w_out


def swiglu(h, w_gate, w_up, w_down):
    return (jax.nn.silu(h @ w_gate) * (h @ w_up)) @ w_down


def setup_inputs(seed: int = 0) -> dict:
    key = jax.random.key(seed)
    ks = jax.random.split(key, 24)
    f32 = jnp.float32

    def nrm(k, shape, fan_in):
        return jax.random.normal(k, shape, f32) * (fan_in ** -0.5)

    def small(k, shape, scale=0.01):
        return jax.random.normal(k, shape, f32) * scale

    u = jax.random.uniform(ks[15], (N_RNN_LAYERS, 2, D_RNN), f32, minval=0.9, maxval=0.999)
    s_lam = u ** (1.0 / LRU_C)
    rnn_lambda = jnp.log(s_lam) - jnp.log1p(-s_lam)

    return {
        "x": jax.random.normal(ks[0], (BATCH, SEQ, D_MODEL), f32),
        "norm_mix": 1.0 + small(ks[1], (DEPTH, D_MODEL), 0.02),
        "norm_ffn": 1.0 + small(ks[2], (DEPTH, D_MODEL), 0.02),
        "attn_w_qkv": nrm(ks[3], (N_ATTN_LAYERS, D_MODEL, QKV_DIM), D_MODEL),
        "attn_q_gain": 1.0 + small(ks[4], (N_ATTN_LAYERS, HEAD_DIM), 0.02),
        "attn_k_gain": 1.0 + small(ks[5], (N_ATTN_LAYERS, HEAD_DIM), 0.02),
        "attn_w_o": nrm(ks[6], (N_ATTN_LAYERS, N_Q_HEADS * HEAD_DIM, D_MODEL), N_Q_HEADS * HEAD_DIM),
        "rnn_w_in": nrm(ks[7], (N_RNN_LAYERS, D_MODEL, 2 * D_RNN), D_MODEL),
        "rnn_conv_w": nrm(ks[8], (N_RNN_LAYERS, CONV_W, D_RNN), CONV_W),
        "rnn_conv_b": small(ks[9], (N_RNN_LAYERS, D_RNN)),
        "rnn_w_a": nrm(ks[10], (N_RNN_LAYERS, 2, RNN_BLOCKS, RNN_BLOCK_W, RNN_BLOCK_W), RNN_BLOCK_W),
        "rnn_b_a": small(ks[11], (N_RNN_LAYERS, 2, D_RNN)),
        "rnn_w_i": nrm(ks[12], (N_RNN_LAYERS, 2, RNN_BLOCKS, RNN_BLOCK_W, RNN_BLOCK_W), RNN_BLOCK_W),
        "rnn_b_i": small(ks[13], (N_RNN_LAYERS, 2, D_RNN)),
        "rnn_lambda": rnn_lambda,
        "rnn_w_out": nrm(ks[14], (N_RNN_LAYERS, D_RNN, D_MODEL), D_RNN),
        "ffn_w_gate": nrm(ks[16], (DEPTH, D_MODEL, D_FF), D_MODEL),
        "ffn_w_up": nrm(ks[17], (DEPTH, D_MODEL, D_FF), D_MODEL),
        "ffn_w_down": nrm(ks[18], (DEPTH, D_FF, D_MODEL), D_FF),
    }


def reference(x, norm_mix, norm_ffn, attn_w_qkv, attn_q_gain, attn_k_gain, attn_w_o,
              rnn_w_in, rnn_conv_w, rnn_conv_b, rnn_w_a, rnn_b_a, rnn_w_i, rnn_b_i,
              rnn_lambda, rnn_w_out, ffn_w_gate, ffn_w_up, ffn_w_down):
    seq_len = x.shape[1]
    cos, sin = axial_rope_tables(seq_len)
    for i in range(DEPTH):
        j = i // N_MIXERS
        h = rms_norm(x, norm_mix[i])
        if i % N_MIXERS == 0:
            mix = attention_mixer(h, attn_w_qkv[j], attn_q_gain[j], attn_k_gain[j], attn_w_o[j], cos, sin)
        else:
            mix = recurrent_mixer(h, rnn_w_in[j], rnn_conv_w[j], rnn_conv_b[j], rnn_w_a[j], rnn_b_a[j],
                                  rnn_w_i[j], rnn_b_i[j], rnn_lambda[j], rnn_w_out[j])
        x = x + mix
        x = x + swiglu(rms_norm(x, norm_ffn[i]), ffn_w_gate[i], ffn_w_up[i], ffn_w_down[i])
    return x
```

```python
import functools

import jax
import jax.numpy as jnp
from jax import lax
from jax.experimental import pallas as pl
from jax.experimental.pallas import tpu as pltpu

F32 = jnp.float32
BF16 = jnp.bfloat16

D_MODEL = 2048
DEPTH = 4
N_MIXERS = 2
GRID_W = 64
ROPE_THETA = 10000.0
HEAD_DIM = 128
N_Q_HEADS = 16
N_KV_HEADS = 4
GQA_GROUP = N_Q_HEADS // N_KV_HEADS
ROPE_FREQS = HEAD_DIM // 4
QKV_DIM = (N_Q_HEADS + 2 * N_KV_HEADS) * HEAD_DIM
D_RNN = D_MODEL
RNN_BLOCK_W = 256
RNN_BLOCKS = D_RNN // RNN_BLOCK_W
CONV_W = 4
CONV_LEFT = 2
LRU_C = 8.0
D_FF = 5632
EPS = 1e-6

SUBLANES = 8
VMEM_LIMIT_BYTES = 56 * 1024 * 1024

TM = 512
TN = 512
TF = 512
TQ = 512
TK = 512
TT = 512


def _params(*sem):
    return pltpu.CompilerParams(dimension_semantics=sem, vmem_limit_bytes=VMEM_LIMIT_BYTES)


def _rms_normed(x, g):
    ms = jnp.mean(x * x, axis=-1, keepdims=True)
    return x * lax.rsqrt(ms + EPS) * g


def _norm_rope(y, gain, cos, sin):
    yn = _rms_normed(y, gain)
    lane = lax.broadcasted_iota(jnp.int32, yn.shape, 1)
    first_half = (lane & (2 * ROPE_FREQS - 1)) < ROPE_FREQS
    partner = jnp.where(first_half,
                        pltpu.roll(yn, HEAD_DIM - ROPE_FREQS, 1),
                        pltpu.roll(yn, ROPE_FREQS, 1))
    return yn * cos + partner * sin


def _qkv_kernel(x_ref, g_ref, w_ref, qg_ref, kg_ref, cos_ref, sin_ref, o_ref, hn_ref):
    j = pl.program_id(1)

    @pl.when(j == 0)
    def _():
        hn_ref[...] = _rms_normed(x_ref[...], g_ref[...]).astype(BF16)

    y = jnp.dot(hn_ref[...], w_ref[...], preferred_element_type=F32)
    heads_per_tile = TN // HEAD_DIM
    n_q_tiles = N_Q_HEADS // heads_per_tile

    @pl.when(j < n_q_tiles)
    def _():
        for h in range(heads_per_tile):
            sl = slice(h * HEAD_DIM, (h + 1) * HEAD_DIM)
            q = _norm_rope(y[:, sl], qg_ref[...], cos_ref[...], sin_ref[...])
            o_ref[:, sl] = (q * (HEAD_DIM ** -0.5)).astype(BF16)

    @pl.when(j == n_q_tiles)
    def _():
        for h in range(heads_per_tile):
            sl = slice(h * HEAD_DIM, (h + 1) * HEAD_DIM)
            o_ref[:, sl] = _norm_rope(y[:, sl], kg_ref[...], cos_ref[...], sin_ref[...]).astype(BF16)

    @pl.when(j > n_q_tiles)
    def _():
        o_ref[...] = y.astype(BF16)


def _qkv_proj(x, g, w, q_gain, k_gain, cos_t, sin_t):
    s = x.shape[0]
    return pl.pallas_call(
        _qkv_kernel,
        grid=(s // TM, QKV_DIM // TN),
        in_specs=[
            pl.BlockSpec((TM, D_MODEL), lambda i, j: (i, 0)),
            pl.BlockSpec((1, D_MODEL), lambda i, j: (0, 0)),
            pl.BlockSpec((D_MODEL, TN), lambda i, j: (0, j)),
            pl.BlockSpec((1, HEAD_DIM), lambda i, j: (0, 0)),
            pl.BlockSpec((1, HEAD_DIM), lambda i, j: (0, 0)),
            pl.BlockSpec((TM, HEAD_DIM), lambda i, j: (i, 0)),
            pl.BlockSpec((TM, HEAD_DIM), lambda i, j: (i, 0)),
        ],
        out_specs=pl.BlockSpec((TM, TN), lambda i, j: (i, j)),
        out_shape=jax.ShapeDtypeStruct((s, QKV_DIM), BF16),
        scratch_shapes=[pltpu.VMEM((TM, D_MODEL), BF16)],
        compiler_params=_params("arbitrary", "arbitrary"),
        name="qkv_proj",
    )(x, g, w, q_gain, k_gain, cos_t, sin_t)


def _attn_kernel(q_ref, k_ref, v_ref, o_ref):
    q = q_ref[...]
    n_chunks = k_ref.shape[0] // TK

    def body(c, carry):
        m, l, acc = carry
        r0 = pl.multiple_of(c * TK, TK)
        kc = k_ref[pl.ds(r0, TK), :]
        vc = v_ref[pl.ds(r0, TK), :]
        s = lax.dot_general(q, kc, (((1,), (1,)), ((), ())), preferred_element_type=F32)
        m_new = jnp.maximum(m, jnp.max(s, axis=-1, keepdims=True))
        alpha = jnp.exp(m - m_new)
        p = jnp.exp(s - m_new)
        l = alpha * l + jnp.sum(p, axis=-1, keepdims=True)
        acc = alpha * acc + jnp.dot(p.astype(BF16), vc, preferred_element_type=F32)
        return m_new, l, acc

    m0 = jnp.full((TQ, 1), -jnp.inf, F32)
    l0 = jnp.zeros((TQ, 1), F32)
    acc0 = jnp.zeros((TQ, HEAD_DIM), F32)
    _, l, acc = lax.fori_loop(0, n_chunks, body, (m0, l0, acc0))
    o_ref[...] = (acc / l).astype(BF16)


def _attention(qkv):
    s = qkv.shape[0]
    return pl.pallas_call(
        _attn_kernel,
        grid=(N_KV_HEADS, s // TQ, GQA_GROUP),
        in_specs=[
            pl.BlockSpec((TQ, HEAD_DIM), lambda h, i, g: (i, h * GQA_GROUP + g)),
            pl.BlockSpec((s, HEAD_DIM), lambda h, i, g: (0, N_Q_HEADS + h)),
            pl.BlockSpec((s, HEAD_DIM), lambda h, i, g: (0, N_Q_HEADS + N_KV_HEADS + h)),
        ],
        out_specs=pl.BlockSpec((TQ, HEAD_DIM), lambda h, i, g: (i, h * GQA_GROUP + g)),
        out_shape=jax.ShapeDtypeStruct((s, N_Q_HEADS * HEAD_DIM), BF16),
        compiler_params=_params("arbitrary", "arbitrary", "arbitrary"),
        name="attention",
    )(qkv, qkv, qkv)


def _proj_res_kernel(a_ref, w_ref, x_ref, o_ref):
    o_ref[...] = x_ref[...] + jnp.dot(a_ref[...], w_ref[...], preferred_element_type=F32)


def _proj_residual(a, w, x):
    s, k = a.shape
    return pl.pallas_call(
        _proj_res_kernel,
        grid=(s // TM, D_MODEL // TN),
        in_specs=[
            pl.BlockSpec((TM, k), lambda i, j: (i, 0)),
            pl.BlockSpec((k, TN), lambda i, j: (0, j)),
            pl.BlockSpec((TM, TN), lambda i, j: (i, j)),
        ],
        out_specs=pl.BlockSpec((TM, TN), lambda i, j: (i, j)),
        out_shape=jax.ShapeDtypeStruct((s, D_MODEL), F32),
        compiler_params=_params("arbitrary", "arbitrary"),
        name="proj_residual",
    )(a, w, x)


def _rnn_in_kernel(x_ref, g_ref, wx_ref, wy_ref, xb_ref, yb_ref, hn_ref):
    @pl.when(pl.program_id(1) == 0)
    def _():
        hn_ref[...] = _rms_normed(x_ref[...], g_ref[...]).astype(BF16)

    hn = hn_ref[...]
    xb_ref[...] = jnp.dot(hn, wx_ref[...], preferred_element_type=F32)
    yb_ref[...] = jax.nn.gelu(jnp.dot(hn, wy_ref[...], preferred_element_type=F32), approximate=True)


def _rnn_in_proj(x, g, w_in):
    s = x.shape[0]
    n_col = D_RNN // TN
    return pl.pallas_call(
        _rnn_in_kernel,
        grid=(s // TM, n_col),
        in_specs=[
            pl.BlockSpec((TM, D_MODEL), lambda i, j: (i, 0)),
            pl.BlockSpec((1, D_MODEL), lambda i, j: (0, 0)),
            pl.BlockSpec((D_MODEL, TN), lambda i, j: (0, j)),
            pl.BlockSpec((D_MODEL, TN), lambda i, j: (0, j + n_col)),
        ],
        out_specs=[
            pl.BlockSpec((TM, TN), lambda i, j: (i, j)),
            pl.BlockSpec((TM, TN), lambda i, j: (i, j)),
        ],
        out_shape=[jax.ShapeDtypeStruct((s, D_RNN), F32), jax.ShapeDtypeStruct((s, D_RNN), F32)],
        scratch_shapes=[pltpu.VMEM((TM, D_MODEL), BF16)],
        compiler_params=_params("arbitrary", "arbitrary"),
        name="rnn_in_proj",
    )(x, g, w_in, w_in)


def _softplus(z):
    return jnp.maximum(z, 0.0) + jnp.log1p(jnp.exp(-jnp.abs(z)))


def _lru_inputs(d, cur_ref, prev_ref, next_ref, first, last, cw_ref, cb_ref, wa_ref, ba_ref, wi_ref,
                bi_ref, lam_ref, ext_ref, a_ref, u_ref):
    ext_ref[d, 0:SUBLANES, :] = jnp.where(first, 0.0, prev_ref[...])
    ext_ref[d, SUBLANES:SUBLANES + TT, :] = cur_ref[...]
    ext_ref[d, SUBLANES + TT:2 * SUBLANES + TT, :] = jnp.where(last, 0.0, next_ref[...])
    xc = cb_ref[...]
    for k in range(CONV_W):
        r0 = SUBLANES + k - CONV_LEFT
        xc = xc + cw_ref[k:k + 1, :] * ext_ref[d, r0:r0 + TT, :]
    xb16 = xc.astype(BF16)
    r = jax.nn.sigmoid(jnp.dot(xb16, wa_ref[d, 0], preferred_element_type=F32) + ba_ref[d:d + 1, :])
    i = jax.nn.sigmoid(jnp.dot(xb16, wi_ref[d, 0], preferred_element_type=F32) + bi_ref[d:d + 1, :])
    log_a = (-LRU_C * r) * _softplus(-lam_ref[d:d + 1, :])
    a = jnp.exp(log_a)
    a_ref[d] = a
    u_ref[d] = jnp.sqrt(-jnp.tanh(log_a) * (a * a + 1.0)) * (i * xc)


def _rnn_core_kernel(fc_ref, fp_ref, fn_ref, bc_ref, bp_ref, bn_ref, cw_ref, cb_ref, wa_ref, ba_ref,
                     wi_ref, bi_ref, lam_ref, hf_ref, hb_ref, ext_ref, a_ref, u_ref, carry_ref):
    i = pl.program_id(1)
    n = pl.num_programs(1)

    @pl.when(i == 0)
    def _():
        carry_ref[...] = jnp.zeros_like(carry_ref)

    shared = (cw_ref, cb_ref, wa_ref, ba_ref, wi_ref, bi_ref, lam_ref, ext_ref, a_ref, u_ref)
    _lru_inputs(0, fc_ref, fp_ref, fn_ref, i == 0, i == n - 1, *shared)
    _lru_inputs(1, bc_ref, bp_ref, bn_ref, i == n - 1, i == 0, *shared)

    n_groups = TT // SUBLANES
    row = lax.broadcasted_iota(jnp.int32, (SUBLANES, RNN_BLOCK_W), 0)

    def body(k, carry):
        h_prev, h_next = carry
        r0 = pl.multiple_of(k * SUBLANES, SUBLANES)
        a = a_ref[0, pl.ds(r0, SUBLANES), :]
        u = u_ref[0, pl.ds(r0, SUBLANES), :]
        for step in (1, 2, 4):
            keep = row >= step
            u = u + a * jnp.where(keep, pltpu.roll(u, step, 0), 0.0)
            a = a * jnp.where(keep, pltpu.roll(a, step, 0), 1.0)
        hf = u + a * h_prev
        hf_ref[pl.ds(r0, SUBLANES), :] = hf
        q0 = pl.multiple_of((n_groups - 1 - k) * SUBLANES, SUBLANES)
        a = a_ref[1, pl.ds(q0, SUBLANES), :]
        u = u_ref[1, pl.ds(q0, SUBLANES), :]
        for step in (1, 2, 4):
            keep = row < SUBLANES - step
            u = u + a * jnp.where(keep, pltpu.roll(u, SUBLANES - step, 0), 0.0)
            a = a * jnp.where(keep, pltpu.roll(a, SUBLANES - step, 0), 1.0)
        hb = u + a * h_next
        hb_ref[pl.ds(q0, SUBLANES), :] = hb
        return (jnp.broadcast_to(hf[SUBLANES - 1:SUBLANES, :], hf.shape),
                jnp.broadcast_to(hb[0:1, :], hb.shape))

    h_prev, h_next = lax.fori_loop(0, n_groups, body, (carry_ref[0], carry_ref[1]))
    carry_ref[0] = h_prev
    carry_ref[1] = h_next


def _rnn_core(xb, conv_w, conv_b, w_a, b_a, w_i, b_i, lam):
    s = xb.shape[0]
    n = s // TT
    halo = TT // SUBLANES
    last_halo = s // SUBLANES - 1
    cur = lambda t: pl.BlockSpec((TT, RNN_BLOCK_W), lambda c, i: (t(i, n), c))
    prv = lambda t: pl.BlockSpec((SUBLANES, RNN_BLOCK_W),
                                 lambda c, i: (jnp.maximum(t(i, n) * halo - 1, 0), c))
    nxt = lambda t: pl.BlockSpec((SUBLANES, RNN_BLOCK_W),
                                 lambda c, i: (jnp.minimum((t(i, n) + 1) * halo, last_halo), c))
    fwd = lambda i, n: i
    bwd = lambda i, n: n - 1 - i
    chan = lambda rows: pl.BlockSpec((rows, RNN_BLOCK_W), lambda c, i: (0, c))
    gate_w = pl.BlockSpec((2, 1, RNN_BLOCK_W, RNN_BLOCK_W), lambda c, i: (0, c, 0, 0))
    return pl.pallas_call(
        _rnn_core_kernel,
        grid=(RNN_BLOCKS, n),
        in_specs=[cur(fwd), prv(fwd), nxt(fwd), cur(bwd), prv(bwd), nxt(bwd),
                  chan(CONV_W), chan(1), gate_w, chan(2), gate_w, chan(2), chan(2)],
        out_specs=[pl.BlockSpec((TT, RNN_BLOCK_W), lambda c, i: (i, c)),
                   pl.BlockSpec((TT, RNN_BLOCK_W), lambda c, i: (n - 1 - i, c))],
        out_shape=[jax.ShapeDtypeStruct((s, D_RNN), F32), jax.ShapeDtypeStruct((s, D_RNN), F32)],
        scratch_shapes=[
            pltpu.VMEM((2, TT + 2 * SUBLANES, RNN_BLOCK_W), F32),
            pltpu.VMEM((2, TT, RNN_BLOCK_W), F32),
            pltpu.VMEM((2, TT, RNN_BLOCK_W), F32),
            pltpu.VMEM((2, SUBLANES, RNN_BLOCK_W), F32),
        ],
        compiler_params=_params("arbitrary", "arbitrary"),
        name="rnn_core",
    )(xb, xb, xb, xb, xb, xb, conv_w, conv_b, w_a, b_a, w_i, b_i, lam)


def _rnn_out_kernel(hf_ref, hb_ref, yb_ref, w_ref, x_ref, o_ref, a_ref):
    @pl.when(pl.program_id(1) == 0)
    def _():
        a_ref[...] = ((hf_ref[...] + hb_ref[...]) * yb_ref[...]).astype(BF16)

    o_ref[...] = x_ref[...] + jnp.dot(a_ref[...], w_ref[...], preferred_element_type=F32)


def _rnn_out_proj(hf, hb, yb, w, x):
    s = x.shape[0]
    row = pl.BlockSpec((TM, D_RNN), lambda i, j: (i, 0))
    return pl.pallas_call(
        _rnn_out_kernel,
        grid=(s // TM, D_MODEL // TN),
        in_specs=[row, row, row,
                  pl.BlockSpec((D_RNN, TN), lambda i, j: (0, j)),
                  pl.BlockSpec((TM, TN), lambda i, j: (i, j))],
        out_specs=pl.BlockSpec((TM, TN), lambda i, j: (i, j)),
        out_shape=jax.ShapeDtypeStruct((s, D_MODEL), F32),
        scratch_shapes=[pltpu.VMEM((TM, D_RNN), BF16)],
        compiler_params=_params("arbitrary", "arbitrary"),
        name="rnn_out_proj",
    )(hf, hb, yb, w, x)


def _ffn_kernel(x_ref, g_ref, wg_ref, wu_ref, wd_ref, o_ref, hn_ref):
    @pl.when(pl.program_id(1) == 0)
    def _():
        x = x_ref[...]
        hn_ref[...] = _rms_normed(x, g_ref[...]).astype(BF16)
        o_ref[...] = x

    hn = hn_ref[...]
    gate = jnp.dot(hn, wg_ref[...], preferred_element_type=F32)
    up = jnp.dot(hn, wu_ref[...], preferred_element_type=F32)
    act = (jax.nn.silu(gate) * up).astype(BF16)
    o_ref[...] += jnp.dot(act, wd_ref[...], preferred_element_type=F32)


def _ffn(x, g, w_gate, w_up, w_down):
    s = x.shape[0]
    return pl.pallas_call(
        _ffn_kernel,
        grid=(s // TM, D_FF // TF),
        in_specs=[
            pl.BlockSpec((TM, D_MODEL), lambda i, f: (i, 0)),
            pl.BlockSpec((1, D_MODEL), lambda i, f: (0, 0)),
            pl.BlockSpec((D_MODEL, TF), lambda i, f: (0, f)),
            pl.BlockSpec((D_MODEL, TF), lambda i, f: (0, f)),
            pl.BlockSpec((TF, D_MODEL), lambda i, f: (f, 0)),
        ],
        out_specs=pl.BlockSpec((TM, D_MODEL), lambda i, f: (i, 0)),
        out_shape=jax.ShapeDtypeStruct((s, D_MODEL), F32),
        scratch_shapes=[pltpu.VMEM((TM, D_MODEL), BF16)],
        compiler_params=_params("arbitrary", "arbitrary"),
        name="ffn",
    )(x, g, w_gate, w_up, w_down)


def _rope_tables(seq_len):
    rows_n = seq_len // GRID_W
    freqs = ROPE_THETA ** (-jnp.arange(ROPE_FREQS, dtype=F32) / ROPE_FREQS)
    row_ang = jnp.arange(rows_n, dtype=F32)[:, None, None] * freqs
    col_ang = jnp.arange(GRID_W, dtype=F32)[None, :, None] * freqs
    shape = (rows_n, GRID_W, ROPE_FREQS)
    row_ang = jnp.broadcast_to(row_ang, shape).reshape(seq_len, ROPE_FREQS)
    col_ang = jnp.broadcast_to(col_ang, shape).reshape(seq_len, ROPE_FREQS)
    cr, sr, cc, sc = jnp.cos(row_ang), jnp.sin(row_ang), jnp.cos(col_ang), jnp.sin(col_ang)
    return (jnp.concatenate([cr, cr, cc, cc], axis=-1),
            jnp.concatenate([-sr, sr, -sc, sc], axis=-1))


def kernel(x, norm_mix, norm_ffn, attn_w_qkv, attn_q_gain, attn_k_gain, attn_w_o, rnn_w_in, rnn_conv_w, rnn_conv_b, rnn_w_a, rnn_b_a, rnn_w_i, rnn_b_i, rnn_lambda, rnn_w_out, ffn_w_gate, ffn_w_up, ffn_w_down):
    b, s, d = x.shape
    assert (b, d) == (1, D_MODEL) and s % max(TM, TQ, TK, TT) == 0
    cos_t, sin_t = _rope_tables(s)
    h = x.reshape(s, d)
    for i in range(DEPTH):
        j = i // N_MIXERS
        g_mix = norm_mix[i].reshape(1, d)
        if i % N_MIXERS == 0:
            qkv = _qkv_proj(h, g_mix, attn_w_qkv[j].astype(BF16),
                            attn_q_gain[j].reshape(1, HEAD_DIM), attn_k_gain[j].reshape(1, HEAD_DIM),
                            cos_t, sin_t)
            o = _attention(qkv)
            h = _proj_residual(o, attn_w_o[j].astype(BF16), h)
        else:
            xb, yb = _rnn_in_proj(h, g_mix, rnn_w_in[j].astype(BF16))
            hf, hb = _rnn_core(xb, rnn_conv_w[j], rnn_conv_b[j].reshape(1, D_RNN),
                               rnn_w_a[j].astype(BF16), rnn_b_a[j], rnn_w_i[j].astype(BF16),
                               rnn_b_i[j], rnn_lambda[j])
            h = _rnn_out_proj(hf, hb, yb, rnn_w_out[j].astype(BF16), h)
        h = _ffn(h, norm_ffn[i].reshape(1, d), ffn_w_gate[i].astype(BF16),
                 ffn_w_up[i].astype(BF16), ffn_w_down[i].astype(BF16))
    return h.reshape(b, s, d)
```

```python
import functools

import jax
import jax.numpy as jnp
from jax import lax
from jax.experimental import pallas as pl
from jax.experimental.pallas import tpu as pltpu

F32 = jnp.float32
BF16 = jnp.bfloat16

D_MODEL = 2048
DEPTH = 4
N_MIXERS = 2
GRID_W = 64
ROPE_THETA = 10000.0
HEAD_DIM = 128
N_Q_HEADS = 16
N_KV_HEADS = 4
GQA_GROUP = N_Q_HEADS // N_KV_HEADS
ROPE_FREQS = HEAD_DIM // 4
QKV_DIM = (N_Q_HEADS + 2 * N_KV_HEADS) * HEAD_DIM
D_RNN = D_MODEL
RNN_BLOCK_W = 256
RNN_BLOCKS = D_RNN // RNN_BLOCK_W
CONV_W = 4
CONV_LEFT = 2
LRU_C = 8.0
D_FF = 5632
EPS = 1e-6
Q_SCALE = HEAD_DIM ** -0.5 * 1.4426950408889634

SUBLANES = 8
VMEM_LIMIT_BYTES = 56 * 1024 * 1024

TM = 512
TN = 512
TF = 512
TQ = 512
TK = 512
TT = 512


def _params(*sem):
    return pltpu.CompilerParams(dimension_semantics=sem, vmem_limit_bytes=VMEM_LIMIT_BYTES)


def _rms_normed(x, g):
    ms = jnp.mean(x * x, axis=-1, keepdims=True)
    return x * lax.rsqrt(ms + EPS) * g


def _norm_rope(y, gain, cos, sin):
    yn = _rms_normed(y, gain)
    lane = lax.broadcasted_iota(jnp.int32, yn.shape, 1)
    first_half = (lane & (2 * ROPE_FREQS - 1)) < ROPE_FREQS
    partner = jnp.where(first_half,
                        pltpu.roll(yn, HEAD_DIM - ROPE_FREQS, 1),
                        pltpu.roll(yn, ROPE_FREQS, 1))
    return yn * cos + partner * sin


def _qkv_kernel(x_ref, g_ref, w_ref, qg_ref, kg_ref, cos_ref, sin_ref, o_ref, hn_ref):
    j = pl.program_id(1)

    @pl.when(j == 0)
    def _():
        hn_ref[...] = _rms_normed(x_ref[...], g_ref[...]).astype(BF16)

    y = jnp.dot(hn_ref[...], w_ref[...], preferred_element_type=F32)
    heads_per_tile = TN // HEAD_DIM
    n_q_tiles = N_Q_HEADS // heads_per_tile

    @pl.when(j < n_q_tiles)
    def _():
        for h in range(heads_per_tile):
            sl = slice(h * HEAD_DIM, (h + 1) * HEAD_DIM)
            q = _norm_rope(y[:, sl], qg_ref[...], cos_ref[...], sin_ref[...])
            o_ref[:, sl] = (q * Q_SCALE).astype(BF16)

    @pl.when(j == n_q_tiles)
    def _():
        for h in range(heads_per_tile):
            sl = slice(h * HEAD_DIM, (h + 1) * HEAD_DIM)
            o_ref[:, sl] = _norm_rope(y[:, sl], kg_ref[...], cos_ref[...], sin_ref[...]).astype(BF16)

    @pl.when(j > n_q_tiles)
    def _():
        o_ref[...] = y.astype(BF16)


def _qkv_proj(x, g, w, q_gain, k_gain, cos_t, sin_t):
    s = x.shape[0]
    return pl.pallas_call(
        _qkv_kernel,
        grid=(s // TM, QKV_DIM // TN),
        in_specs=[
            pl.BlockSpec((TM, D_MODEL), lambda i, j: (i, 0)),
            pl.BlockSpec((1, D_MODEL), lambda i, j: (0, 0)),
            pl.BlockSpec((D_MODEL, TN), lambda i, j: (0, j)),
            pl.BlockSpec((1, HEAD_DIM), lambda i, j: (0, 0)),
            pl.BlockSpec((1, HEAD_DIM), lambda i, j: (0, 0)),
            pl.BlockSpec((TM, HEAD_DIM), lambda i, j: (i, 0)),
            pl.BlockSpec((TM, HEAD_DIM), lambda i, j: (i, 0)),
        ],
        out_specs=pl.BlockSpec((TM, TN), lambda i, j: (i, j)),
        out_shape=jax.ShapeDtypeStruct((s, QKV_DIM), BF16),
        scratch_shapes=[pltpu.VMEM((TM, D_MODEL), BF16)],
        compiler_params=_params("arbitrary", "arbitrary"),
        name="qkv_proj",
    )(x, g, w, q_gain, k_gain, cos_t, sin_t)


def _attn_kernel(q_ref, k_ref, v_ref, o_ref, vext_ref, acc_ref):
    @pl.when(pl.program_id(1) == 0)
    def _():
        vext_ref[:, :HEAD_DIM] = v_ref[...]
        vext_ref[:, HEAD_DIM:] = jnp.ones((vext_ref.shape[0], HEAD_DIM), BF16)

    acc_ref[...] = jnp.zeros_like(acc_ref)
    n_chunks = k_ref.shape[0] // TK

    def body(c, ms):
        r0 = pl.multiple_of(c * TK, TK)
        kc = k_ref[pl.ds(r0, TK), :]
        vc = vext_ref[pl.ds(r0, TK), :]
        new_ms = []
        for g in range(GQA_GROUP):
            q = q_ref[:, g * HEAD_DIM:(g + 1) * HEAD_DIM]
            s = lax.dot_general(q, kc, (((1,), (1,)), ((), ())), preferred_element_type=F32)
            m_new = jnp.maximum(ms[g], jnp.max(s, axis=-1, keepdims=True))
            p = jnp.exp2(s - m_new).astype(BF16)
            alpha = jnp.exp2(ms[g] - m_new)
            acc_ref[g] = alpha * acc_ref[g] + jnp.dot(p, vc, preferred_element_type=F32)
            new_ms.append(m_new)
        return tuple(new_ms)

    m0 = jnp.full((TQ, 1), -jnp.inf, F32)
    lax.fori_loop(0, n_chunks, body, (m0,) * GQA_GROUP)
    for g in range(GQA_GROUP):
        acc = acc_ref[g]
        o_ref[:, g * HEAD_DIM:(g + 1) * HEAD_DIM] = (
            acc[:, :HEAD_DIM] / acc[:, HEAD_DIM:HEAD_DIM + 1]).astype(BF16)


def _attention(qkv):
    s = qkv.shape[0]
    group_w = GQA_GROUP * HEAD_DIM
    return pl.pallas_call(
        _attn_kernel,
        grid=(N_KV_HEADS, s // TQ),
        in_specs=[
            pl.BlockSpec((TQ, group_w), lambda h, i: (i, h)),
            pl.BlockSpec((s, HEAD_DIM), lambda h, i: (0, N_Q_HEADS + h)),
            pl.BlockSpec((s, HEAD_DIM), lambda h, i: (0, N_Q_HEADS + N_KV_HEADS + h)),
        ],
        out_specs=pl.BlockSpec((TQ, group_w), lambda h, i: (i, h)),
        out_shape=jax.ShapeDtypeStruct((s, N_Q_HEADS * HEAD_DIM), BF16),
        scratch_shapes=[pltpu.VMEM((s, 2 * HEAD_DIM), BF16),
                        pltpu.VMEM((GQA_GROUP, TQ, 2 * HEAD_DIM), F32)],
        compiler_params=_params("arbitrary", "arbitrary"),
        name="attention",
    )(qkv, qkv, qkv)


def _proj_res_kernel(a_ref, w_ref, x_ref, o_ref):
    o_ref[...] = x_ref[...] + jnp.dot(a_ref[...], w_ref[...], preferred_element_type=F32)


def _proj_residual(a, w, x):
    s, k = a.shape
    return pl.pallas_call(
        _proj_res_kernel,
        grid=(s // TM, D_MODEL // TN),
        in_specs=[
            pl.BlockSpec((TM, k), lambda i, j: (i, 0)),
            pl.BlockSpec((k, TN), lambda i, j: (0, j)),
            pl.BlockSpec((TM, TN), lambda i, j: (i, j)),
        ],
        out_specs=pl.BlockSpec((TM, TN), lambda i, j: (i, j)),
        out_shape=jax.ShapeDtypeStruct((s, D_MODEL), F32),
        compiler_params=_params("arbitrary", "arbitrary"),
        name="proj_residual",
    )(a, w, x)


def _rnn_in_kernel(x_ref, g_ref, wx_ref, wy_ref, xb_ref, yb_ref, hn_ref):
    @pl.when(pl.program_id(1) == 0)
    def _():
        hn_ref[...] = _rms_normed(x_ref[...], g_ref[...]).astype(BF16)

    hn = hn_ref[...]
    xb_ref[...] = jnp.dot(hn, wx_ref[...], preferred_element_type=F32)
    yb_ref[...] = jax.nn.gelu(jnp.dot(hn, wy_ref[...], preferred_element_type=F32), approximate=True)


def _rnn_in_proj(x, g, w_in):
    s = x.shape[0]
    n_col = D_RNN // TN
    return pl.pallas_call(
        _rnn_in_kernel,
        grid=(s // TM, n_col),
        in_specs=[
            pl.BlockSpec((TM, D_MODEL), lambda i, j: (i, 0)),
            pl.BlockSpec((1, D_MODEL), lambda i, j: (0, 0)),
            pl.BlockSpec((D_MODEL, TN), lambda i, j: (0, j)),
            pl.BlockSpec((D_MODEL, TN), lambda i, j: (0, j + n_col)),
        ],
        out_specs=[
            pl.BlockSpec((TM, TN), lambda i, j: (i, j)),
            pl.BlockSpec((TM, TN), lambda i, j: (i, j)),
        ],
        out_shape=[jax.ShapeDtypeStruct((s, D_RNN), F32), jax.ShapeDtypeStruct((s, D_RNN), F32)],
        scratch_shapes=[pltpu.VMEM((TM, D_MODEL), BF16)],
        compiler_params=_params("arbitrary", "arbitrary"),
        name="rnn_in_proj",
    )(x, g, w_in, w_in)


def _softplus(z):
    return jnp.maximum(z, 0.0) + jnp.log1p(jnp.exp(-jnp.abs(z)))


def _lru_inputs(d, cur_ref, prev_ref, next_ref, first, last, cw_ref, cb_ref, wa_ref, ba_ref, wi_ref,
                bi_ref, lam_ref, ext_ref, a_ref, u_ref):
    ext_ref[d, 0:SUBLANES, :] = jnp.where(first, 0.0, prev_ref[...])
    ext_ref[d, SUBLANES:SUBLANES + TT, :] = cur_ref[...]
    ext_ref[d, SUBLANES + TT:2 * SUBLANES + TT, :] = jnp.where(last, 0.0, next_ref[...])
    xc = cb_ref[...]
    for k in range(CONV_W):
        r0 = SUBLANES + k - CONV_LEFT
        xc = xc + cw_ref[k:k + 1, :] * ext_ref[d, r0:r0 + TT, :]
    xb16 = xc.astype(BF16)
    r = jax.nn.sigmoid(jnp.dot(xb16, wa_ref[d, 0], preferred_element_type=F32) + ba_ref[d:d + 1, :])
    i = jax.nn.sigmoid(jnp.dot(xb16, wi_ref[d, 0], preferred_element_type=F32) + bi_ref[d:d + 1, :])
    log_a = (-LRU_C * r) * _softplus(-lam_ref[d:d + 1, :])
    a = jnp.exp(log_a)
    a_ref[d] = a
    u_ref[d] = jnp.sqrt(-jnp.tanh(log_a) * (a * a + 1.0)) * (i * xc)


def _rnn_core_kernel(fc_ref, fp_ref, fn_ref, bc_ref, bp_ref, bn_ref, cw_ref, cb_ref, wa_ref, ba_ref,
                     wi_ref, bi_ref, lam_ref, hf_ref, hb_ref, ext_ref, a_ref, u_ref, carry_ref):
    i = pl.program_id(1)
    n = pl.num_programs(1)

    @pl.when(i == 0)
    def _():
        carry_ref[...] = jnp.zeros_like(carry_ref)

    shared = (cw_ref, cb_ref, wa_ref, ba_ref, wi_ref, bi_ref, lam_ref, ext_ref, a_ref, u_ref)
    _lru_inputs(0, fc_ref, fp_ref, fn_ref, i == 0, i == n - 1, *shared)
    _lru_inputs(1, bc_ref, bp_ref, bn_ref, i == n - 1, i == 0, *shared)

    n_groups = TT // SUBLANES
    row = lax.broadcasted_iota(jnp.int32, (SUBLANES, RNN_BLOCK_W), 0)

    def body(k, carry):
        h_prev, h_next = carry
        r0 = pl.multiple_of(k * SUBLANES, SUBLANES)
        a = a_ref[0, pl.ds(r0, SUBLANES), :]
        u = u_ref[0, pl.ds(r0, SUBLANES), :]
        for step in (1, 2, 4):
            keep = row >= step
            u = u + a * jnp.where(keep, pltpu.roll(u, step, 0), 0.0)
            a = a * jnp.where(keep, pltpu.roll(a, step, 0), 1.0)
        hf = u + a * h_prev
        hf_ref[pl.ds(r0, SUBLANES), :] = hf
        q0 = pl.multiple_of((n_groups - 1 - k) * SUBLANES, SUBLANES)
        a = a_ref[1, pl.ds(q0, SUBLANES), :]
        u = u_ref[1, pl.ds(q0, SUBLANES), :]
        for step in (1, 2, 4):
            keep = row < SUBLANES - step
            u = u + a * jnp.where(keep, pltpu.roll(u, SUBLANES - step, 0), 0.0)
            a = a * jnp.where(keep, pltpu.roll(a, SUBLANES - step, 0), 1.0)
        hb = u + a * h_next
        hb_ref[pl.ds(q0, SUBLANES), :] = hb
        return (jnp.broadcast_to(hf[SUBLANES - 1:SUBLANES, :], hf.shape),
                jnp.broadcast_to(hb[0:1, :], hb.shape))

    h_prev, h_next = lax.fori_loop(0, n_groups, body, (carry_ref[0], carry_ref[1]))
    carry_ref[0] = h_prev
    carry_ref[1] = h_next


def _rnn_core(xb, conv_w, conv_b, w_a, b_a, w_i, b_i, lam):
    s = xb.shape[0]
    n = s // TT
    halo = TT // SUBLANES
    last_halo = s // SUBLANES - 1
    cur = lambda t: pl.BlockSpec((TT, RNN_BLOCK_W), lambda c, i: (t(i, n), c))
    prv = lambda t: pl.BlockSpec((SUBLANES, RNN_BLOCK_W),
                                 lambda c, i: (jnp.maximum(t(i, n) * halo - 1, 0), c))
    nxt = lambda t: pl.BlockSpec((SUBLANES, RNN_BLOCK_W),
                                 lambda c, i: (jnp.minimum((t(i, n) + 1) * halo, last_halo), c))
    fwd = lambda i, n: i
    bwd = lambda i, n: n - 1 - i
    chan = lambda rows: pl.BlockSpec((rows, RNN_BLOCK_W), lambda c, i: (0, c))
    gate_w = pl.BlockSpec((2, 1, RNN_BLOCK_W, RNN_BLOCK_W), lambda c, i: (0, c, 0, 0))
    return pl.pallas_call(
        _rnn_core_kernel,
        grid=(RNN_BLOCKS, n),
        in_specs=[cur(fwd), prv(fwd), nxt(fwd), cur(bwd), prv(bwd), nxt(bwd),
                  chan(CONV_W), chan(1), gate_w, chan(2), gate_w, chan(2), chan(2)],
        out_specs=[pl.BlockSpec((TT, RNN_BLOCK_W), lambda c, i: (i, c)),
                   pl.BlockSpec((TT, RNN_BLOCK_W), lambda c, i: (n - 1 - i, c))],
        out_shape=[jax.ShapeDtypeStruct((s, D_RNN), F32), jax.ShapeDtypeStruct((s, D_RNN), F32)],
        scratch_shapes=[
            pltpu.VMEM((2, TT + 2 * SUBLANES, RNN_BLOCK_W), F32),
            pltpu.VMEM((2, TT, RNN_BLOCK_W), F32),
            pltpu.VMEM((2, TT, RNN_BLOCK_W), F32),
            pltpu.VMEM((2, SUBLANES, RNN_BLOCK_W), F32),
        ],
        compiler_params=_params("arbitrary", "arbitrary"),
        name="rnn_core",
    )(xb, xb, xb, xb, xb, xb, conv_w, conv_b, w_a, b_a, w_i, b_i, lam)


def _rnn_out_kernel(hf_ref, hb_ref, yb_ref, w_ref, x_ref, o_ref, a_ref):
    @pl.when(pl.program_id(1) == 0)
    def _():
        a_ref[...] = ((hf_ref[...] + hb_ref[...]) * yb_ref[...]).astype(BF16)

    o_ref[...] = x_ref[...] + jnp.dot(a_ref[...], w_ref[...], preferred_element_type=F32)


def _rnn_out_proj(hf, hb, yb, w, x):
    s = x.shape[0]
    row = pl.BlockSpec((TM, D_RNN), lambda i, j: (i, 0))
    return pl.pallas_call(
        _rnn_out_kernel,
        grid=(s // TM, D_MODEL // TN),
        in_specs=[row, row, row,
                  pl.BlockSpec((D_RNN, TN), lambda i, j: (0, j)),
                  pl.BlockSpec((TM, TN), lambda i, j: (i, j))],
        out_specs=pl.BlockSpec((TM, TN), lambda i, j: (i, j)),
        out_shape=jax.ShapeDtypeStruct((s, D_MODEL), F32),
        scratch_shapes=[pltpu.VMEM((TM, D_RNN), BF16)],
        compiler_params=_params("arbitrary", "arbitrary"),
        name="rnn_out_proj",
    )(hf, hb, yb, w, x)


def _ffn_kernel(x_ref, g_ref, wg_ref, wu_ref, wd_ref, o_ref, hn_ref):
    @pl.when(pl.program_id(1) == 0)
    def _():
        x = x_ref[...]
        hn_ref[...] = _rms_normed(x, g_ref[...]).astype(BF16)
        o_ref[...] = x

    hn = hn_ref[...]
    gate = jnp.dot(hn, wg_ref[...], preferred_element_type=F32)
    up = jnp.dot(hn, wu_ref[...], preferred_element_type=F32)
    act = (jax.nn.silu(gate) * up).astype(BF16)
    o_ref[...] += jnp.dot(act, wd_ref[...], preferred_element_type=F32)


def _ffn(x, g, w_gate, w_up, w_down):
    s = x.shape[0]
    return pl.pallas_call(
        _ffn_kernel,
        grid=(s // TM, D_FF // TF),
        in_specs=[
            pl.BlockSpec((TM, D_MODEL), lambda i, f: (i, 0)),
            pl.BlockSpec((1, D_MODEL), lambda i, f: (0, 0)),
            pl.BlockSpec((D_MODEL, TF), lambda i, f: (0, f)),
            pl.BlockSpec((D_MODEL, TF), lambda i, f: (0, f)),
            pl.BlockSpec((TF, D_MODEL), lambda i, f: (f, 0)),
        ],
        out_specs=pl.BlockSpec((TM, D_MODEL), lambda i, f: (i, 0)),
        out_shape=jax.ShapeDtypeStruct((s, D_MODEL), F32),
        scratch_shapes=[pltpu.VMEM((TM, D_MODEL), BF16)],
        compiler_params=_params("arbitrary", "arbitrary"),
        name="ffn",
    )(x, g, w_gate, w_up, w_down)


def _rope_tables(seq_len):
    rows_n = seq_len // GRID_W
    freqs = ROPE_THETA ** (-jnp.arange(ROPE_FREQS, dtype=F32) / ROPE_FREQS)
    row_ang = jnp.arange(rows_n, dtype=F32)[:, None, None] * freqs
    col_ang = jnp.arange(GRID_W, dtype=F32)[None, :, None] * freqs
    shape = (rows_n, GRID_W, ROPE_FREQS)
    row_ang = jnp.broadcast_to(row_ang, shape).reshape(seq_len, ROPE_FREQS)
    col_ang = jnp.broadcast_to(col_ang, shape).reshape(seq_len, ROPE_FREQS)
    cr, sr, cc, sc = jnp.cos(row_ang), jnp.sin(row_ang), jnp.cos(col_ang), jnp.sin(col_ang)
    return (jnp.concatenate([cr, cr, cc, cc], axis=-1),
            jnp.concatenate([-sr, sr, -sc, sc], axis=-1))


def kernel(x, norm_mix, norm_ffn, attn_w_qkv, attn_q_gain, attn_k_gain, attn_w_o, rnn_w_in, rnn_conv_w, rnn_conv_b, rnn_w_a, rnn_b_a, rnn_w_i, rnn_b_i, rnn_lambda, rnn_w_out, ffn_w_gate, ffn_w_up, ffn_w_down):
    b, s, d = x.shape
    assert (b, d) == (1, D_MODEL) and s % max(TM, TQ, TK, TT) == 0
    cos_t, sin_t = _rope_tables(s)
    h = x.reshape(s, d)
    for i in range(DEPTH):
        j = i // N_MIXERS
        g_mix = norm_mix[i].reshape(1, d)
        if i % N_MIXERS == 0:
            qkv = _qkv_proj(h, g_mix, attn_w_qkv[j].astype(BF16),
                            attn_q_gain[j].reshape(1, HEAD_DIM), attn_k_gain[j].reshape(1, HEAD_DIM),
                            cos_t, sin_t)
            o = _attention(qkv)
            h = _proj_residual(o, attn_w_o[j].astype(BF16), h)
        else:
            xb, yb = _rnn_in_proj(h, g_mix, rnn_w_in[j].astype(BF16))
            hf, hb = _rnn_core(xb, rnn_conv_w[j], rnn_conv_b[j].reshape(1, D_RNN),
                               rnn_w_a[j].astype(BF16), rnn_b_a[j], rnn_w_i[j].astype(BF16),
                               rnn_b_i[j], rnn_lambda[j])
            h = _rnn_out_proj(hf, hb, yb, rnn_w_out[j].astype(BF16), h)
        h = _ffn(h, norm_ffn[i].reshape(1, d), ffn_w_gate[i].astype(BF16),
                 ffn_w_up[i].astype(BF16), ffn_w_down[i].astype(BF16))
    return h.reshape(b, s, d)
```

```python
import functools

import jax
import jax.numpy as jnp
from jax import lax
from jax.experimental import pallas as pl
from jax.experimental.pallas import tpu as pltpu

F32 = jnp.float32
BF16 = jnp.bfloat16

D_MODEL = 2048
DEPTH = 4
N_MIXERS = 2
GRID_W = 64
ROPE_THETA = 10000.0
HEAD_DIM = 128
N_Q_HEADS = 16
N_KV_HEADS = 4
GQA_GROUP = N_Q_HEADS // N_KV_HEADS
ROPE_FREQS = HEAD_DIM // 4
QKV_DIM = (N_Q_HEADS + 2 * N_KV_HEADS) * HEAD_DIM
D_RNN = D_MODEL
RNN_BLOCK_W = 256
RNN_BLOCKS = D_RNN // RNN_BLOCK_W
CONV_W = 4
CONV_LEFT = 2
LRU_C = 8.0
D_FF = 5632
EPS = 1e-6
Q_SCALE = HEAD_DIM ** -0.5 * 1.4426950408889634

SUBLANES = 8
VMEM_LIMIT_BYTES = 56 * 1024 * 1024

TM = 512
TN = 512
TF = 512
TQ = 512
TK = 1024
TT = 512
VT_ROWS = HEAD_DIM + 16
EXP_ROWS = 32

def _params(*sem):
    return pltpu.CompilerParams(dimension_semantics=sem, vmem_limit_bytes=VMEM_LIMIT_BYTES)


def _rms_normed(x, g):
    ms = jnp.mean(x * x, axis=-1, keepdims=True)
    return x * lax.rsqrt(ms + EPS) * g


def _norm_rope(y, gain, cos, sin):
    yn = _rms_normed(y, gain)
    lane = lax.broadcasted_iota(jnp.int32, yn.shape, 1)
    first_half = (lane & (2 * ROPE_FREQS - 1)) < ROPE_FREQS
    partner = jnp.where(first_half,
                        pltpu.roll(yn, HEAD_DIM - ROPE_FREQS, 1),
                        pltpu.roll(yn, ROPE_FREQS, 1))
    return yn * cos + partner * sin


def _qkv_kernel(x_ref, g_ref, w_ref, qg_ref, kg_ref, cos_ref, sin_ref, o_ref, hn_ref):
    j = pl.program_id(1)

    @pl.when(j == 0)
    def _():
        hn_ref[...] = _rms_normed(x_ref[...], g_ref[...]).astype(BF16)

    y = jnp.dot(hn_ref[...], w_ref[...], preferred_element_type=F32)
    heads_per_tile = TN // HEAD_DIM
    n_q_tiles = N_Q_HEADS // heads_per_tile

    @pl.when(j < n_q_tiles)
    def _():
        for h in range(heads_per_tile):
            sl = slice(h * HEAD_DIM, (h + 1) * HEAD_DIM)
            q = _norm_rope(y[:, sl], qg_ref[...], cos_ref[...], sin_ref[...])
            o_ref[:, sl] = (q * Q_SCALE).astype(BF16)

    @pl.when(j == n_q_tiles)
    def _():
        for h in range(heads_per_tile):
            sl = slice(h * HEAD_DIM, (h + 1) * HEAD_DIM)
            o_ref[:, sl] = _norm_rope(y[:, sl], kg_ref[...], cos_ref[...], sin_ref[...]).astype(BF16)

    @pl.when(j > n_q_tiles)
    def _():
        o_ref[...] = y.astype(BF16)


def _qkv_proj(x, g, w, q_gain, k_gain, cos_t, sin_t):
    s = x.shape[0]
    return pl.pallas_call(
        _qkv_kernel,
        grid=(s // TM, QKV_DIM // TN),
        in_specs=[
            pl.BlockSpec((TM, D_MODEL), lambda i, j: (i, 0)),
            pl.BlockSpec((1, D_MODEL), lambda i, j: (0, 0)),
            pl.BlockSpec((D_MODEL, TN), lambda i, j: (0, j)),
            pl.BlockSpec((1, HEAD_DIM), lambda i, j: (0, 0)),
            pl.BlockSpec((1, HEAD_DIM), lambda i, j: (0, 0)),
            pl.BlockSpec((TM, HEAD_DIM), lambda i, j: (i, 0)),
            pl.BlockSpec((TM, HEAD_DIM), lambda i, j: (i, 0)),
        ],
        out_specs=pl.BlockSpec((TM, TN), lambda i, j: (i, j)),
        out_shape=jax.ShapeDtypeStruct((s, QKV_DIM), BF16),
        scratch_shapes=[pltpu.VMEM((TM, D_MODEL), BF16)],
        compiler_params=_params("arbitrary", "arbitrary"),
        name="qkv_proj",
    )(x, g, w, q_gain, k_gain, cos_t, sin_t)


def _attn_kernel(q_ref, k_ref, v_ref, o_ref, vt_ref, acc_ref, sa_ref, sb_ref, pa_ref, pb_ref):
    n_chunks = k_ref.shape[0] // TK

    @pl.when(pl.program_id(1) == 0)
    def _():
        for c in range(n_chunks):
            vc = v_ref[c * TK:(c + 1) * TK, :].astype(F32)
            vt_ref[c, :HEAD_DIM, :] = vc.T.astype(BF16)
            vt_ref[c, HEAD_DIM:, :] = jnp.ones((VT_ROWS - HEAD_DIM, TK), BF16)

    acc_ref[...] = jnp.zeros_like(acc_ref)
    heads = range(GQA_GROUP)

    def score_stage(c, st_ref):
        kc = k_ref[pl.ds(pl.multiple_of(c * TK, TK), TK), :]
        chunk_max = []
        for g in heads:
            q = q_ref[:, g * HEAD_DIM:(g + 1) * HEAD_DIM]
            st = lax.dot_general(kc, q, (((1,), (1,)), ((), ())), preferred_element_type=F32)
            st_ref[g] = st
            chunk_max.append(jnp.max(st, axis=0, keepdims=True))
        return tuple(chunk_max)

    def exp_stage(st_ref, pt_ref, ms, chunk_max):
        new_ms, alphas = [], []
        for g in heads:
            m_new = jnp.maximum(ms[g], chunk_max[g])
            for r in range(0, TK, EXP_ROWS):
                pt_ref[g, r:r + EXP_ROWS, :] = jnp.exp2(st_ref[g, r:r + EXP_ROWS, :] - m_new).astype(BF16)
            new_ms.append(m_new)
            alphas.append(jnp.exp2(ms[g] - m_new))
        return tuple(new_ms), tuple(alphas)

    def out_stage(c, pt_ref, alphas):
        for g in heads:
            acc_ref[g] = alphas[g] * acc_ref[g] + jnp.dot(vt_ref[c], pt_ref[g], preferred_element_type=F32)

    def pair(t, carry):
        ms, alphas, chunk_max = carry
        c = 2 * t
        out_stage(c, pa_ref, alphas)
        ms, alphas = exp_stage(sb_ref, pb_ref, ms, chunk_max)
        chunk_max = score_stage(c + 2, sa_ref)
        out_stage(c + 1, pb_ref, alphas)
        ms, alphas = exp_stage(sa_ref, pa_ref, ms, chunk_max)
        chunk_max = score_stage(c + 3, sb_ref)
        return ms, alphas, chunk_max

    ms = (jnp.full((1, TQ), -jnp.inf, F32),) * GQA_GROUP
    chunk_max = score_stage(0, sa_ref)
    ms, alphas = exp_stage(sa_ref, pa_ref, ms, chunk_max)
    chunk_max = score_stage(1, sb_ref)
    ms, alphas, chunk_max = lax.fori_loop(0, n_chunks // 2 - 1, pair, (ms, alphas, chunk_max))
    out_stage(n_chunks - 2, pa_ref, alphas)
    ms, alphas = exp_stage(sb_ref, pb_ref, ms, chunk_max)
    out_stage(n_chunks - 1, pb_ref, alphas)
    for g in heads:
        acc = acc_ref[g]
        out_t = acc[:HEAD_DIM, :] / acc[HEAD_DIM:HEAD_DIM + 1, :]
        o_ref[:, g * HEAD_DIM:(g + 1) * HEAD_DIM] = out_t.T.astype(BF16)


def _attention(qkv):
    s = qkv.shape[0]
    group_w = GQA_GROUP * HEAD_DIM
    return pl.pallas_call(
        _attn_kernel,
        grid=(N_KV_HEADS, s // TQ),
        in_specs=[
            pl.BlockSpec((TQ, group_w), lambda h, i: (i, h)),
            pl.BlockSpec((s, HEAD_DIM), lambda h, i: (0, N_Q_HEADS + h)),
            pl.BlockSpec((s, HEAD_DIM), lambda h, i: (0, N_Q_HEADS + N_KV_HEADS + h)),
        ],
        out_specs=pl.BlockSpec((TQ, group_w), lambda h, i: (i, h)),
        out_shape=jax.ShapeDtypeStruct((s, N_Q_HEADS * HEAD_DIM), BF16),
        scratch_shapes=[pltpu.VMEM((s // TK, VT_ROWS, TK), BF16),
                        pltpu.VMEM((GQA_GROUP, VT_ROWS, TQ), F32),
                        pltpu.VMEM((GQA_GROUP, TK, TQ), F32),
                        pltpu.VMEM((GQA_GROUP, TK, TQ), F32),
                        pltpu.VMEM((GQA_GROUP, TK, TQ), BF16),
                        pltpu.VMEM((GQA_GROUP, TK, TQ), BF16)],
        compiler_params=_params("arbitrary", "arbitrary"),
        name="attention",
    )(qkv, qkv, qkv)


def _proj_res_kernel(a_ref, w_ref, x_ref, o_ref):
    o_ref[...] = x_ref[...] + jnp.dot(a_ref[...], w_ref[...], preferred_element_type=F32)


def _proj_residual(a, w, x):
    s, k = a.shape
    return pl.pallas_call(
        _proj_res_kernel,
        grid=(s // TM, D_MODEL // TN),
        in_specs=[
            pl.BlockSpec((TM, k), lambda i, j: (i, 0)),
            pl.BlockSpec((k, TN), lambda i, j: (0, j)),
            pl.BlockSpec((TM, TN), lambda i, j: (i, j)),
        ],
        out_specs=pl.BlockSpec((TM, TN), lambda i, j: (i, j)),
        out_shape=jax.ShapeDtypeStruct((s, D_MODEL), F32),
        compiler_params=_params("arbitrary", "arbitrary"),
        name="proj_residual",
    )(a, w, x)


def _rnn_in_kernel(x_ref, g_ref, wx_ref, wy_ref, xb_ref, yb_ref, hn_ref):
    @pl.when(pl.program_id(1) == 0)
    def _():
        hn_ref[...] = _rms_normed(x_ref[...], g_ref[...]).astype(BF16)

    hn = hn_ref[...]
    xb_ref[...] = jnp.dot(hn, wx_ref[...], preferred_element_type=F32)
    yb_ref[...] = jax.nn.gelu(jnp.dot(hn, wy_ref[...], preferred_element_type=F32), approximate=True)


def _rnn_in_proj(x, g, w_in):
    s = x.shape[0]
    n_col = D_RNN // TN
    return pl.pallas_call(
        _rnn_in_kernel,
        grid=(s // TM, n_col),
        in_specs=[
            pl.BlockSpec((TM, D_MODEL), lambda i, j: (i, 0)),
            pl.BlockSpec((1, D_MODEL), lambda i, j: (0, 0)),
            pl.BlockSpec((D_MODEL, TN), lambda i, j: (0, j)),
            pl.BlockSpec((D_MODEL, TN), lambda i, j: (0, j + n_col)),
        ],
        out_specs=[
            pl.BlockSpec((TM, TN), lambda i, j: (i, j)),
            pl.BlockSpec((TM, TN), lambda i, j: (i, j)),
        ],
        out_shape=[jax.ShapeDtypeStruct((s, D_RNN), F32), jax.ShapeDtypeStruct((s, D_RNN), F32)],
        scratch_shapes=[pltpu.VMEM((TM, D_MODEL), BF16)],
        compiler_params=_params("arbitrary", "arbitrary"),
        name="rnn_in_proj",
    )(x, g, w_in, w_in)


def _softplus(z):
    return jnp.maximum(z, 0.0) + jnp.log1p(jnp.exp(-jnp.abs(z)))


def _lru_inputs(d, cur_ref, prev_ref, next_ref, first, last, cw_ref, cb_ref, wa_ref, ba_ref, wi_ref,
                bi_ref, lam_ref, ext_ref, a_ref, u_ref):
    ext_ref[d, 0:SUBLANES, :] = jnp.where(first, 0.0, prev_ref[...])
    ext_ref[d, SUBLANES:SUBLANES + TT, :] = cur_ref[...]
    ext_ref[d, SUBLANES + TT:2 * SUBLANES + TT, :] = jnp.where(last, 0.0, next_ref[...])
    xc = cb_ref[...]
    for k in range(CONV_W):
        r0 = SUBLANES + k - CONV_LEFT
        xc = xc + cw_ref[k:k + 1, :] * ext_ref[d, r0:r0 + TT, :]
    xb16 = xc.astype(BF16)
    r = jax.nn.sigmoid(jnp.dot(xb16, wa_ref[d, 0], preferred_element_type=F32) + ba_ref[d:d + 1, :])
    i = jax.nn.sigmoid(jnp.dot(xb16, wi_ref[d, 0], preferred_element_type=F32) + bi_ref[d:d + 1, :])
    log_a = (-LRU_C * r) * _softplus(-lam_ref[d:d + 1, :])
    a = jnp.exp(log_a)
    a_ref[d] = a
    u_ref[d] = jnp.sqrt(-jnp.tanh(log_a) * (a * a + 1.0)) * (i * xc)


def _rnn_core_kernel(fc_ref, fp_ref, fn_ref, bc_ref, bp_ref, bn_ref, cw_ref, cb_ref, wa_ref, ba_ref,
                     wi_ref, bi_ref, lam_ref, hf_ref, hb_ref, ext_ref, a_ref, u_ref, carry_ref):
    i = pl.program_id(1)
    n = pl.num_programs(1)

    @pl.when(i == 0)
    def _():
        carry_ref[...] = jnp.zeros_like(carry_ref)

    shared = (cw_ref, cb_ref, wa_ref, ba_ref, wi_ref, bi_ref, lam_ref, ext_ref, a_ref, u_ref)
    _lru_inputs(0, fc_ref, fp_ref, fn_ref, i == 0, i == n - 1, *shared)
    _lru_inputs(1, bc_ref, bp_ref, bn_ref, i == n - 1, i == 0, *shared)

    n_groups = TT // SUBLANES
    row = lax.broadcasted_iota(jnp.int32, (SUBLANES, RNN_BLOCK_W), 0)

    def body(k, carry):
        h_prev, h_next = carry
        r0 = pl.multiple_of(k * SUBLANES, SUBLANES)
        a = a_ref[0, pl.ds(r0, SUBLANES), :]
        u = u_ref[0, pl.ds(r0, SUBLANES), :]
        for step in (1, 2, 4):
            keep = row >= step
            u = u + a * jnp.where(keep, pltpu.roll(u, step, 0), 0.0)
            a = a * jnp.where(keep, pltpu.roll(a, step, 0), 1.0)
        hf = u + a * h_prev
        hf_ref[pl.ds(r0, SUBLANES), :] = hf
        q0 = pl.multiple_of((n_groups - 1 - k) * SUBLANES, SUBLANES)
        a = a_ref[1, pl.ds(q0, SUBLANES), :]
        u = u_ref[1, pl.ds(q0, SUBLANES), :]
        for step in (1, 2, 4):
            keep = row < SUBLANES - step
            u = u + a * jnp.where(keep, pltpu.roll(u, SUBLANES - step, 0), 0.0)
            a = a * jnp.where(keep, pltpu.roll(a, SUBLANES - step, 0), 1.0)
        hb = u + a * h_next
        hb_ref[pl.ds(q0, SUBLANES), :] = hb
        return (jnp.broadcast_to(hf[SUBLANES - 1:SUBLANES, :], hf.shape),
                jnp.broadcast_to(hb[0:1, :], hb.shape))

    h_prev, h_next = lax.fori_loop(0, n_groups, body, (carry_ref[0], carry_ref[1]))
    carry_ref[0] = h_prev
    carry_ref[1] = h_next


def _rnn_core(xb, conv_w, conv_b, w_a, b_a, w_i, b_i, lam):
    s = xb.shape[0]
    n = s // TT
    halo = TT // SUBLANES
    last_halo = s // SUBLANES - 1
    cur = lambda t: pl.BlockSpec((TT, RNN_BLOCK_W), lambda c, i: (t(i, n), c))
    prv = lambda t: pl.BlockSpec((SUBLANES, RNN_BLOCK_W),
                                 lambda c, i: (jnp.maximum(t(i, n) * halo - 1, 0), c))
    nxt = lambda t: pl.BlockSpec((SUBLANES, RNN_BLOCK_W),
                                 lambda c, i: (jnp.minimum((t(i, n) + 1) * halo, last_halo), c))
    fwd = lambda i, n: i
    bwd = lambda i, n: n - 1 - i
    chan = lambda rows: pl.BlockSpec((rows, RNN_BLOCK_W), lambda c, i: (0, c))
    gate_w = pl.BlockSpec((2, 1, RNN_BLOCK_W, RNN_BLOCK_W), lambda c, i: (0, c, 0, 0))
    return pl.pallas_call(
        _rnn_core_kernel,
        grid=(RNN_BLOCKS, n),
        in_specs=[cur(fwd), prv(fwd), nxt(fwd), cur(bwd), prv(bwd), nxt(bwd),
                  chan(CONV_W), chan(1), gate_w, chan(2), gate_w, chan(2), chan(2)],
        out_specs=[pl.BlockSpec((TT, RNN_BLOCK_W), lambda c, i: (i, c)),
                   pl.BlockSpec((TT, RNN_BLOCK_W), lambda c, i: (n - 1 - i, c))],
        out_shape=[jax.ShapeDtypeStruct((s, D_RNN), F32), jax.ShapeDtypeStruct((s, D_RNN), F32)],
        scratch_shapes=[
            pltpu.VMEM((2, TT + 2 * SUBLANES, RNN_BLOCK_W), F32),
            pltpu.VMEM((2, TT, RNN_BLOCK_W), F32),
            pltpu.VMEM((2, TT, RNN_BLOCK_W), F32),
            pltpu.VMEM((2, SUBLANES, RNN_BLOCK_W), F32),
        ],
        compiler_params=_params("arbitrary", "arbitrary"),
        name="rnn_core",
    )(xb, xb, xb, xb, xb, xb, conv_w, conv_b, w_a, b_a, w_i, b_i, lam)


def _rnn_out_kernel(hf_ref, hb_ref, yb_ref, w_ref, x_ref, o_ref, a_ref):
    @pl.when(pl.program_id(1) == 0)
    def _():
        a_ref[...] = ((hf_ref[...] + hb_ref[...]) * yb_ref[...]).astype(BF16)

    o_ref[...] = x_ref[...] + jnp.dot(a_ref[...], w_ref[...], preferred_element_type=F32)


def _rnn_out_proj(hf, hb, yb, w, x):
    s = x.shape[0]
    row = pl.BlockSpec((TM, D_RNN), lambda i, j: (i, 0))
    return pl.pallas_call(
        _rnn_out_kernel,
        grid=(s // TM, D_MODEL // TN),
        in_specs=[row, row, row,
                  pl.BlockSpec((D_RNN, TN), lambda i, j: (0, j)),
                  pl.BlockSpec((TM, TN), lambda i, j: (i, j))],
        out_specs=pl.BlockSpec((TM, TN), lambda i, j: (i, j)),
        out_shape=jax.ShapeDtypeStruct((s, D_MODEL), F32),
        scratch_shapes=[pltpu.VMEM((TM, D_RNN), BF16)],
        compiler_params=_params("arbitrary", "arbitrary"),
        name="rnn_out_proj",
    )(hf, hb, yb, w, x)


def _ffn_kernel(x_ref, g_ref, wg_ref, wu_ref, wd_ref, o_ref, hn_ref):
    @pl.when(pl.program_id(1) == 0)
    def _():
        x = x_ref[...]
        hn_ref[...] = _rms_normed(x, g_ref[...]).astype(BF16)
        o_ref[...] = x

    hn = hn_ref[...]
    gate = jnp.dot(hn, wg_ref[...], preferred_element_type=F32)
    up = jnp.dot(hn, wu_ref[...], preferred_element_type=F32)
    act = (jax.nn.silu(gate) * up).astype(BF16)
    o_ref[...] += jnp.dot(act, wd_ref[...], preferred_element_type=F32)


def _ffn(x, g, w_gate, w_up, w_down):
    s = x.shape[0]
    return pl.pallas_call(
        _ffn_kernel,
        grid=(s // TM, D_FF // TF),
        in_specs=[
            pl.BlockSpec((TM, D_MODEL), lambda i, f: (i, 0)),
            pl.BlockSpec((1, D_MODEL), lambda i, f: (0, 0)),
            pl.BlockSpec((D_MODEL, TF), lambda i, f: (0, f)),
            pl.BlockSpec((D_MODEL, TF), lambda i, f: (0, f)),
            pl.BlockSpec((TF, D_MODEL), lambda i, f: (f, 0)),
        ],
        out_specs=pl.BlockSpec((TM, D_MODEL), lambda i, f: (i, 0)),
        out_shape=jax.ShapeDtypeStruct((s, D_MODEL), F32),
        scratch_shapes=[pltpu.VMEM((TM, D_MODEL), BF16)],
        compiler_params=_params("arbitrary", "arbitrary"),
        name="ffn",
    )(x, g, w_gate, w_up, w_down)


def _rope_tables(seq_len):
    rows_n = seq_len // GRID_W
    freqs = ROPE_THETA ** (-jnp.arange(ROPE_FREQS, dtype=F32) / ROPE_FREQS)
    row_ang = jnp.arange(rows_n, dtype=F32)[:, None, None] * freqs
    col_ang = jnp.arange(GRID_W, dtype=F32)[None, :, None] * freqs
    shape = (rows_n, GRID_W, ROPE_FREQS)
    row_ang = jnp.broadcast_to(row_ang, shape).reshape(seq_len, ROPE_FREQS)
    col_ang = jnp.broadcast_to(col_ang, shape).reshape(seq_len, ROPE_FREQS)
    cr, sr, cc, sc = jnp.cos(row_ang), jnp.sin(row_ang), jnp.cos(col_ang), jnp.sin(col_ang)
    return (jnp.concatenate([cr, cr, cc, cc], axis=-1),
            jnp.concatenate([-sr, sr, -sc, sc], axis=-1))


def kernel(x, norm_mix, norm_ffn, attn_w_qkv, attn_q_gain, attn_k_gain, attn_w_o, rnn_w_in, rnn_conv_w, rnn_conv_b, rnn_w_a, rnn_b_a, rnn_w_i, rnn_b_i, rnn_lambda, rnn_w_out, ffn_w_gate, ffn_w_up, ffn_w_down):
    b, s, d = x.shape
    assert (b, d) == (1, D_MODEL) and s % max(TM, TQ, TK, TT) == 0
    cos_t, sin_t = _rope_tables(s)
    h = x.reshape(s, d)
    for i in range(DEPTH):
        j = i // N_MIXERS
        g_mix = norm_mix[i].reshape(1, d)
        if i % N_MIXERS == 0:
            qkv = _qkv_proj(h, g_mix, attn_w_qkv[j].astype(BF16),
                            attn_q_gain[j].reshape(1, HEAD_DIM), attn_k_gain[j].reshape(1, HEAD_DIM),
                            cos_t, sin_t)
            o = _attention(qkv)
            h = _proj_residual(o, attn_w_o[j].astype(BF16), h)
        else:
            xb, yb = _rnn_in_proj(h, g_mix, rnn_w_in[j].astype(BF16))
            hf, hb = _rnn_core(xb, rnn_conv_w[j], rnn_conv_b[j].reshape(1, D_RNN),
                               rnn_w_a[j].astype(BF16), rnn_b_a[j], rnn_w_i[j].astype(BF16),
                               rnn_b_i[j], rnn_lambda[j])
            h = _rnn_out_proj(hf, hb, yb, rnn_w_out[j].astype(BF16), h)
        h = _ffn(h, norm_ffn[i].reshape(1, d), ffn_w_gate[i].astype(BF16),
                 ffn_w_up[i].astype(BF16), ffn_w_down[i].astype(BF16))
    return h.reshape(b, s, d)
```

```python
import functools

import jax
import jax.numpy as jnp
from jax import lax
from jax.experimental import pallas as pl
from jax.experimental.pallas import tpu as pltpu

F32 = jnp.float32
BF16 = jnp.bfloat16

D_MODEL = 2048
DEPTH = 4
N_MIXERS = 2
GRID_W = 64
ROPE_THETA = 10000.0
HEAD_DIM = 128
N_Q_HEADS = 16
N_KV_HEADS = 4
GQA_GROUP = N_Q_HEADS // N_KV_HEADS
ROPE_FREQS = HEAD_DIM // 4
QKV_DIM = (N_Q_HEADS + 2 * N_KV_HEADS) * HEAD_DIM
D_RNN = D_MODEL
RNN_BLOCK_W = 256
RNN_BLOCKS = D_RNN // RNN_BLOCK_W
CONV_W = 4
CONV_LEFT = 2
LRU_C = 8.0
D_FF = 5632
EPS = 1e-6
Q_SCALE = HEAD_DIM ** -0.5 * 1.4426950408889634

SUBLANES = 8
VMEM_LIMIT_BYTES = 56 * 1024 * 1024

TM = 512
TM_OUT = 256
TN = 512
TF = 512
TQ = 512
TK = 1024
TT = 512
VT_ROWS = HEAD_DIM + 16
EXP_ROWS = 32

def _params(*sem):
    return pltpu.CompilerParams(dimension_semantics=sem, vmem_limit_bytes=VMEM_LIMIT_BYTES)


def _rms_normed(x, g):
    ms = jnp.mean(x * x, axis=-1, keepdims=True)
    return x * lax.rsqrt(ms + EPS) * g


def _norm_rope(y, gain, cos, sin):
    yn = _rms_normed(y, gain)
    lane = lax.broadcasted_iota(jnp.int32, yn.shape, 1)
    first_half = (lane & (2 * ROPE_FREQS - 1)) < ROPE_FREQS
    partner = jnp.where(first_half,
                        pltpu.roll(yn, HEAD_DIM - ROPE_FREQS, 1),
                        pltpu.roll(yn, ROPE_FREQS, 1))
    return yn * cos + partner * sin


def _qkv_kernel(x_ref, g_ref, w_ref, qg_ref, kg_ref, cos_ref, sin_ref, o_ref, hn_ref):
    j = pl.program_id(1)

    @pl.when(j == 0)
    def _():
        hn_ref[...] = _rms_normed(x_ref[...], g_ref[...]).astype(BF16)

    y = jnp.dot(hn_ref[...], w_ref[...], preferred_element_type=F32)
    heads_per_tile = TN // HEAD_DIM
    n_q_tiles = N_Q_HEADS // heads_per_tile

    @pl.when(j < n_q_tiles)
    def _():
        for h in range(heads_per_tile):
            sl = slice(h * HEAD_DIM, (h + 1) * HEAD_DIM)
            q = _norm_rope(y[:, sl], qg_ref[...], cos_ref[...], sin_ref[...])
            o_ref[:, sl] = (q * Q_SCALE).astype(BF16)

    @pl.when(j == n_q_tiles)
    def _():
        for h in range(heads_per_tile):
            sl = slice(h * HEAD_DIM, (h + 1) * HEAD_DIM)
            o_ref[:, sl] = _norm_rope(y[:, sl], kg_ref[...], cos_ref[...], sin_ref[...]).astype(BF16)

    @pl.when(j > n_q_tiles)
    def _():
        o_ref[...] = y.astype(BF16)


def _qkv_proj(x, g, w_all, layer, q_gain, k_gain, cos_t, sin_t):
    s = x.shape[0]
    return pl.pallas_call(
        _qkv_kernel,
        grid=(s // TM, QKV_DIM // TN),
        in_specs=[
            pl.BlockSpec((TM, D_MODEL), lambda i, j: (i, 0)),
            pl.BlockSpec((1, D_MODEL), lambda i, j: (0, 0)),
            pl.BlockSpec((None, D_MODEL, TN), lambda i, j: (layer, 0, j)),
            pl.BlockSpec((1, HEAD_DIM), lambda i, j: (0, 0)),
            pl.BlockSpec((1, HEAD_DIM), lambda i, j: (0, 0)),
            pl.BlockSpec((TM, HEAD_DIM), lambda i, j: (i, 0)),
            pl.BlockSpec((TM, HEAD_DIM), lambda i, j: (i, 0)),
        ],
        out_specs=pl.BlockSpec((TM, TN), lambda i, j: (i, j)),
        out_shape=jax.ShapeDtypeStruct((s, QKV_DIM), BF16),
        scratch_shapes=[pltpu.VMEM((TM, D_MODEL), BF16)],
        compiler_params=_params("arbitrary", "arbitrary"),
        name="qkv_proj",
    )(x, g, w_all, q_gain, k_gain, cos_t, sin_t)


def _attn_kernel(q_ref, k_ref, v_ref, o_ref, vt_ref, acc_ref, sa_ref, sb_ref, pa_ref, pb_ref):
    n_chunks = k_ref.shape[0] // TK

    @pl.when(pl.program_id(1) == 0)
    def _():
        for c in range(n_chunks):
            vc = v_ref[c * TK:(c + 1) * TK, :].astype(F32)
            vt_ref[c, :HEAD_DIM, :] = vc.T.astype(BF16)
            vt_ref[c, HEAD_DIM:, :] = jnp.ones((VT_ROWS - HEAD_DIM, TK), BF16)

    acc_ref[...] = jnp.zeros_like(acc_ref)
    heads = range(GQA_GROUP)

    def score_stage(c, st_ref):
        kc = k_ref[pl.ds(pl.multiple_of(c * TK, TK), TK), :]
        chunk_max = []
        for g in heads:
            q = q_ref[:, g * HEAD_DIM:(g + 1) * HEAD_DIM]
            st = lax.dot_general(kc, q, (((1,), (1,)), ((), ())), preferred_element_type=F32)
            st_ref[g] = st
            chunk_max.append(jnp.max(st, axis=0, keepdims=True))
        return tuple(chunk_max)

    def exp_stage(st_ref, pt_ref, ms, chunk_max):
        new_ms, alphas = [], []
        for g in heads:
            m_new = jnp.maximum(ms[g], chunk_max[g])
            for r in range(0, TK, EXP_ROWS):
                pt_ref[g, r:r + EXP_ROWS, :] = jnp.exp2(st_ref[g, r:r + EXP_ROWS, :] - m_new).astype(BF16)
            new_ms.append(m_new)
            alphas.append(jnp.exp2(ms[g] - m_new))
        return tuple(new_ms), tuple(alphas)

    def out_stage(c, pt_ref, alphas):
        for g in heads:
            acc_ref[g] = alphas[g] * acc_ref[g] + jnp.dot(vt_ref[c], pt_ref[g], preferred_element_type=F32)

    def trip(c, carry, s_cur, p_cur, s_nxt, p_nxt):
        ms, alphas, chunk_max = carry
        out_stage(c, p_cur, alphas)
        ms, alphas = exp_stage(s_nxt, p_nxt, ms, chunk_max)
        chunk_max = score_stage(c + 2, s_cur)
        return ms, alphas, chunk_max

    def step(c, carry):
        return lax.cond(c % 2 == 0,
                        lambda cr: trip(c, cr, sa_ref, pa_ref, sb_ref, pb_ref),
                        lambda cr: trip(c, cr, sb_ref, pb_ref, sa_ref, pa_ref),
                        carry)

    ms = (jnp.full((1, TQ), -jnp.inf, F32),) * GQA_GROUP
    chunk_max = score_stage(0, sa_ref)
    ms, alphas = exp_stage(sa_ref, pa_ref, ms, chunk_max)
    chunk_max = score_stage(1, sb_ref)
    ms, alphas, chunk_max = lax.fori_loop(0, n_chunks - 2, step, (ms, alphas, chunk_max))
    out_stage(n_chunks - 2, pa_ref, alphas)
    ms, alphas = exp_stage(sb_ref, pb_ref, ms, chunk_max)
    out_stage(n_chunks - 1, pb_ref, alphas)
    for g in heads:
        acc = acc_ref[g]
        out_t = acc[:HEAD_DIM, :] / acc[HEAD_DIM:HEAD_DIM + 1, :]
        o_ref[:, g * HEAD_DIM:(g + 1) * HEAD_DIM] = out_t.T.astype(BF16)


def _attention(qkv):
    s = qkv.shape[0]
    group_w = GQA_GROUP * HEAD_DIM
    return pl.pallas_call(
        _attn_kernel,
        grid=(N_KV_HEADS, s // TQ),
        in_specs=[
            pl.BlockSpec((TQ, group_w), lambda h, i: (i, h)),
            pl.BlockSpec((s, HEAD_DIM), lambda h, i: (0, N_Q_HEADS + h)),
            pl.BlockSpec((s, HEAD_DIM), lambda h, i: (0, N_Q_HEADS + N_KV_HEADS + h)),
        ],
        out_specs=pl.BlockSpec((TQ, group_w), lambda h, i: (i, h)),
        out_shape=jax.ShapeDtypeStruct((s, N_Q_HEADS * HEAD_DIM), BF16),
        scratch_shapes=[pltpu.VMEM((s // TK, VT_ROWS, TK), BF16),
                        pltpu.VMEM((GQA_GROUP, VT_ROWS, TQ), F32),
                        pltpu.VMEM((GQA_GROUP, TK, TQ), F32),
                        pltpu.VMEM((GQA_GROUP, TK, TQ), F32),
                        pltpu.VMEM((GQA_GROUP, TK, TQ), BF16),
                        pltpu.VMEM((GQA_GROUP, TK, TQ), BF16)],
        compiler_params=_params("arbitrary", "arbitrary"),
        name="attention",
    )(qkv, qkv, qkv)


def _proj_res_kernel(a_ref, w_ref, x_ref, o_ref):
    o_ref[...] = x_ref[...] + jnp.dot(a_ref[...], w_ref[...], preferred_element_type=F32)


def _proj_residual(a, w_all, layer, x):
    s, k = a.shape
    return pl.pallas_call(
        _proj_res_kernel,
        grid=(s // TM,),
        in_specs=[
            pl.BlockSpec((TM, k), lambda i: (i, 0)),
            pl.BlockSpec((None, k, D_MODEL), lambda i: (layer, 0, 0), pipeline_mode=pl.Buffered(1)),
            pl.BlockSpec((TM, D_MODEL), lambda i: (i, 0)),
        ],
        out_specs=pl.BlockSpec((TM, D_MODEL), lambda i: (i, 0)),
        out_shape=jax.ShapeDtypeStruct((s, D_MODEL), F32),
        compiler_params=_params("arbitrary"),
        name="proj_residual",
    )(a, w_all, x)


def _rnn_in_kernel(x_ref, g_ref, wx_ref, wy_ref, xb_ref, yb_ref, hn_ref):
    @pl.when(pl.program_id(1) == 0)
    def _():
        hn_ref[...] = _rms_normed(x_ref[...], g_ref[...]).astype(BF16)

    hn = hn_ref[...]
    xb_ref[...] = jnp.dot(hn, wx_ref[...], preferred_element_type=F32)
    yb_ref[...] = jax.nn.gelu(jnp.dot(hn, wy_ref[...], preferred_element_type=F32), approximate=True)


def _rnn_in_proj(x, g, w_all, layer):
    s = x.shape[0]
    n_col = D_RNN // TN
    return pl.pallas_call(
        _rnn_in_kernel,
        grid=(s // TM, n_col),
        in_specs=[
            pl.BlockSpec((TM, D_MODEL), lambda i, j: (i, 0)),
            pl.BlockSpec((1, D_MODEL), lambda i, j: (0, 0)),
            pl.BlockSpec((None, D_MODEL, TN), lambda i, j: (layer, 0, j)),
            pl.BlockSpec((None, D_MODEL, TN), lambda i, j: (layer, 0, j + n_col)),
        ],
        out_specs=[
            pl.BlockSpec((TM, TN), lambda i, j: (i, j)),
            pl.BlockSpec((TM, TN), lambda i, j: (i, j)),
        ],
        out_shape=[jax.ShapeDtypeStruct((s, D_RNN), F32), jax.ShapeDtypeStruct((s, D_RNN), F32)],
        scratch_shapes=[pltpu.VMEM((TM, D_MODEL), BF16)],
        compiler_params=_params("arbitrary", "arbitrary"),
        name="rnn_in_proj",
    )(x, g, w_all, w_all)


def _sigmoid(z):
    return 0.5 * jnp.tanh(0.5 * z) + 0.5


def _softplus(z):
    return jnp.maximum(z, 0.0) + jnp.log1p(jnp.exp(-jnp.abs(z)))


def _lru_inputs(d, cur_ref, prev_ref, next_ref, first, last, cw_ref, cb_ref, wa_ref, ba_ref, wi_ref,
                bi_ref, lam_ref, ext_ref, a_ref, u_ref):
    ext_ref[d, 0:SUBLANES, :] = jnp.where(first, 0.0, prev_ref[...])
    ext_ref[d, SUBLANES:SUBLANES + TT, :] = cur_ref[...]
    ext_ref[d, SUBLANES + TT:2 * SUBLANES + TT, :] = jnp.where(last, 0.0, next_ref[...])
    ext = ext_ref[d]
    xc = cb_ref[...]
    for k in range(CONV_W):
        shift = (CONV_LEFT - k) % ext.shape[0]
        tap = ext if shift == 0 else pltpu.roll(ext, shift, 0)
        xc = xc + cw_ref[k:k + 1, :] * tap[SUBLANES:SUBLANES + TT, :]
    xb16 = xc.astype(BF16)
    r = _sigmoid(jnp.dot(xb16, wa_ref[d, 0], preferred_element_type=F32) + ba_ref[d:d + 1, :])
    i = _sigmoid(jnp.dot(xb16, wi_ref[d, 0], preferred_element_type=F32) + bi_ref[d:d + 1, :])
    log_a = r * (-LRU_C * _softplus(-lam_ref[d:d + 1, :]))
    a = jnp.exp(log_a)
    a_ref[d] = a
    one_minus_a2 = -jnp.tanh(log_a) * (a * a + 1.0)
    root = jnp.where(one_minus_a2 > 0.0, one_minus_a2 * lax.rsqrt(one_minus_a2), 0.0)
    u_ref[d] = root * (i * xc)


def _rnn_core_kernel(fc_ref, fp_ref, fn_ref, bc_ref, bp_ref, bn_ref, cw_ref, cb_ref, wa_ref, ba_ref,
                     wi_ref, bi_ref, lam_ref, hf_ref, hb_ref, ext_ref, a_ref, u_ref, carry_ref):
    i = pl.program_id(1)
    n = pl.num_programs(1)

    @pl.when(i == 0)
    def _():
        carry_ref[...] = jnp.zeros_like(carry_ref)

    shared = (cw_ref, cb_ref, wa_ref, ba_ref, wi_ref, bi_ref, lam_ref, ext_ref, a_ref, u_ref)
    _lru_inputs(0, fc_ref, fp_ref, fn_ref, i == 0, i == n - 1, *shared)
    _lru_inputs(1, bc_ref, bp_ref, bn_ref, i == n - 1, i == 0, *shared)

    n_groups = TT // SUBLANES
    row = lax.broadcasted_iota(jnp.int32, (SUBLANES, RNN_BLOCK_W), 0)

    def body(k, carry):
        h_prev, h_next = carry
        r0 = pl.multiple_of(k * SUBLANES, SUBLANES)
        a = a_ref[0, pl.ds(r0, SUBLANES), :]
        u = u_ref[0, pl.ds(r0, SUBLANES), :]
        for step in (1, 2, 4):
            keep = row >= step
            u = u + a * jnp.where(keep, pltpu.roll(u, step, 0), 0.0)
            a = a * jnp.where(keep, pltpu.roll(a, step, 0), 1.0)
        hf = u + a * h_prev
        hf_ref[pl.ds(r0, SUBLANES), :] = hf
        q0 = pl.multiple_of((n_groups - 1 - k) * SUBLANES, SUBLANES)
        a = a_ref[1, pl.ds(q0, SUBLANES), :]
        u = u_ref[1, pl.ds(q0, SUBLANES), :]
        for step in (1, 2, 4):
            keep = row < SUBLANES - step
            u = u + a * jnp.where(keep, pltpu.roll(u, SUBLANES - step, 0), 0.0)
            a = a * jnp.where(keep, pltpu.roll(a, SUBLANES - step, 0), 1.0)
        hb = u + a * h_next
        hb_ref[pl.ds(q0, SUBLANES), :] = hb
        return (jnp.broadcast_to(hf[SUBLANES - 1:SUBLANES, :], hf.shape),
                jnp.broadcast_to(hb[0:1, :], hb.shape))

    h_prev, h_next = lax.fori_loop(0, n_groups, body, (carry_ref[0], carry_ref[1]))
    carry_ref[0] = h_prev
    carry_ref[1] = h_next


def _rnn_core(xb, conv_w, conv_b, w_a, b_a, w_i, b_i, lam):
    s = xb.shape[0]
    n = s // TT
    halo = TT // SUBLANES
    last_halo = s // SUBLANES - 1
    cur = lambda t: pl.BlockSpec((TT, RNN_BLOCK_W), lambda c, i: (t(i, n), c))
    prv = lambda t: pl.BlockSpec((SUBLANES, RNN_BLOCK_W),
                                 lambda c, i: (jnp.maximum(t(i, n) * halo - 1, 0), c))
    nxt = lambda t: pl.BlockSpec((SUBLANES, RNN_BLOCK_W),
                                 lambda c, i: (jnp.minimum((t(i, n) + 1) * halo, last_halo), c))
    fwd = lambda i, n: i
    bwd = lambda i, n: n - 1 - i
    chan = lambda rows: pl.BlockSpec((rows, RNN_BLOCK_W), lambda c, i: (0, c))
    gate_w = pl.BlockSpec((2, 1, RNN_BLOCK_W, RNN_BLOCK_W), lambda c, i: (0, c, 0, 0))
    return pl.pallas_call(
        _rnn_core_kernel,
        grid=(RNN_BLOCKS, n),
        in_specs=[cur(fwd), prv(fwd), nxt(fwd), cur(bwd), prv(bwd), nxt(bwd),
                  chan(CONV_W), chan(1), gate_w, chan(2), gate_w, chan(2), chan(2)],
        out_specs=[pl.BlockSpec((TT, RNN_BLOCK_W), lambda c, i: (i, c)),
                   pl.BlockSpec((TT, RNN_BLOCK_W), lambda c, i: (n - 1 - i, c))],
        out_shape=[jax.ShapeDtypeStruct((s, D_RNN), F32), jax.ShapeDtypeStruct((s, D_RNN), F32)],
        scratch_shapes=[
            pltpu.VMEM((2, TT + 2 * SUBLANES, RNN_BLOCK_W), F32),
            pltpu.VMEM((2, TT, RNN_BLOCK_W), F32),
            pltpu.VMEM((2, TT, RNN_BLOCK_W), F32),
            pltpu.VMEM((2, SUBLANES, RNN_BLOCK_W), F32),
        ],
        compiler_params=_params("arbitrary", "arbitrary"),
        name="rnn_core",
    )(xb, xb, xb, xb, xb, xb, conv_w, conv_b, w_a, b_a, w_i, b_i, lam)


def _rnn_out_kernel(hf_ref, hb_ref, yb_ref, w_ref, x_ref, o_ref):
    a = ((hf_ref[...] + hb_ref[...]) * yb_ref[...]).astype(BF16)
    o_ref[...] = x_ref[...] + jnp.dot(a, w_ref[...], preferred_element_type=F32)


def _rnn_out_proj(hf, hb, yb, w_all, layer, x):
    s = x.shape[0]
    row = pl.BlockSpec((TM_OUT, D_RNN), lambda i: (i, 0))
    return pl.pallas_call(
        _rnn_out_kernel,
        grid=(s // TM_OUT,),
        in_specs=[row, row, row,
                  pl.BlockSpec((None, D_RNN, D_MODEL), lambda i: (layer, 0, 0),
                               pipeline_mode=pl.Buffered(1)),
                  pl.BlockSpec((TM_OUT, D_MODEL), lambda i: (i, 0))],
        out_specs=pl.BlockSpec((TM_OUT, D_MODEL), lambda i: (i, 0)),
        out_shape=jax.ShapeDtypeStruct((s, D_MODEL), F32),
        compiler_params=_params("arbitrary"),
        name="rnn_out_proj",
    )(hf, hb, yb, w_all, x)


def _ffn_kernel(x_ref, g_ref, wg_ref, wu_ref, wd_ref, o_ref, hn_ref):
    @pl.when(pl.program_id(1) == 0)
    def _():
        x = x_ref[...]
        hn_ref[...] = _rms_normed(x, g_ref[...]).astype(BF16)
        o_ref[...] = x

    hn = hn_ref[...]
    gate = jnp.dot(hn, wg_ref[...], preferred_element_type=F32)
    up = jnp.dot(hn, wu_ref[...], preferred_element_type=F32)
    act = (jax.nn.silu(gate) * up).astype(BF16)
    o_ref[...] += jnp.dot(act, wd_ref[...], preferred_element_type=F32)


def _ffn(x, g, w_gate, w_up, w_down, layer):
    s = x.shape[0]
    return pl.pallas_call(
        _ffn_kernel,
        grid=(s // TM, D_FF // TF),
        in_specs=[
            pl.BlockSpec((TM, D_MODEL), lambda i, f: (i, 0)),
            pl.BlockSpec((1, D_MODEL), lambda i, f: (0, 0)),
            pl.BlockSpec((None, D_MODEL, TF), lambda i, f: (layer, 0, f)),
            pl.BlockSpec((None, D_MODEL, TF), lambda i, f: (layer, 0, f)),
            pl.BlockSpec((None, TF, D_MODEL), lambda i, f: (layer, f, 0)),
        ],
        out_specs=pl.BlockSpec((TM, D_MODEL), lambda i, f: (i, 0)),
        out_shape=jax.ShapeDtypeStruct((s, D_MODEL), F32),
        scratch_shapes=[pltpu.VMEM((TM, D_MODEL), BF16)],
        compiler_params=_params("arbitrary", "arbitrary"),
        name="ffn",
    )(x, g, w_gate, w_up, w_down)


def _rope_tables(seq_len):
    rows_n = seq_len // GRID_W
    freqs = ROPE_THETA ** (-jnp.arange(ROPE_FREQS, dtype=F32) / ROPE_FREQS)
    row_ang = jnp.arange(rows_n, dtype=F32)[:, None, None] * freqs
    col_ang = jnp.arange(GRID_W, dtype=F32)[None, :, None] * freqs
    shape = (rows_n, GRID_W, ROPE_FREQS)
    row_ang = jnp.broadcast_to(row_ang, shape).reshape(seq_len, ROPE_FREQS)
    col_ang = jnp.broadcast_to(col_ang, shape).reshape(seq_len, ROPE_FREQS)
    cr, sr, cc, sc = jnp.cos(row_ang), jnp.sin(row_ang), jnp.cos(col_ang), jnp.sin(col_ang)
    return (jnp.concatenate([cr, cr, cc, cc], axis=-1),
            jnp.concatenate([-sr, sr, -sc, sc], axis=-1))


def kernel(x, norm_mix, norm_ffn, attn_w_qkv, attn_q_gain, attn_k_gain, attn_w_o, rnn_w_in, rnn_conv_w, rnn_conv_b, rnn_w_a, rnn_b_a, rnn_w_i, rnn_b_i, rnn_lambda, rnn_w_out, ffn_w_gate, ffn_w_up, ffn_w_down):
    b, s, d = x.shape
    assert (b, d) == (1, D_MODEL) and s % max(TM, TQ, TK, TT) == 0
    cos_t, sin_t = _rope_tables(s)
    w_qkv, w_o = attn_w_qkv.astype(BF16), attn_w_o.astype(BF16)
    w_in, w_out = rnn_w_in.astype(BF16), rnn_w_out.astype(BF16)
    w_a, w_i = rnn_w_a.astype(BF16), rnn_w_i.astype(BF16)
    w_gate, w_up, w_down = ffn_w_gate.astype(BF16), ffn_w_up.astype(BF16), ffn_w_down.astype(BF16)
    h = x.reshape(s, d)
    for i in range(DEPTH):
        j = i // N_MIXERS
        g_mix = norm_mix[i].reshape(1, d)
        if i % N_MIXERS == 0:
            qkv = _qkv_proj(h, g_mix, w_qkv, j,
                            attn_q_gain[j].reshape(1, HEAD_DIM), attn_k_gain[j].reshape(1, HEAD_DIM),
                            cos_t, sin_t)
            o = _attention(qkv)
            h = _proj_residual(o, w_o, j, h)
        else:
            xb, yb = _rnn_in_proj(h, g_mix, w_in, j)
            hf, hb = _rnn_core(xb, rnn_conv_w[j], rnn_conv_b[j].reshape(1, D_RNN),
                               w_a[j], rnn_b_a[j], w_i[j], rnn_b_i[j], rnn_lambda[j])
            h = _rnn_out_proj(hf, hb, yb, w_out, j, h)
        h = _ffn(h, norm_ffn[i].reshape(1, d), w_gate, w_up, w_down, i)
    return h.reshape(b, s, d)
```

```python
import functools

import jax
import jax.numpy as jnp
from jax import lax
from jax.experimental import pallas as pl
from jax.experimental.pallas import tpu as pltpu

F32 = jnp.float32
BF16 = jnp.bfloat16

D_MODEL = 2048
DEPTH = 4
N_MIXERS = 2
GRID_W = 64
ROPE_THETA = 10000.0
HEAD_DIM = 128
N_Q_HEADS = 16
N_KV_HEADS = 4
GQA_GROUP = N_Q_HEADS // N_KV_HEADS
ROPE_FREQS = HEAD_DIM // 4
QKV_DIM = (N_Q_HEADS + 2 * N_KV_HEADS) * HEAD_DIM
D_RNN = D_MODEL
RNN_BLOCK_W = 256
RNN_BLOCKS = D_RNN // RNN_BLOCK_W
CONV_W = 4
CONV_LEFT = 2
LRU_C = 8.0
D_FF = 5632
EPS = 1e-6
Q_SCALE = HEAD_DIM ** -0.5 * 1.4426950408889634

SUBLANES = 8
LANES = 128
VMEM_LIMIT_BYTES = 56 * 1024 * 1024

TM = 512
TM_IN = 1024
TM_OUT = 256
TN = 512
MXU_COLS = 256
QKV_TN = 2 * N_KV_HEADS * HEAD_DIM
QKV_Q_TILES = N_Q_HEADS * HEAD_DIM // QKV_TN
TM_FFN = 1024
TF = 256
TQ = 512
TK = 1024
TT = 512
SCAN_BLOCKS = SUBLANES
SCAN_ROWS = TT // SCAN_BLOCKS
SCAN_PITCH = SCAN_ROWS + SUBLANES
LANE_SLABS = RNN_BLOCK_W // LANES
VT_ROWS = HEAD_DIM + 16
EXP_ROWS = 32

def _params(*sem):
    return pltpu.CompilerParams(dimension_semantics=sem, vmem_limit_bytes=VMEM_LIMIT_BYTES)


def _rms_normed(x, g):
    ms = jnp.mean(x * x, axis=-1, keepdims=True)
    return x * lax.rsqrt(ms + EPS) * g


def _qkv_kernel(x_ref, g_ref, w_ref, qg_ref, kg_ref, cos_ref, sin_ref, o_ref, hn_ref):
    j = pl.program_id(1)

    @pl.when(j == 0)
    def _():
        hn_ref[...] = _rms_normed(x_ref[...], g_ref[...]).astype(BF16)

    last = j == QKV_Q_TILES
    q_gain = qg_ref[...] * Q_SCALE
    gain_qk = jnp.where(last, kg_ref[...], q_gain)
    gain_qv = jnp.where(last, 1.0, q_gain)
    cos_qv = jnp.where(last, 1.0, cos_ref[...])
    sin_qv = jnp.where(last, 0.0, sin_ref[...])
    lane = lax.broadcasted_iota(jnp.int32, (TM, HEAD_DIM), 1)
    first_half = (lane & (2 * ROPE_FREQS - 1)) < ROPE_FREQS
    hn = hn_ref[...]
    for c0 in range(0, QKV_TN, MXU_COLS):
        y = jnp.dot(hn, w_ref[:, c0:c0 + MXU_COLS], preferred_element_type=F32)
        for h0 in range(0, MXU_COLS, HEAD_DIM):
            yh = y[:, h0:h0 + HEAD_DIM]
            inv = lax.rsqrt(jnp.mean(yh * yh, axis=-1, keepdims=True) + EPS)
            if c0 + h0 < N_KV_HEADS * HEAD_DIM:
                yn, cos, sin = yh * inv * gain_qk, cos_ref[...], sin_ref[...]
            else:
                yn, cos, sin = yh * jnp.where(last, 1.0, inv) * gain_qv, cos_qv, sin_qv
            partner = jnp.where(first_half,
                                pltpu.roll(yn, HEAD_DIM - ROPE_FREQS, 1),
                                pltpu.roll(yn, ROPE_FREQS, 1))
            o_ref[:, c0 + h0:c0 + h0 + HEAD_DIM] = (yn * cos + partner * sin).astype(BF16)


def _qkv_proj(x, g, w_all, layer, q_gain, k_gain, cos_t, sin_t):
    s = x.shape[0]
    return pl.pallas_call(
        _qkv_kernel,
        grid=(s // TM, QKV_DIM // QKV_TN),
        in_specs=[
            pl.BlockSpec((TM, D_MODEL), lambda i, j: (i, 0)),
            pl.BlockSpec((1, D_MODEL), lambda i, j: (0, 0)),
            pl.BlockSpec((None, D_MODEL, QKV_TN), lambda i, j: (layer, 0, j)),
            pl.BlockSpec((1, HEAD_DIM), lambda i, j: (0, 0)),
            pl.BlockSpec((1, HEAD_DIM), lambda i, j: (0, 0)),
            pl.BlockSpec((TM, HEAD_DIM), lambda i, j: (i, 0)),
            pl.BlockSpec((TM, HEAD_DIM), lambda i, j: (i, 0)),
        ],
        out_specs=pl.BlockSpec((TM, QKV_TN), lambda i, j: (i, j)),
        out_shape=jax.ShapeDtypeStruct((s, QKV_DIM), BF16),
        scratch_shapes=[pltpu.VMEM((TM, D_MODEL), BF16)],
        compiler_params=_params("arbitrary", "arbitrary"),
        name="qkv_proj",
    )(x, g, w_all, q_gain, k_gain, cos_t, sin_t)


def _attn_kernel(q_ref, k_ref, v_ref, o_ref, vt_ref, acc_ref, sa_ref, sb_ref, pa_ref, pb_ref):
    n_chunks = k_ref.shape[0] // TK

    @pl.when(pl.program_id(1) == 0)
    def _():
        for c in range(n_chunks):
            vc = v_ref[c * TK:(c + 1) * TK, :].astype(F32)
            vt_ref[c, :HEAD_DIM, :] = vc.T.astype(BF16)
            vt_ref[c, HEAD_DIM:, :] = jnp.ones((VT_ROWS - HEAD_DIM, TK), BF16)

    acc_ref[...] = jnp.zeros_like(acc_ref)
    heads = range(GQA_GROUP)

    def score_stage(c, st_ref):
        kc = k_ref[pl.ds(pl.multiple_of(c * TK, TK), TK), :]
        chunk_max = []
        for g in heads:
            q = q_ref[:, g * HEAD_DIM:(g + 1) * HEAD_DIM]
            st = lax.dot_general(kc, q, (((1,), (1,)), ((), ())), preferred_element_type=F32)
            st_ref[g] = st
            chunk_max.append(jnp.max(st, axis=0, keepdims=True))
        return tuple(chunk_max)

    def exp_stage(st_ref, pt_ref, ms, chunk_max):
        new_ms, alphas = [], []
        for g in heads:
            m_new = jnp.maximum(ms[g], chunk_max[g])
            for r in range(0, TK, EXP_ROWS):
                pt_ref[g, r:r + EXP_ROWS, :] = jnp.exp2(st_ref[g, r:r + EXP_ROWS, :] - m_new).astype(BF16)
            new_ms.append(m_new)
            alphas.append(jnp.exp2(ms[g] - m_new))
        return tuple(new_ms), tuple(alphas)

    def out_stage(c, pt_ref, alphas):
        for g in heads:
            acc_ref[g] = alphas[g] * acc_ref[g] + jnp.dot(vt_ref[c], pt_ref[g], preferred_element_type=F32)

    def trip(c, carry, s_cur, p_cur, s_nxt, p_nxt):
        ms, alphas, chunk_max = carry
        out_stage(c, p_cur, alphas)
        ms, alphas = exp_stage(s_nxt, p_nxt, ms, chunk_max)
        chunk_max = score_stage(c + 2, s_cur)
        return ms, alphas, chunk_max

    def step(c, carry):
        return lax.cond(c % 2 == 0,
                        lambda cr: trip(c, cr, sa_ref, pa_ref, sb_ref, pb_ref),
                        lambda cr: trip(c, cr, sb_ref, pb_ref, sa_ref, pa_ref),
                        carry)

    ms = (jnp.full((1, TQ), -jnp.inf, F32),) * GQA_GROUP
    chunk_max = score_stage(0, sa_ref)
    ms, alphas = exp_stage(sa_ref, pa_ref, ms, chunk_max)
    chunk_max = score_stage(1, sb_ref)
    ms, alphas, chunk_max = lax.fori_loop(0, n_chunks - 2, step, (ms, alphas, chunk_max))
    out_stage(n_chunks - 2, pa_ref, alphas)
    ms, alphas = exp_stage(sb_ref, pb_ref, ms, chunk_max)
    out_stage(n_chunks - 1, pb_ref, alphas)
    for g in heads:
        acc = acc_ref[g]
        out_t = acc[:HEAD_DIM, :] / acc[HEAD_DIM:HEAD_DIM + 1, :]
        o_ref[:, g * HEAD_DIM:(g + 1) * HEAD_DIM] = out_t.T.astype(BF16)


def _attention(qkv):
    s = qkv.shape[0]
    group_w = GQA_GROUP * HEAD_DIM
    return pl.pallas_call(
        _attn_kernel,
        grid=(N_KV_HEADS, s // TQ),
        in_specs=[
            pl.BlockSpec((TQ, group_w), lambda h, i: (i, h)),
            pl.BlockSpec((s, HEAD_DIM), lambda h, i: (0, N_Q_HEADS + h)),
            pl.BlockSpec((s, HEAD_DIM), lambda h, i: (0, N_Q_HEADS + N_KV_HEADS + h)),
        ],
        out_specs=pl.BlockSpec((TQ, group_w), lambda h, i: (i, h)),
        out_shape=jax.ShapeDtypeStruct((s, N_Q_HEADS * HEAD_DIM), BF16),
        scratch_shapes=[pltpu.VMEM((s // TK, VT_ROWS, TK), BF16),
                        pltpu.VMEM((GQA_GROUP, VT_ROWS, TQ), F32),
                        pltpu.VMEM((GQA_GROUP, TK, TQ), F32),
                        pltpu.VMEM((GQA_GROUP, TK, TQ), F32),
                        pltpu.VMEM((GQA_GROUP, TK, TQ), BF16),
                        pltpu.VMEM((GQA_GROUP, TK, TQ), BF16)],
        compiler_params=_params("arbitrary", "arbitrary"),
        name="attention",
    )(qkv, qkv, qkv)


def _proj_res_kernel(a_ref, w_ref, x_ref, o_ref):
    o_ref[...] = x_ref[...] + jnp.dot(a_ref[...], w_ref[...], preferred_element_type=F32)


def _proj_residual(a, w_all, layer, x):
    s, k = a.shape
    return pl.pallas_call(
        _proj_res_kernel,
        grid=(s // TM,),
        in_specs=[
            pl.BlockSpec((TM, k), lambda i: (i, 0)),
            pl.BlockSpec((None, k, D_MODEL), lambda i: (layer, 0, 0), pipeline_mode=pl.Buffered(1)),
            pl.BlockSpec((TM, D_MODEL), lambda i: (i, 0)),
        ],
        out_specs=pl.BlockSpec((TM, D_MODEL), lambda i: (i, 0)),
        out_shape=jax.ShapeDtypeStruct((s, D_MODEL), F32),
        compiler_params=_params("arbitrary"),
        name="proj_residual",
    )(a, w_all, x)


def _rnn_in_kernel(x_ref, g_ref, wx_ref, wy_ref, xb_ref, yb_ref, hn_ref):
    @pl.when(pl.program_id(1) == 0)
    def _():
        hn_ref[...] = _rms_normed(x_ref[...], g_ref[...]).astype(BF16)

    hn = hn_ref[...]
    xb_ref[...] = jnp.dot(hn, wx_ref[...], preferred_element_type=F32)
    yb_ref[...] = jax.nn.gelu(jnp.dot(hn, wy_ref[...], preferred_element_type=F32), approximate=True)


def _rnn_in_proj(x, g, w_all, layer):
    s = x.shape[0]
    n_col = D_RNN // TN
    return pl.pallas_call(
        _rnn_in_kernel,
        grid=(s // TM_IN, n_col),
        in_specs=[
            pl.BlockSpec((TM_IN, D_MODEL), lambda i, j: (i, 0)),
            pl.BlockSpec((1, D_MODEL), lambda i, j: (0, 0)),
            pl.BlockSpec((None, D_MODEL, TN), lambda i, j: (layer, 0, j)),
            pl.BlockSpec((None, D_MODEL, TN), lambda i, j: (layer, 0, j + n_col)),
        ],
        out_specs=[
            pl.BlockSpec((TM_IN, TN), lambda i, j: (i, j)),
            pl.BlockSpec((TM_IN, TN), lambda i, j: (i, j)),
        ],
        out_shape=[jax.ShapeDtypeStruct((s, D_RNN), F32), jax.ShapeDtypeStruct((s, D_RNN), F32)],
        scratch_shapes=[pltpu.VMEM((TM_IN, D_MODEL), BF16)],
        compiler_params=_params("arbitrary", "arbitrary"),
        name="rnn_in_proj",
    )(x, g, w_all, w_all)


def _sigmoid(z):
    return 0.5 * jnp.tanh(0.5 * z) + 0.5


def _softplus(z):
    return jnp.maximum(z, 0.0) + jnp.log1p(jnp.exp(-jnp.abs(z)))


def _lru_inputs(d, cur_ref, prev_ref, next_ref, first, last, cw_ref, cb_ref, wa_ref, ba_ref, wi_ref,
                bi_ref, lam_ref, ext_ref, a_ref, u_ref):
    ext_ref[d, 0:SUBLANES, :] = jnp.where(first, 0.0, prev_ref[...])
    ext_ref[d, SUBLANES:SUBLANES + TT, :] = cur_ref[...]
    ext_ref[d, SUBLANES + TT:2 * SUBLANES + TT, :] = jnp.where(last, 0.0, next_ref[...])
    ext = ext_ref[d]
    xc = cb_ref[...]
    for k in range(CONV_W):
        shift = (CONV_LEFT - k) % ext.shape[0]
        tap = ext if shift == 0 else pltpu.roll(ext, shift, 0)
        xc = xc + cw_ref[k:k + 1, :] * tap[SUBLANES:SUBLANES + TT, :]
    xb16 = xc.astype(BF16)
    r = _sigmoid(jnp.dot(xb16, wa_ref[d, 0], preferred_element_type=F32) + ba_ref[d:d + 1, :])
    i = _sigmoid(jnp.dot(xb16, wi_ref[d, 0], preferred_element_type=F32) + bi_ref[d:d + 1, :])
    log_a = r * (-LRU_C * _softplus(-lam_ref[d:d + 1, :]))
    a = jnp.exp(log_a)
    one_minus_a2 = -jnp.tanh(log_a) * (a * a + 1.0)
    root = jnp.where(one_minus_a2 > 0.0, one_minus_a2 * lax.rsqrt(one_minus_a2), 0.0)
    u = root * (i * xc)
    for s in range(SCAN_BLOCKS):
        for l in range(LANE_SLABS):
            rows = slice(s * SCAN_ROWS, (s + 1) * SCAN_ROWS)
            lanes = slice(l * LANES, (l + 1) * LANES)
            a_ref[d, l, s * SCAN_PITCH:s * SCAN_PITCH + SCAN_ROWS, :] = a[rows, lanes]
            u_ref[d, l, s * SCAN_PITCH:s * SCAN_PITCH + SCAN_ROWS, :] = u[rows, lanes]


def _rnn_core_kernel(fc_ref, fp_ref, fn_ref, bc_ref, bp_ref, bn_ref, cw_ref, cb_ref, wa_ref, ba_ref,
                     wi_ref, bi_ref, lam_ref, hf_ref, hb_ref, ext_ref, a_ref, u_ref, hl_ref, p_ref,
                     carry_ref):
    i = pl.program_id(1)
    n = pl.num_programs(1)

    @pl.when(i == 0)
    def _():
        carry_ref[...] = jnp.zeros_like(carry_ref)

    shared = (cw_ref, cb_ref, wa_ref, ba_ref, wi_ref, bi_ref, lam_ref, ext_ref, a_ref, u_ref)
    _lru_inputs(0, fc_ref, fp_ref, fn_ref, i == 0, i == n - 1, *shared)
    _lru_inputs(1, bc_ref, bp_ref, bn_ref, i == n - 1, i == 0, *shared)

    pairs = [(d, l) for d in range(2) for l in range(LANE_SLABS)]

    def body(jj, carry):
        out = []
        for (d, l), (h, p) in zip(pairs, carry):
            j = jj if d == 0 else SCAN_ROWS - 1 - jj
            rows = pl.ds(j, SCAN_BLOCKS, stride=SCAN_PITCH)
            a = a_ref[d, l, rows, :]
            h = a * h + u_ref[d, l, rows, :]
            p = a * p
            hl_ref[d, l, rows, :] = h
            p_ref[d, l, rows, :] = p
            out.append((h, p))
        return tuple(out)

    zero = jnp.zeros((SCAN_BLOCKS, LANES), F32)
    ends = lax.fori_loop(0, SCAN_ROWS, body, ((zero, zero + 1.0),) * len(pairs), unroll=4)

    for (d, l), (h_end, p_end) in zip(pairs, ends):
        lanes = slice(l * LANES, (l + 1) * LANES)
        out_ref = hf_ref if d == 0 else hb_ref
        c = carry_ref[d, 0:1, lanes]
        for s in (range(SCAN_BLOCKS) if d == 0 else reversed(range(SCAN_BLOCKS))):
            blk = slice(s * SCAN_PITCH, s * SCAN_PITCH + SCAN_ROWS)
            out_ref[s * SCAN_ROWS:(s + 1) * SCAN_ROWS, lanes] = hl_ref[d, l, blk, :] + p_ref[d, l, blk, :] * c
            c = h_end[s:s + 1, :] + p_end[s:s + 1, :] * c
        carry_ref[d, :, lanes] = jnp.broadcast_to(c, (SUBLANES, LANES))


def _rnn_core(xb, conv_w, conv_b, w_a, b_a, w_i, b_i, lam):
    s = xb.shape[0]
    n = s // TT
    halo = TT // SUBLANES
    last_halo = s // SUBLANES - 1
    cur = lambda t: pl.BlockSpec((TT, RNN_BLOCK_W), lambda c, i: (t(i, n), c))
    prv = lambda t: pl.BlockSpec((SUBLANES, RNN_BLOCK_W),
                                 lambda c, i: (jnp.maximum(t(i, n) * halo - 1, 0), c))
    nxt = lambda t: pl.BlockSpec((SUBLANES, RNN_BLOCK_W),
                                 lambda c, i: (jnp.minimum((t(i, n) + 1) * halo, last_halo), c))
    fwd = lambda i, n: i
    bwd = lambda i, n: n - 1 - i
    chan = lambda rows: pl.BlockSpec((rows, RNN_BLOCK_W), lambda c, i: (0, c))
    gate_w = pl.BlockSpec((2, 1, RNN_BLOCK_W, RNN_BLOCK_W), lambda c, i: (0, c, 0, 0))
    return pl.pallas_call(
        _rnn_core_kernel,
        grid=(RNN_BLOCKS, n),
        in_specs=[cur(fwd), prv(fwd), nxt(fwd), cur(bwd), prv(bwd), nxt(bwd),
                  chan(CONV_W), chan(1), gate_w, chan(2), gate_w, chan(2), chan(2)],
        out_specs=[pl.BlockSpec((TT, RNN_BLOCK_W), lambda c, i: (i, c)),
                   pl.BlockSpec((TT, RNN_BLOCK_W), lambda c, i: (n - 1 - i, c))],
        out_shape=[jax.ShapeDtypeStruct((s, D_RNN), F32), jax.ShapeDtypeStruct((s, D_RNN), F32)],
        scratch_shapes=[
            pltpu.VMEM((2, TT + 2 * SUBLANES, RNN_BLOCK_W), F32),
            pltpu.VMEM((2, LANE_SLABS, SCAN_BLOCKS * SCAN_PITCH, LANES), F32),
            pltpu.VMEM((2, LANE_SLABS, SCAN_BLOCKS * SCAN_PITCH, LANES), F32),
            pltpu.VMEM((2, LANE_SLABS, SCAN_BLOCKS * SCAN_PITCH, LANES), F32),
            pltpu.VMEM((2, LANE_SLABS, SCAN_BLOCKS * SCAN_PITCH, LANES), F32),
            pltpu.VMEM((2, SUBLANES, RNN_BLOCK_W), F32),
        ],
        compiler_params=_params("arbitrary", "arbitrary"),
        name="rnn_core",
    )(xb, xb, xb, xb, xb, xb, conv_w, conv_b, w_a, b_a, w_i, b_i, lam)


def _rnn_out_kernel(hf_ref, hb_ref, yb_ref, w_ref, x_ref, o_ref):
    a = ((hf_ref[...] + hb_ref[...]) * yb_ref[...]).astype(BF16)
    o_ref[...] = x_ref[...] + jnp.dot(a, w_ref[...], preferred_element_type=F32)


def _rnn_out_proj(hf, hb, yb, w_all, layer, x):
    s = x.shape[0]
    row = pl.BlockSpec((TM_OUT, D_RNN), lambda i: (i, 0))
    return pl.pallas_call(
        _rnn_out_kernel,
        grid=(s // TM_OUT,),
        in_specs=[row, row, row,
                  pl.BlockSpec((None, D_RNN, D_MODEL), lambda i: (layer, 0, 0),
                               pipeline_mode=pl.Buffered(1)),
                  pl.BlockSpec((TM_OUT, D_MODEL), lambda i: (i, 0))],
        out_specs=pl.BlockSpec((TM_OUT, D_MODEL), lambda i: (i, 0)),
        out_shape=jax.ShapeDtypeStruct((s, D_MODEL), F32),
        compiler_params=_params("arbitrary"),
        name="rnn_out_proj",
    )(hf, hb, yb, w_all, x)


def _ffn_kernel(x_ref, g_ref, wg_ref, wu_ref, wd_ref, o_ref, hn_ref):
    @pl.when(pl.program_id(1) == 0)
    def _():
        x = x_ref[...]
        hn_ref[...] = _rms_normed(x, g_ref[...]).astype(BF16)
        o_ref[...] = x

    hn = hn_ref[...]
    gate = jnp.dot(hn, wg_ref[...].astype(BF16), preferred_element_type=F32)
    up = jnp.dot(hn, wu_ref[...].astype(BF16), preferred_element_type=F32)
    act = (jax.nn.silu(gate) * up).astype(BF16)
    o_ref[...] += jnp.dot(act, wd_ref[...].astype(BF16), preferred_element_type=F32)


def _ffn(x, g, w_gate, w_up, w_down, layer):
    s = x.shape[0]
    return pl.pallas_call(
        _ffn_kernel,
        grid=(s // TM_FFN, D_FF // TF),
        in_specs=[
            pl.BlockSpec((TM_FFN, D_MODEL), lambda i, f: (i, 0), pipeline_mode=pl.Buffered(1)),
            pl.BlockSpec((1, D_MODEL), lambda i, f: (0, 0)),
            pl.BlockSpec((None, D_MODEL, TF), lambda i, f: (layer, 0, f)),
            pl.BlockSpec((None, D_MODEL, TF), lambda i, f: (layer, 0, f)),
            pl.BlockSpec((None, TF, D_MODEL), lambda i, f: (layer, f, 0)),
        ],
        out_specs=pl.BlockSpec((TM_FFN, D_MODEL), lambda i, f: (i, 0)),
        out_shape=jax.ShapeDtypeStruct((s, D_MODEL), F32),
        scratch_shapes=[pltpu.VMEM((TM_FFN, D_MODEL), BF16)],
        compiler_params=_params("arbitrary", "arbitrary"),
        name="ffn",
    )(x, g, w_gate, w_up, w_down)


def _rope_tables(seq_len):
    rows_n = seq_len // GRID_W
    freqs = ROPE_THETA ** (-jnp.arange(ROPE_FREQS, dtype=F32) / ROPE_FREQS)
    row_ang = jnp.arange(rows_n, dtype=F32)[:, None, None] * freqs
    col_ang = jnp.arange(GRID_W, dtype=F32)[None, :, None] * freqs
    shape = (rows_n, GRID_W, ROPE_FREQS)
    row_ang = jnp.broadcast_to(row_ang, shape).reshape(seq_len, ROPE_FREQS)
    col_ang = jnp.broadcast_to(col_ang, shape).reshape(seq_len, ROPE_FREQS)
    cr, sr, cc, sc = jnp.cos(row_ang), jnp.sin(row_ang), jnp.cos(col_ang), jnp.sin(col_ang)
    return (jnp.concatenate([cr, cr, cc, cc], axis=-1),
            jnp.concatenate([-sr, sr, -sc, sc], axis=-1))


def kernel(x, norm_mix, norm_ffn, attn_w_qkv, attn_q_gain, attn_k_gain, attn_w_o, rnn_w_in, rnn_conv_w, rnn_conv_b, rnn_w_a, rnn_b_a, rnn_w_i, rnn_b_i, rnn_lambda, rnn_w_out, ffn_w_gate, ffn_w_up, ffn_w_down):
    b, s, d = x.shape
    assert (b, d) == (1, D_MODEL) and s % max(TM, TQ, TK, TT) == 0
    cos_t, sin_t = _rope_tables(s)
    w_qkv, w_o = attn_w_qkv.astype(BF16), attn_w_o.astype(BF16)
    w_in, w_out = rnn_w_in.astype(BF16), rnn_w_out.astype(BF16)
    w_a, w_i = rnn_w_a.astype(BF16), rnn_w_i.astype(BF16)
    h = x.reshape(s, d)
    for i in range(DEPTH):
        j = i // N_MIXERS
        g_mix = norm_mix[i].reshape(1, d)
        if i % N_MIXERS == 0:
            qkv = _qkv_proj(h, g_mix, w_qkv, j,
                            attn_q_gain[j].reshape(1, HEAD_DIM), attn_k_gain[j].reshape(1, HEAD_DIM),
                            cos_t, sin_t)
            o = _attention(qkv)
            h = _proj_residual(o, w_o, j, h)
        else:
            xb, yb = _rnn_in_proj(h, g_mix, w_in, j)
            hf, hb = _rnn_core(xb, rnn_conv_w[j], rnn_conv_b[j].reshape(1, D_RNN),
                               w_a[j], rnn_b_a[j], w_i[j], rnn_b_i[j], rnn_lambda[j])
            h = _rnn_out_proj(hf, hb, yb, w_out, j, h)
        h = _ffn(h, norm_ffn[i].reshape(1, d), ffn_w_gate, ffn_w_up, ffn_w_down, i)
    return h.reshape(b, s, d)
```

```python
import functools

import jax
import jax.numpy as jnp
from jax import lax
from jax.experimental import pallas as pl
from jax.experimental.pallas import tpu as pltpu

F32 = jnp.float32
BF16 = jnp.bfloat16

D_MODEL = 2048
DEPTH = 4
N_MIXERS = 2
GRID_W = 64
ROPE_THETA = 10000.0
HEAD_DIM = 128
N_Q_HEADS = 16
N_KV_HEADS = 4
GQA_GROUP = N_Q_HEADS // N_KV_HEADS
ROPE_FREQS = HEAD_DIM // 4
QKV_DIM = (N_Q_HEADS + 2 * N_KV_HEADS) * HEAD_DIM
D_RNN = D_MODEL
RNN_BLOCK_W = 256
RNN_BLOCKS = D_RNN // RNN_BLOCK_W
CONV_W = 4
CONV_LEFT = 2
LRU_C = 8.0
D_FF = 5632
EPS = 1e-6
Q_SCALE = HEAD_DIM ** -0.5 * 1.4426950408889634

SUBLANES = 8
LANES = 128
VMEM_LIMIT_BYTES = 56 * 1024 * 1024

TM = 512
TM_OUT = 256
TN = 512
MXU_COLS = 256
QKV_TN = 2 * N_KV_HEADS * HEAD_DIM
QKV_Q_TILES = N_Q_HEADS * HEAD_DIM // QKV_TN
TM_FFN = 1024
TF = 256
TQ = 512
TK = 1024
TT = 512
SCAN_BLOCKS = SUBLANES
SCAN_ROWS = TT // SCAN_BLOCKS
SCAN_PITCH = SCAN_ROWS + SUBLANES
LANE_SLABS = RNN_BLOCK_W // LANES
VT_ROWS = HEAD_DIM + 16
EXP_ROWS = 32

def _params(*sem):
    return pltpu.CompilerParams(dimension_semantics=sem, vmem_limit_bytes=VMEM_LIMIT_BYTES)


def _rms_normed(x, g):
    ms = jnp.mean(x * x, axis=-1, keepdims=True)
    return x * lax.rsqrt(ms + EPS) * g


def _qkv_kernel(x_ref, g_ref, w_ref, qg_ref, kg_ref, cos_ref, sin_ref, o_ref, hn_ref):
    j = pl.program_id(1)

    @pl.when(j == 0)
    def _():
        hn_ref[...] = _rms_normed(x_ref[...], g_ref[...]).astype(BF16)

    last = j == QKV_Q_TILES
    q_gain = qg_ref[...] * Q_SCALE
    gain_qk = jnp.where(last, kg_ref[...], q_gain)
    gain_qv = jnp.where(last, 1.0, q_gain)
    cos_qv = jnp.where(last, 1.0, cos_ref[...])
    sin_qv = jnp.where(last, 0.0, sin_ref[...])
    lane = lax.broadcasted_iota(jnp.int32, (TM, HEAD_DIM), 1)
    first_half = (lane & (2 * ROPE_FREQS - 1)) < ROPE_FREQS
    hn = hn_ref[...]
    for c0 in range(0, QKV_TN, MXU_COLS):
        y = jnp.dot(hn, w_ref[:, c0:c0 + MXU_COLS], preferred_element_type=F32)
        for h0 in range(0, MXU_COLS, HEAD_DIM):
            yh = y[:, h0:h0 + HEAD_DIM]
            inv = lax.rsqrt(jnp.mean(yh * yh, axis=-1, keepdims=True) + EPS)
            if c0 + h0 < N_KV_HEADS * HEAD_DIM:
                yn, cos, sin = yh * inv * gain_qk, cos_ref[...], sin_ref[...]
            else:
                yn, cos, sin = yh * jnp.where(last, 1.0, inv) * gain_qv, cos_qv, sin_qv
            partner = jnp.where(first_half,
                                pltpu.roll(yn, HEAD_DIM - ROPE_FREQS, 1),
                                pltpu.roll(yn, ROPE_FREQS, 1))
            o_ref[:, c0 + h0:c0 + h0 + HEAD_DIM] = (yn * cos + partner * sin).astype(BF16)


def _qkv_proj(x, g, w_all, layer, q_gain, k_gain, cos_t, sin_t):
    s = x.shape[0]
    return pl.pallas_call(
        _qkv_kernel,
        grid=(s // TM, QKV_DIM // QKV_TN),
        in_specs=[
            pl.BlockSpec((TM, D_MODEL), lambda i, j: (i, 0)),
            pl.BlockSpec((1, D_MODEL), lambda i, j: (0, 0)),
            pl.BlockSpec((None, D_MODEL, QKV_TN), lambda i, j: (layer, 0, j)),
            pl.BlockSpec((1, HEAD_DIM), lambda i, j: (0, 0)),
            pl.BlockSpec((1, HEAD_DIM), lambda i, j: (0, 0)),
            pl.BlockSpec((TM, HEAD_DIM), lambda i, j: (i, 0)),
            pl.BlockSpec((TM, HEAD_DIM), lambda i, j: (i, 0)),
        ],
        out_specs=pl.BlockSpec((TM, QKV_TN), lambda i, j: (i, j)),
        out_shape=jax.ShapeDtypeStruct((s, QKV_DIM), BF16),
        scratch_shapes=[pltpu.VMEM((TM, D_MODEL), BF16)],
        compiler_params=_params("arbitrary", "arbitrary"),
        name="qkv_proj",
    )(x, g, w_all, q_gain, k_gain, cos_t, sin_t)


def _attn_kernel(q_ref, k_ref, v_ref, o_ref, vt_ref, acc_ref, sa_ref, sb_ref, pa_ref, pb_ref):
    n_chunks = k_ref.shape[0] // TK

    @pl.when(pl.program_id(1) == 0)
    def _():
        for c in range(n_chunks):
            vc = v_ref[c * TK:(c + 1) * TK, :].astype(F32)
            vt_ref[c, :HEAD_DIM, :] = vc.T.astype(BF16)
            vt_ref[c, HEAD_DIM:, :] = jnp.ones((VT_ROWS - HEAD_DIM, TK), BF16)

    acc_ref[...] = jnp.zeros_like(acc_ref)
    heads = range(GQA_GROUP)

    def score_stage(c, st_ref):
        kc = k_ref[pl.ds(pl.multiple_of(c * TK, TK), TK), :]
        chunk_max = []
        for g in heads:
            q = q_ref[:, g * HEAD_DIM:(g + 1) * HEAD_DIM]
            st = lax.dot_general(kc, q, (((1,), (1,)), ((), ())), preferred_element_type=F32)
            st_ref[g] = st
            chunk_max.append(jnp.max(st, axis=0, keepdims=True))
        return tuple(chunk_max)

    def exp_stage(st_ref, pt_ref, ms, chunk_max):
        new_ms, alphas = [], []
        for g in heads:
            m_new = jnp.maximum(ms[g], chunk_max[g])
            for r in range(0, TK, EXP_ROWS):
                pt_ref[g, r:r + EXP_ROWS, :] = jnp.exp2(st_ref[g, r:r + EXP_ROWS, :] - m_new).astype(BF16)
            new_ms.append(m_new)
            alphas.append(jnp.exp2(ms[g] - m_new))
        return tuple(new_ms), tuple(alphas)

    def out_stage(c, pt_ref, alphas):
        for g in heads:
            acc_ref[g] = alphas[g] * acc_ref[g] + jnp.dot(vt_ref[c], pt_ref[g], preferred_element_type=F32)

    def trip(c, carry, s_cur, p_cur, s_nxt, p_nxt):
        ms, alphas, chunk_max = carry
        out_stage(c, p_cur, alphas)
        ms, alphas = exp_stage(s_nxt, p_nxt, ms, chunk_max)
        chunk_max = score_stage(c + 2, s_cur)
        return ms, alphas, chunk_max

    def step(c, carry):
        return lax.cond(c % 2 == 0,
                        lambda cr: trip(c, cr, sa_ref, pa_ref, sb_ref, pb_ref),
                        lambda cr: trip(c, cr, sb_ref, pb_ref, sa_ref, pa_ref),
                        carry)

    ms = (jnp.full((1, TQ), -jnp.inf, F32),) * GQA_GROUP
    chunk_max = score_stage(0, sa_ref)
    ms, alphas = exp_stage(sa_ref, pa_ref, ms, chunk_max)
    chunk_max = score_stage(1, sb_ref)
    ms, alphas, chunk_max = lax.fori_loop(0, n_chunks - 2, step, (ms, alphas, chunk_max))
    out_stage(n_chunks - 2, pa_ref, alphas)
    ms, alphas = exp_stage(sb_ref, pb_ref, ms, chunk_max)
    out_stage(n_chunks - 1, pb_ref, alphas)
    for g in heads:
        acc = acc_ref[g]
        out_t = acc[:HEAD_DIM, :] / acc[HEAD_DIM:HEAD_DIM + 1, :]
        o_ref[:, g * HEAD_DIM:(g + 1) * HEAD_DIM] = out_t.T.astype(BF16)


def _attention(qkv):
    s = qkv.shape[0]
    group_w = GQA_GROUP * HEAD_DIM
    return pl.pallas_call(
        _attn_kernel,
        grid=(N_KV_HEADS, s // TQ),
        in_specs=[
            pl.BlockSpec((TQ, group_w), lambda h, i: (i, h)),
            pl.BlockSpec((s, HEAD_DIM), lambda h, i: (0, N_Q_HEADS + h)),
            pl.BlockSpec((s, HEAD_DIM), lambda h, i: (0, N_Q_HEADS + N_KV_HEADS + h)),
        ],
        out_specs=pl.BlockSpec((TQ, group_w), lambda h, i: (i, h)),
        out_shape=jax.ShapeDtypeStruct((s, N_Q_HEADS * HEAD_DIM), BF16),
        scratch_shapes=[pltpu.VMEM((s // TK, VT_ROWS, TK), BF16),
                        pltpu.VMEM((GQA_GROUP, VT_ROWS, TQ), F32),
                        pltpu.VMEM((GQA_GROUP, TK, TQ), F32),
                        pltpu.VMEM((GQA_GROUP, TK, TQ), F32),
                        pltpu.VMEM((GQA_GROUP, TK, TQ), BF16),
                        pltpu.VMEM((GQA_GROUP, TK, TQ), BF16)],
        compiler_params=_params("arbitrary", "arbitrary"),
        name="attention",
    )(qkv, qkv, qkv)


def _proj_res_kernel(a_ref, w_ref, x_ref, o_ref):
    o_ref[...] = x_ref[...] + jnp.dot(a_ref[...], w_ref[...], preferred_element_type=F32)


def _proj_residual(a, w_all, layer, x):
    s, k = a.shape
    return pl.pallas_call(
        _proj_res_kernel,
        grid=(s // TM,),
        in_specs=[
            pl.BlockSpec((TM, k), lambda i: (i, 0)),
            pl.BlockSpec((None, k, D_MODEL), lambda i: (layer, 0, 0), pipeline_mode=pl.Buffered(1)),
            pl.BlockSpec((TM, D_MODEL), lambda i: (i, 0)),
        ],
        out_specs=pl.BlockSpec((TM, D_MODEL), lambda i: (i, 0)),
        out_shape=jax.ShapeDtypeStruct((s, D_MODEL), F32),
        compiler_params=_params("arbitrary"),
        name="proj_residual",
    )(a, w_all, x)


def _rnn_in_kernel(x_ref, g_ref, w_ref, xb_ref):
    hn = _rms_normed(x_ref[...], g_ref[...]).astype(BF16)
    xb_ref[...] = jnp.dot(hn, w_ref[...], preferred_element_type=F32)


def _rnn_in_proj(x, g, w_all, layer):
    s = x.shape[0]
    return pl.pallas_call(
        _rnn_in_kernel,
        grid=(s // TM,),
        in_specs=[
            pl.BlockSpec((TM, D_MODEL), lambda i: (i, 0)),
            pl.BlockSpec((1, D_MODEL), lambda i: (0, 0)),
            pl.BlockSpec((None, D_MODEL, D_RNN), lambda i: (layer, 0, 0), pipeline_mode=pl.Buffered(1)),
        ],
        out_specs=pl.BlockSpec((TM, D_RNN), lambda i: (i, 0)),
        out_shape=jax.ShapeDtypeStruct((s, D_RNN), F32),
        compiler_params=_params("arbitrary"),
        name="rnn_in_proj",
    )(x, g, w_all)


def _sigmoid(z):
    return 0.5 * jnp.tanh(0.5 * z) + 0.5


def _softplus(z):
    return jnp.maximum(z, 0.0) + jnp.log1p(jnp.exp(-jnp.abs(z)))


def _lru_inputs(d, cur_ref, prev_ref, next_ref, first, last, cw_ref, cb_ref, wa_ref, ba_ref, wi_ref,
                bi_ref, lam_ref, ext_ref, a_ref, u_ref):
    ext_ref[d, 0:SUBLANES, :] = jnp.where(first, 0.0, prev_ref[...])
    ext_ref[d, SUBLANES:SUBLANES + TT, :] = cur_ref[...]
    ext_ref[d, SUBLANES + TT:2 * SUBLANES + TT, :] = jnp.where(last, 0.0, next_ref[...])
    ext = ext_ref[d]
    xc = cb_ref[...]
    for k in range(CONV_W):
        shift = (CONV_LEFT - k) % ext.shape[0]
        tap = ext if shift == 0 else pltpu.roll(ext, shift, 0)
        xc = xc + cw_ref[k:k + 1, :] * tap[SUBLANES:SUBLANES + TT, :]
    xb16 = xc.astype(BF16)
    r = _sigmoid(jnp.dot(xb16, wa_ref[d, 0], preferred_element_type=F32) + ba_ref[d:d + 1, :])
    i = _sigmoid(jnp.dot(xb16, wi_ref[d, 0], preferred_element_type=F32) + bi_ref[d:d + 1, :])
    log_a = r * (-LRU_C * _softplus(-lam_ref[d:d + 1, :]))
    a = jnp.exp(log_a)
    one_minus_a2 = -jnp.tanh(log_a) * (a * a + 1.0)
    root = jnp.where(one_minus_a2 > 0.0, one_minus_a2 * lax.rsqrt(one_minus_a2), 0.0)
    u = root * (i * xc)
    for s in range(SCAN_BLOCKS):
        for l in range(LANE_SLABS):
            rows = slice(s * SCAN_ROWS, (s + 1) * SCAN_ROWS)
            lanes = slice(l * LANES, (l + 1) * LANES)
            a_ref[d, l, s * SCAN_PITCH:s * SCAN_PITCH + SCAN_ROWS, :] = a[rows, lanes]
            u_ref[d, l, s * SCAN_PITCH:s * SCAN_PITCH + SCAN_ROWS, :] = u[rows, lanes]


def _rnn_core_kernel(fc_ref, fp_ref, fn_ref, bc_ref, bp_ref, bn_ref, cw_ref, cb_ref, wa_ref, ba_ref,
                     wi_ref, bi_ref, lam_ref, hf_ref, hb_ref, ext_ref, a_ref, u_ref, hl_ref, p_ref,
                     carry_ref):
    i = pl.program_id(1)
    n = pl.num_programs(1)

    @pl.when(i == 0)
    def _():
        carry_ref[...] = jnp.zeros_like(carry_ref)

    shared = (cw_ref, cb_ref, wa_ref, ba_ref, wi_ref, bi_ref, lam_ref, ext_ref, a_ref, u_ref)
    _lru_inputs(0, fc_ref, fp_ref, fn_ref, i == 0, i == n - 1, *shared)
    _lru_inputs(1, bc_ref, bp_ref, bn_ref, i == n - 1, i == 0, *shared)

    pairs = [(d, l) for d in range(2) for l in range(LANE_SLABS)]

    def body(jj, carry):
        out = []
        for (d, l), (h, p) in zip(pairs, carry):
            j = jj if d == 0 else SCAN_ROWS - 1 - jj
            rows = pl.ds(j, SCAN_BLOCKS, stride=SCAN_PITCH)
            a = a_ref[d, l, rows, :]
            h = a * h + u_ref[d, l, rows, :]
            p = a * p
            hl_ref[d, l, rows, :] = h
            p_ref[d, l, rows, :] = p
            out.append((h, p))
        return tuple(out)

    zero = jnp.zeros((SCAN_BLOCKS, LANES), F32)
    ends = lax.fori_loop(0, SCAN_ROWS, body, ((zero, zero + 1.0),) * len(pairs), unroll=4)

    for (d, l), (h_end, p_end) in zip(pairs, ends):
        lanes = slice(l * LANES, (l + 1) * LANES)
        out_ref = hf_ref if d == 0 else hb_ref
        c = carry_ref[d, 0:1, lanes]
        for s in (range(SCAN_BLOCKS) if d == 0 else reversed(range(SCAN_BLOCKS))):
            blk = slice(s * SCAN_PITCH, s * SCAN_PITCH + SCAN_ROWS)
            out_ref[s * SCAN_ROWS:(s + 1) * SCAN_ROWS, lanes] = hl_ref[d, l, blk, :] + p_ref[d, l, blk, :] * c
            c = h_end[s:s + 1, :] + p_end[s:s + 1, :] * c
        carry_ref[d, :, lanes] = jnp.broadcast_to(c, (SUBLANES, LANES))


def _rnn_core(xb, conv_w, conv_b, w_a, b_a, w_i, b_i, lam):
    s = xb.shape[0]
    n = s // TT
    halo = TT // SUBLANES
    last_halo = s // SUBLANES - 1
    cur = lambda t: pl.BlockSpec((TT, RNN_BLOCK_W), lambda c, i: (t(i, n), c))
    prv = lambda t: pl.BlockSpec((SUBLANES, RNN_BLOCK_W),
                                 lambda c, i: (jnp.maximum(t(i, n) * halo - 1, 0), c))
    nxt = lambda t: pl.BlockSpec((SUBLANES, RNN_BLOCK_W),
                                 lambda c, i: (jnp.minimum((t(i, n) + 1) * halo, last_halo), c))
    fwd = lambda i, n: i
    bwd = lambda i, n: n - 1 - i
    chan = lambda rows: pl.BlockSpec((rows, RNN_BLOCK_W), lambda c, i: (0, c))
    gate_w = pl.BlockSpec((2, 1, RNN_BLOCK_W, RNN_BLOCK_W), lambda c, i: (0, c, 0, 0))
    return pl.pallas_call(
        _rnn_core_kernel,
        grid=(RNN_BLOCKS, n),
        in_specs=[cur(fwd), prv(fwd), nxt(fwd), cur(bwd), prv(bwd), nxt(bwd),
                  chan(CONV_W), chan(1), gate_w, chan(2), gate_w, chan(2), chan(2)],
        out_specs=[pl.BlockSpec((TT, RNN_BLOCK_W), lambda c, i: (i, c)),
                   pl.BlockSpec((TT, RNN_BLOCK_W), lambda c, i: (n - 1 - i, c))],
        out_shape=[jax.ShapeDtypeStruct((s, D_RNN), F32), jax.ShapeDtypeStruct((s, D_RNN), F32)],
        scratch_shapes=[
            pltpu.VMEM((2, TT + 2 * SUBLANES, RNN_BLOCK_W), F32),
            pltpu.VMEM((2, LANE_SLABS, SCAN_BLOCKS * SCAN_PITCH, LANES), F32),
            pltpu.VMEM((2, LANE_SLABS, SCAN_BLOCKS * SCAN_PITCH, LANES), F32),
            pltpu.VMEM((2, LANE_SLABS, SCAN_BLOCKS * SCAN_PITCH, LANES), F32),
            pltpu.VMEM((2, LANE_SLABS, SCAN_BLOCKS * SCAN_PITCH, LANES), F32),
            pltpu.VMEM((2, SUBLANES, RNN_BLOCK_W), F32),
        ],
        compiler_params=_params("arbitrary", "arbitrary"),
        name="rnn_core",
    )(xb, xb, xb, xb, xb, xb, conv_w, conv_b, w_a, b_a, w_i, b_i, lam)


def _rnn_out_kernel(x_ref, g_ref, hf_ref, hb_ref, wy_ref, wo_ref, o_ref):
    x = x_ref[...]
    hn = _rms_normed(x, g_ref[...]).astype(BF16)
    gate = jax.nn.gelu(jnp.dot(hn, wy_ref[...], preferred_element_type=F32), approximate=True)
    a = ((hf_ref[...] + hb_ref[...]) * gate).astype(BF16)
    o_ref[...] = x + jnp.dot(a, wo_ref[...], preferred_element_type=F32)


def _rnn_out_proj(x, g, hf, hb, w_in_all, w_out_all, layer):
    s = x.shape[0]
    row = pl.BlockSpec((TM_OUT, D_MODEL), lambda i: (i, 0))
    return pl.pallas_call(
        _rnn_out_kernel,
        grid=(s // TM_OUT,),
        in_specs=[row,
                  pl.BlockSpec((1, D_MODEL), lambda i: (0, 0)),
                  row, row,
                  pl.BlockSpec((None, D_MODEL, D_RNN), lambda i: (layer, 0, 1),
                               pipeline_mode=pl.Buffered(1)),
                  pl.BlockSpec((None, D_RNN, D_MODEL), lambda i: (layer, 0, 0),
                               pipeline_mode=pl.Buffered(1))],
        out_specs=row,
        out_shape=jax.ShapeDtypeStruct((s, D_MODEL), F32),
        compiler_params=_params("arbitrary"),
        name="rnn_out_proj",
    )(x, g, hf, hb, w_in_all, w_out_all)


def _ffn_kernel(x_ref, g_ref, wg_ref, wu_ref, wd_ref, o_ref, hn_ref):
    @pl.when(pl.program_id(1) == 0)
    def _():
        x = x_ref[...]
        hn_ref[...] = _rms_normed(x, g_ref[...]).astype(BF16)
        o_ref[...] = x

    hn = hn_ref[...]
    gate = jnp.dot(hn, wg_ref[...].astype(BF16), preferred_element_type=F32)
    up = jnp.dot(hn, wu_ref[...].astype(BF16), preferred_element_type=F32)
    act = (jax.nn.silu(gate) * up).astype(BF16)
    o_ref[...] += jnp.dot(act, wd_ref[...].astype(BF16), preferred_element_type=F32)


def _ffn(x, g, w_gate, w_up, w_down, layer):
    s = x.shape[0]
    return pl.pallas_call(
        _ffn_kernel,
        grid=(s // TM_FFN, D_FF // TF),
        in_specs=[
            pl.BlockSpec((TM_FFN, D_MODEL), lambda i, f: (i, 0)),
            pl.BlockSpec((1, D_MODEL), lambda i, f: (0, 0)),
            pl.BlockSpec((None, D_MODEL, TF), lambda i, f: (layer, 0, f)),
            pl.BlockSpec((None, D_MODEL, TF), lambda i, f: (layer, 0, f)),
            pl.BlockSpec((None, TF, D_MODEL), lambda i, f: (layer, f, 0)),
        ],
        out_specs=pl.BlockSpec((TM_FFN, D_MODEL), lambda i, f: (i, 0)),
        out_shape=jax.ShapeDtypeStruct((s, D_MODEL), F32),
        scratch_shapes=[pltpu.VMEM((TM_FFN, D_MODEL), BF16)],
        compiler_params=_params("arbitrary", "arbitrary"),
        name="ffn",
    )(x, g, w_gate, w_up, w_down)


def _rope_tables(seq_len):
    rows_n = seq_len // GRID_W
    freqs = ROPE_THETA ** (-jnp.arange(ROPE_FREQS, dtype=F32) / ROPE_FREQS)
    row_ang = jnp.arange(rows_n, dtype=F32)[:, None, None] * freqs
    col_ang = jnp.arange(GRID_W, dtype=F32)[None, :, None] * freqs
    shape = (rows_n, GRID_W, ROPE_FREQS)
    row_ang = jnp.broadcast_to(row_ang, shape).reshape(seq_len, ROPE_FREQS)
    col_ang = jnp.broadcast_to(col_ang, shape).reshape(seq_len, ROPE_FREQS)
    cr, sr, cc, sc = jnp.cos(row_ang), jnp.sin(row_ang), jnp.cos(col_ang), jnp.sin(col_ang)
    return (jnp.concatenate([cr, cr, cc, cc], axis=-1),
            jnp.concatenate([-sr, sr, -sc, sc], axis=-1))


def kernel(x, norm_mix, norm_ffn, attn_w_qkv, attn_q_gain, attn_k_gain, attn_w_o, rnn_w_in, rnn_conv_w, rnn_conv_b, rnn_w_a, rnn_b_a, rnn_w_i, rnn_b_i, rnn_lambda, rnn_w_out, ffn_w_gate, ffn_w_up, ffn_w_down):
    b, s, d = x.shape
    assert (b, d) == (1, D_MODEL) and s % max(TM, TQ, TK, TT) == 0
    cos_t, sin_t = _rope_tables(s)
    w_qkv, w_o = attn_w_qkv.astype(BF16), attn_w_o.astype(BF16)
    w_in, w_out = rnn_w_in.astype(BF16), rnn_w_out.astype(BF16)
    w_a, w_i = rnn_w_a.astype(BF16), rnn_w_i.astype(BF16)
    h = x.reshape(s, d)
    for i in range(DEPTH):
        j = i // N_MIXERS
        g_mix = norm_mix[i].reshape(1, d)
        if i % N_MIXERS == 0:
            qkv = _qkv_proj(h, g_mix, w_qkv, j,
                            attn_q_gain[j].reshape(1, HEAD_DIM), attn_k_gain[j].reshape(1, HEAD_DIM),
                            cos_t, sin_t)
            o = _attention(qkv)
            h = _proj_residual(o, w_o, j, h)
        else:
            xb = _rnn_in_proj(h, g_mix, w_in, j)
            hf, hb = _rnn_core(xb, rnn_conv_w[j], rnn_conv_b[j].reshape(1, D_RNN),
                               w_a[j], rnn_b_a[j], w_i[j], rnn_b_i[j], rnn_lambda[j])
            h = _rnn_out_proj(h, g_mix, hf, hb, w_in, w_out, j)
        h = _ffn(h, norm_ffn[i].reshape(1, d), ffn_w_gate, ffn_w_up, ffn_w_down, i)
    return h.reshape(b, s, d)
```

```python
import functools

import jax
import jax.numpy as jnp
from jax import lax
from jax.experimental import pallas as pl
from jax.experimental.pallas import tpu as pltpu

F32 = jnp.float32
BF16 = jnp.bfloat16

D_MODEL = 2048
DEPTH = 4
N_MIXERS = 2
GRID_W = 64
ROPE_THETA = 10000.0
HEAD_DIM = 128
N_Q_HEADS = 16
N_KV_HEADS = 4
GQA_GROUP = N_Q_HEADS // N_KV_HEADS
ROPE_FREQS = HEAD_DIM // 4
QKV_DIM = (N_Q_HEADS + 2 * N_KV_HEADS) * HEAD_DIM
D_RNN = D_MODEL
RNN_BLOCK_W = 256
RNN_BLOCKS = D_RNN // RNN_BLOCK_W
CONV_W = 4
CONV_LEFT = 2
LRU_C = 8.0
D_FF = 5632
EPS = 1e-6
Q_SCALE = HEAD_DIM ** -0.5 * 1.4426950408889634

SUBLANES = 8
LANES = 128
VMEM_LIMIT_BYTES = 56 * 1024 * 1024

TM = 512
TM_OUT = 256
TN = 512
MXU_COLS = 256
HEADS_PER_SLAB = MXU_COLS // HEAD_DIM
QKV_SLABS = QKV_DIM // MXU_COLS
QKV_Q_SLABS = N_Q_HEADS // HEADS_PER_SLAB
QKV_K_SLABS = N_KV_HEADS // HEADS_PER_SLAB
TM_FFN = 1024
TF = 256
TQ = 512
TK = 1024
TT = 512
SCAN_BLOCKS = SUBLANES
SCAN_ROWS = TT // SCAN_BLOCKS
SCAN_PITCH = SCAN_ROWS + SUBLANES
LANE_SLABS = RNN_BLOCK_W // LANES
VT_ROWS = HEAD_DIM + 16
EXP_ROWS = 32

def _params(*sem):
    return pltpu.CompilerParams(dimension_semantics=sem, vmem_limit_bytes=VMEM_LIMIT_BYTES)


def _rms_normed(x, g):
    ms = jnp.mean(x * x, axis=-1, keepdims=True)
    return x * lax.rsqrt(ms + EPS) * g


def _qkv_kernel(x_ref, g_ref, w_ref, qg_ref, kg_ref, cos_ref, sin_ref, o_ref, hn_ref, ya_ref, yb_ref):
    hn_ref[...] = _rms_normed(x_ref[...], g_ref[...]).astype(BF16)
    q_gain = qg_ref[...] * Q_SCALE
    lane = lax.broadcasted_iota(jnp.int32, (TM, HEAD_DIM), 1)
    first_half = (lane & (2 * ROPE_FREQS - 1)) < ROPE_FREQS

    def matmul(s, y_ref):
        y_ref[...] = jnp.dot(hn_ref[...], w_ref[s], preferred_element_type=F32)

    def epilogue(s, y_ref):
        is_v = s >= QKV_Q_SLABS + QKV_K_SLABS
        is_k = jnp.logical_and(s >= QKV_Q_SLABS, jnp.logical_not(is_v))
        gain = jnp.where(is_v, 1.0, jnp.where(is_k, kg_ref[...], q_gain))
        cos = jnp.where(is_v, 1.0, cos_ref[...])
        sin = jnp.where(is_v, 0.0, sin_ref[...])
        for h0 in range(0, MXU_COLS, HEAD_DIM):
            yh = y_ref[:, h0:h0 + HEAD_DIM]
            inv = lax.rsqrt(jnp.mean(yh * yh, axis=-1, keepdims=True) + EPS)
            yn = yh * jnp.where(is_v, 1.0, inv) * gain
            partner = jnp.where(first_half,
                                pltpu.roll(yn, HEAD_DIM - ROPE_FREQS, 1),
                                pltpu.roll(yn, ROPE_FREQS, 1))
            o_ref[s, :, h0:h0 + HEAD_DIM] = (yn * cos + partner * sin).astype(BF16)

    def trip(s, y_cur, y_prev):
        epilogue(s - 1, y_prev)
        matmul(s, y_cur)

    def step(s, carry):
        lax.cond(s % 2 == 1, lambda: trip(s, yb_ref, ya_ref), lambda: trip(s, ya_ref, yb_ref))
        return carry

    matmul(0, ya_ref)
    lax.fori_loop(1, QKV_SLABS, step, 0)
    epilogue(QKV_SLABS - 1, yb_ref if (QKV_SLABS - 1) % 2 == 1 else ya_ref)


def _qkv_proj(x, g, w_slabs, layer, q_gain, k_gain, cos_t, sin_t):
    s = x.shape[0]
    return pl.pallas_call(
        _qkv_kernel,
        grid=(s // TM,),
        in_specs=[
            pl.BlockSpec((TM, D_MODEL), lambda i: (i, 0)),
            pl.BlockSpec((1, D_MODEL), lambda i: (0, 0)),
            pl.BlockSpec((None, QKV_SLABS, D_MODEL, MXU_COLS), lambda i: (layer, 0, 0, 0),
                         pipeline_mode=pl.Buffered(1)),
            pl.BlockSpec((1, HEAD_DIM), lambda i: (0, 0)),
            pl.BlockSpec((1, HEAD_DIM), lambda i: (0, 0)),
            pl.BlockSpec((TM, HEAD_DIM), lambda i: (i, 0)),
            pl.BlockSpec((TM, HEAD_DIM), lambda i: (i, 0)),
        ],
        out_specs=pl.BlockSpec((QKV_SLABS, TM, MXU_COLS), lambda i: (0, i, 0)),
        out_shape=jax.ShapeDtypeStruct((QKV_SLABS, s, MXU_COLS), BF16),
        scratch_shapes=[pltpu.VMEM((TM, D_MODEL), BF16),
                        pltpu.VMEM((TM, MXU_COLS), F32),
                        pltpu.VMEM((TM, MXU_COLS), F32)],
        compiler_params=_params("arbitrary"),
        name="qkv_proj",
    )(x, g, w_slabs, q_gain, k_gain, cos_t, sin_t)


def _attn_kernel(q_ref, k_ref, v_ref, o_ref, vt_ref, acc_ref, sa_ref, sb_ref, pa_ref, pb_ref):
    n_chunks = k_ref.shape[0] // TK

    @pl.when(pl.program_id(1) == 0)
    def _():
        for c in range(n_chunks):
            vc = v_ref[c * TK:(c + 1) * TK, :].astype(F32)
            vt_ref[c, :HEAD_DIM, :] = vc.T.astype(BF16)
            vt_ref[c, HEAD_DIM:, :] = jnp.ones((VT_ROWS - HEAD_DIM, TK), BF16)

    acc_ref[...] = jnp.zeros_like(acc_ref)
    heads = range(GQA_GROUP)

    def score_stage(c, st_ref):
        kc = k_ref[pl.ds(pl.multiple_of(c * TK, TK), TK), :]
        chunk_max = []
        for g in heads:
            lane0 = (g % HEADS_PER_SLAB) * HEAD_DIM
            q = q_ref[g // HEADS_PER_SLAB, :, lane0:lane0 + HEAD_DIM]
            st = lax.dot_general(kc, q, (((1,), (1,)), ((), ())), preferred_element_type=F32)
            st_ref[g] = st
            chunk_max.append(jnp.max(st, axis=0, keepdims=True))
        return tuple(chunk_max)

    def exp_stage(st_ref, pt_ref, ms, chunk_max):
        new_ms, alphas = [], []
        for g in heads:
            m_new = jnp.maximum(ms[g], chunk_max[g])
            for r in range(0, TK, EXP_ROWS):
                pt_ref[g, r:r + EXP_ROWS, :] = jnp.exp2(st_ref[g, r:r + EXP_ROWS, :] - m_new).astype(BF16)
            new_ms.append(m_new)
            alphas.append(jnp.exp2(ms[g] - m_new))
        return tuple(new_ms), tuple(alphas)

    def out_stage(c, pt_ref, alphas):
        for g in heads:
            acc_ref[g] = alphas[g] * acc_ref[g] + jnp.dot(vt_ref[c], pt_ref[g], preferred_element_type=F32)

    def trip(c, carry, s_cur, p_cur, s_nxt, p_nxt):
        ms, alphas, chunk_max = carry
        out_stage(c, p_cur, alphas)
        ms, alphas = exp_stage(s_nxt, p_nxt, ms, chunk_max)
        chunk_max = score_stage(c + 2, s_cur)
        return ms, alphas, chunk_max

    def step(c, carry):
        return lax.cond(c % 2 == 0,
                        lambda cr: trip(c, cr, sa_ref, pa_ref, sb_ref, pb_ref),
                        lambda cr: trip(c, cr, sb_ref, pb_ref, sa_ref, pa_ref),
                        carry)

    ms = (jnp.full((1, TQ), -jnp.inf, F32),) * GQA_GROUP
    chunk_max = score_stage(0, sa_ref)
    ms, alphas = exp_stage(sa_ref, pa_ref, ms, chunk_max)
    chunk_max = score_stage(1, sb_ref)
    ms, alphas, chunk_max = lax.fori_loop(0, n_chunks - 2, step, (ms, alphas, chunk_max))
    out_stage(n_chunks - 2, pa_ref, alphas)
    ms, alphas = exp_stage(sb_ref, pb_ref, ms, chunk_max)
    out_stage(n_chunks - 1, pb_ref, alphas)
    for g in heads:
        acc = acc_ref[g]
        out_t = acc[:HEAD_DIM, :] / acc[HEAD_DIM:HEAD_DIM + 1, :]
        o_ref[:, g * HEAD_DIM:(g + 1) * HEAD_DIM] = out_t.T.astype(BF16)


def _attention(qkv):
    s = qkv.shape[1]
    group_w = GQA_GROUP * HEAD_DIM
    return pl.pallas_call(
        _attn_kernel,
        grid=(N_KV_HEADS, s // TQ),
        in_specs=[
            pl.BlockSpec((GQA_GROUP // HEADS_PER_SLAB, TQ, MXU_COLS), lambda h, i: (h, i, 0)),
            pl.BlockSpec((None, s, HEAD_DIM),
                         lambda h, i: (QKV_Q_SLABS + h // HEADS_PER_SLAB, 0, h % HEADS_PER_SLAB)),
            pl.BlockSpec((None, s, HEAD_DIM),
                         lambda h, i: (QKV_Q_SLABS + QKV_K_SLABS + h // HEADS_PER_SLAB, 0,
                                       h % HEADS_PER_SLAB)),
        ],
        out_specs=pl.BlockSpec((TQ, group_w), lambda h, i: (i, h)),
        out_shape=jax.ShapeDtypeStruct((s, N_Q_HEADS * HEAD_DIM), BF16),
        scratch_shapes=[pltpu.VMEM((s // TK, VT_ROWS, TK), BF16),
                        pltpu.VMEM((GQA_GROUP, VT_ROWS, TQ), F32),
                        pltpu.VMEM((GQA_GROUP, TK, TQ), F32),
                        pltpu.VMEM((GQA_GROUP, TK, TQ), F32),
                        pltpu.VMEM((GQA_GROUP, TK, TQ), BF16),
                        pltpu.VMEM((GQA_GROUP, TK, TQ), BF16)],
        compiler_params=_params("arbitrary", "arbitrary"),
        name="attention",
    )(qkv, qkv, qkv)


def _proj_res_kernel(a_ref, w_ref, x_ref, o_ref):
    o_ref[...] = x_ref[...] + jnp.dot(a_ref[...], w_ref[...], preferred_element_type=F32)


def _proj_residual(a, w_all, layer, x):
    s, k = a.shape
    return pl.pallas_call(
        _proj_res_kernel,
        grid=(s // TM,),
        in_specs=[
            pl.BlockSpec((TM, k), lambda i: (i, 0)),
            pl.BlockSpec((None, k, D_MODEL), lambda i: (layer, 0, 0), pipeline_mode=pl.Buffered(1)),
            pl.BlockSpec((TM, D_MODEL), lambda i: (i, 0)),
        ],
        out_specs=pl.BlockSpec((TM, D_MODEL), lambda i: (i, 0)),
        out_shape=jax.ShapeDtypeStruct((s, D_MODEL), F32),
        compiler_params=_params("arbitrary"),
        name="proj_residual",
    )(a, w_all, x)


def _rnn_in_kernel(x_ref, g_ref, w_ref, xb_ref):
    hn = _rms_normed(x_ref[...], g_ref[...]).astype(BF16)
    xb_ref[...] = jnp.dot(hn, w_ref[...], preferred_element_type=F32)


def _rnn_in_proj(x, g, w_all, layer):
    s = x.shape[0]
    return pl.pallas_call(
        _rnn_in_kernel,
        grid=(s // TM,),
        in_specs=[
            pl.BlockSpec((TM, D_MODEL), lambda i: (i, 0)),
            pl.BlockSpec((1, D_MODEL), lambda i: (0, 0)),
            pl.BlockSpec((None, D_MODEL, D_RNN), lambda i: (layer, 0, 0), pipeline_mode=pl.Buffered(1)),
        ],
        out_specs=pl.BlockSpec((TM, D_RNN), lambda i: (i, 0)),
        out_shape=jax.ShapeDtypeStruct((s, D_RNN), F32),
        compiler_params=_params("arbitrary"),
        name="rnn_in_proj",
    )(x, g, w_all)


def _sigmoid(z):
    return 0.5 * jnp.tanh(0.5 * z) + 0.5


def _softplus(z):
    return jnp.maximum(z, 0.0) + jnp.log1p(jnp.exp(-jnp.abs(z)))


def _lru_inputs(d, cur_ref, prev_ref, next_ref, first, last, cw_ref, cb_ref, wa_ref, ba_ref, wi_ref,
                bi_ref, lam_ref, ext_ref, a_ref, u_ref):
    ext_ref[d, 0:SUBLANES, :] = jnp.where(first, 0.0, prev_ref[...])
    ext_ref[d, SUBLANES:SUBLANES + TT, :] = cur_ref[...]
    ext_ref[d, SUBLANES + TT:2 * SUBLANES + TT, :] = jnp.where(last, 0.0, next_ref[...])
    ext = ext_ref[d]
    xc = cb_ref[...]
    for k in range(CONV_W):
        shift = (CONV_LEFT - k) % ext.shape[0]
        tap = ext if shift == 0 else pltpu.roll(ext, shift, 0)
        xc = xc + cw_ref[k:k + 1, :] * tap[SUBLANES:SUBLANES + TT, :]
    xb16 = xc.astype(BF16)
    r = _sigmoid(jnp.dot(xb16, wa_ref[d, 0], preferred_element_type=F32) + ba_ref[d:d + 1, :])
    i = _sigmoid(jnp.dot(xb16, wi_ref[d, 0], preferred_element_type=F32) + bi_ref[d:d + 1, :])
    log_a = r * (-LRU_C * _softplus(-lam_ref[d:d + 1, :]))
    a = jnp.exp(log_a)
    one_minus_a2 = -jnp.tanh(log_a) * (a * a + 1.0)
    root = jnp.where(one_minus_a2 > 0.0, one_minus_a2 * lax.rsqrt(one_minus_a2), 0.0)
    u = root * (i * xc)
    for s in range(SCAN_BLOCKS):
        for l in range(LANE_SLABS):
            rows = slice(s * SCAN_ROWS, (s + 1) * SCAN_ROWS)
            lanes = slice(l * LANES, (l + 1) * LANES)
            a_ref[d, l, s * SCAN_PITCH:s * SCAN_PITCH + SCAN_ROWS, :] = a[rows, lanes]
            u_ref[d, l, s * SCAN_PITCH:s * SCAN_PITCH + SCAN_ROWS, :] = u[rows, lanes]


def _rnn_core_kernel(fc_ref, fp_ref, fn_ref, bc_ref, bp_ref, bn_ref, cw_ref, cb_ref, wa_ref, ba_ref,
                     wi_ref, bi_ref, lam_ref, hf_ref, hb_ref, ext_ref, a_ref, u_ref, hl_ref, p_ref,
                     carry_ref):
    i = pl.program_id(1)
    n = pl.num_programs(1)

    @pl.when(i == 0)
    def _():
        carry_ref[...] = jnp.zeros_like(carry_ref)

    shared = (cw_ref, cb_ref, wa_ref, ba_ref, wi_ref, bi_ref, lam_ref, ext_ref, a_ref, u_ref)
    _lru_inputs(0, fc_ref, fp_ref, fn_ref, i == 0, i == n - 1, *shared)
    _lru_inputs(1, bc_ref, bp_ref, bn_ref, i == n - 1, i == 0, *shared)

    pairs = [(d, l) for d in range(2) for l in range(LANE_SLABS)]

    def body(jj, carry):
        out = []
        for (d, l), (h, p) in zip(pairs, carry):
            j = jj if d == 0 else SCAN_ROWS - 1 - jj
            rows = pl.ds(j, SCAN_BLOCKS, stride=SCAN_PITCH)
            a = a_ref[d, l, rows, :]
            h = a * h + u_ref[d, l, rows, :]
            p = a * p
            hl_ref[d, l, rows, :] = h
            p_ref[d, l, rows, :] = p
            out.append((h, p))
        return tuple(out)

    zero = jnp.zeros((SCAN_BLOCKS, LANES), F32)
    ends = lax.fori_loop(0, SCAN_ROWS, body, ((zero, zero + 1.0),) * len(pairs), unroll=4)

    for (d, l), (h_end, p_end) in zip(pairs, ends):
        lanes = slice(l * LANES, (l + 1) * LANES)
        out_ref = hf_ref if d == 0 else hb_ref
        c = carry_ref[d, 0:1, lanes]
        for s in (range(SCAN_BLOCKS) if d == 0 else reversed(range(SCAN_BLOCKS))):
            blk = slice(s * SCAN_PITCH, s * SCAN_PITCH + SCAN_ROWS)
            out_ref[s * SCAN_ROWS:(s + 1) * SCAN_ROWS, lanes] = hl_ref[d, l, blk, :] + p_ref[d, l, blk, :] * c
            c = h_end[s:s + 1, :] + p_end[s:s + 1, :] * c
        carry_ref[d, :, lanes] = jnp.broadcast_to(c, (SUBLANES, LANES))


def _rnn_core(xb, conv_w, conv_b, w_a, b_a, w_i, b_i, lam):
    s = xb.shape[0]
    n = s // TT
    halo = TT // SUBLANES
    last_halo = s // SUBLANES - 1
    cur = lambda t: pl.BlockSpec((TT, RNN_BLOCK_W), lambda c, i: (t(i, n), c))
    prv = lambda t: pl.BlockSpec((SUBLANES, RNN_BLOCK_W),
                                 lambda c, i: (jnp.maximum(t(i, n) * halo - 1, 0), c))
    nxt = lambda t: pl.BlockSpec((SUBLANES, RNN_BLOCK_W),
                                 lambda c, i: (jnp.minimum((t(i, n) + 1) * halo, last_halo), c))
    fwd = lambda i, n: i
    bwd = lambda i, n: n - 1 - i
    chan = lambda rows: pl.BlockSpec((rows, RNN_BLOCK_W), lambda c, i: (0, c))
    gate_w = pl.BlockSpec((2, 1, RNN_BLOCK_W, RNN_BLOCK_W), lambda c, i: (0, c, 0, 0))
    return pl.pallas_call(
        _rnn_core_kernel,
        grid=(RNN_BLOCKS, n),
        in_specs=[cur(fwd), prv(fwd), nxt(fwd), cur(bwd), prv(bwd), nxt(bwd),
                  chan(CONV_W), chan(1), gate_w, chan(2), gate_w, chan(2), chan(2)],
        out_specs=[pl.BlockSpec((TT, RNN_BLOCK_W), lambda c, i: (i, c)),
                   pl.BlockSpec((TT, RNN_BLOCK_W), lambda c, i: (n - 1 - i, c))],
        out_shape=[jax.ShapeDtypeStruct((s, D_RNN), F32), jax.ShapeDtypeStruct((s, D_RNN), F32)],
        scratch_shapes=[
            pltpu.VMEM((2, TT + 2 * SUBLANES, RNN_BLOCK_W), F32),
            pltpu.VMEM((2, LANE_SLABS, SCAN_BLOCKS * SCAN_PITCH, LANES), F32),
            pltpu.VMEM((2, LANE_SLABS, SCAN_BLOCKS * SCAN_PITCH, LANES), F32),
            pltpu.VMEM((2, LANE_SLABS, SCAN_BLOCKS * SCAN_PITCH, LANES), F32),
            pltpu.VMEM((2, LANE_SLABS, SCAN_BLOCKS * SCAN_PITCH, LANES), F32),
            pltpu.VMEM((2, SUBLANES, RNN_BLOCK_W), F32),
        ],
        compiler_params=_params("arbitrary", "arbitrary"),
        name="rnn_core",
    )(xb, xb, xb, xb, xb, xb, conv_w, conv_b, w_a, b_a, w_i, b_i, lam)


def _rnn_out_kernel(x_ref, g_ref, hf_ref, hb_ref, wy_ref, wo_ref, o_ref):
    x = x_ref[...]
    hn = _rms_normed(x, g_ref[...]).astype(BF16)
    gate = jax.nn.gelu(jnp.dot(hn, wy_ref[...], preferred_element_type=F32), approximate=True)
    a = ((hf_ref[...] + hb_ref[...]) * gate).astype(BF16)
    o_ref[...] = x + jnp.dot(a, wo_ref[...], preferred_element_type=F32)


def _rnn_out_proj(x, g, hf, hb, w_in_all, w_out_all, layer):
    s = x.shape[0]
    row = pl.BlockSpec((TM_OUT, D_MODEL), lambda i: (i, 0))
    return pl.pallas_call(
        _rnn_out_kernel,
        grid=(s // TM_OUT,),
        in_specs=[row,
                  pl.BlockSpec((1, D_MODEL), lambda i: (0, 0)),
                  row, row,
                  pl.BlockSpec((None, D_MODEL, D_RNN), lambda i: (layer, 0, 1),
                               pipeline_mode=pl.Buffered(1)),
                  pl.BlockSpec((None, D_RNN, D_MODEL), lambda i: (layer, 0, 0),
                               pipeline_mode=pl.Buffered(1))],
        out_specs=row,
        out_shape=jax.ShapeDtypeStruct((s, D_MODEL), F32),
        compiler_params=_params("arbitrary"),
        name="rnn_out_proj",
    )(x, g, hf, hb, w_in_all, w_out_all)


def _ffn_kernel(x_ref, g_ref, wg_ref, wu_ref, wd_ref, o_ref, hn_ref):
    @pl.when(pl.program_id(1) == 0)
    def _():
        x = x_ref[...]
        hn_ref[...] = _rms_normed(x, g_ref[...]).astype(BF16)
        o_ref[...] = x

    hn = hn_ref[...]
    gate = jnp.dot(hn, wg_ref[...].astype(BF16), preferred_element_type=F32)
    up = jnp.dot(hn, wu_ref[...].astype(BF16), preferred_element_type=F32)
    act = (jax.nn.silu(gate) * up).astype(BF16)
    o_ref[...] += jnp.dot(act, wd_ref[...].astype(BF16), preferred_element_type=F32)


def _ffn(x, g, w_gate, w_up, w_down, layer):
    s = x.shape[0]
    return pl.pallas_call(
        _ffn_kernel,
        grid=(s // TM_FFN, D_FF // TF),
        in_specs=[
            pl.BlockSpec((TM_FFN, D_MODEL), lambda i, f: (i, 0)),
            pl.BlockSpec((1, D_MODEL), lambda i, f: (0, 0)),
            pl.BlockSpec((None, D_MODEL, TF), lambda i, f: (layer, 0, f)),
            pl.BlockSpec((None, D_MODEL, TF), lambda i, f: (layer, 0, f)),
            pl.BlockSpec((None, TF, D_MODEL), lambda i, f: (layer, f, 0)),
        ],
        out_specs=pl.BlockSpec((TM_FFN, D_MODEL), lambda i, f: (i, 0)),
        out_shape=jax.ShapeDtypeStruct((s, D_MODEL), F32),
        scratch_shapes=[pltpu.VMEM((TM_FFN, D_MODEL), BF16)],
        compiler_params=_params("arbitrary", "arbitrary"),
        name="ffn",
    )(x, g, w_gate, w_up, w_down)


def _rope_tables(seq_len):
    rows_n = seq_len // GRID_W
    freqs = ROPE_THETA ** (-jnp.arange(ROPE_FREQS, dtype=F32) / ROPE_FREQS)
    row_ang = jnp.arange(rows_n, dtype=F32)[:, None, None] * freqs
    col_ang = jnp.arange(GRID_W, dtype=F32)[None, :, None] * freqs
    shape = (rows_n, GRID_W, ROPE_FREQS)
    row_ang = jnp.broadcast_to(row_ang, shape).reshape(seq_len, ROPE_FREQS)
    col_ang = jnp.broadcast_to(col_ang, shape).reshape(seq_len, ROPE_FREQS)
    cr, sr, cc, sc = jnp.cos(row_ang), jnp.sin(row_ang), jnp.cos(col_ang), jnp.sin(col_ang)
    return (jnp.concatenate([cr, cr, cc, cc], axis=-1),
            jnp.concatenate([-sr, sr, -sc, sc], axis=-1))


def kernel(x, norm_mix, norm_ffn, attn_w_qkv, attn_q_gain, attn_k_gain, attn_w_o, rnn_w_in, rnn_conv_w, rnn_conv_b, rnn_w_a, rnn_b_a, rnn_w_i, rnn_b_i, rnn_lambda, rnn_w_out, ffn_w_gate, ffn_w_up, ffn_w_down):
    b, s, d = x.shape
    assert (b, d) == (1, D_MODEL) and s % max(TM, TQ, TK, TT) == 0
    cos_t, sin_t = _rope_tables(s)
    w_o = attn_w_o.astype(BF16)
    w_qkv = attn_w_qkv.astype(BF16).reshape(-1, D_MODEL, QKV_SLABS, MXU_COLS).transpose(0, 2, 1, 3)
    w_in, w_out = rnn_w_in.astype(BF16), rnn_w_out.astype(BF16)
    w_a, w_i = rnn_w_a.astype(BF16), rnn_w_i.astype(BF16)
    h = x.reshape(s, d)
    for i in range(DEPTH):
        j = i // N_MIXERS
        g_mix = norm_mix[i].reshape(1, d)
        if i % N_MIXERS == 0:
            qkv = _qkv_proj(h, g_mix, w_qkv, j,
                            attn_q_gain[j].reshape(1, HEAD_DIM), attn_k_gain[j].reshape(1, HEAD_DIM),
                            cos_t, sin_t)
            o = _attention(qkv)
            h = _proj_residual(o, w_o, j, h)
        else:
            xb = _rnn_in_proj(h, g_mix, w_in, j)
            hf, hb = _rnn_core(xb, rnn_conv_w[j], rnn_conv_b[j].reshape(1, D_RNN),
                               w_a[j], rnn_b_a[j], w_i[j], rnn_b_i[j], rnn_lambda[j])
            h = _rnn_out_proj(h, g_mix, hf, hb, w_in, w_out, j)
        h = _ffn(h, norm_ffn[i].reshape(1, d), ffn_w_gate, ffn_w_up, ffn_w_down, i)
    return h.reshape(b, s, d)
```

```python
import jax
import jax.numpy as jnp
from jax import lax
from jax.experimental import pallas as pl
from jax.experimental.pallas import tpu as pltpu

F32 = jnp.float32
BF16 = jnp.bfloat16

D_MODEL = 2048
DEPTH = 4
N_MIXERS = 2
GRID_W = 64
ROPE_THETA = 10000.0
HEAD_DIM = 128
N_Q_HEADS = 16
N_KV_HEADS = 4
GQA_GROUP = N_Q_HEADS // N_KV_HEADS
ROPE_FREQS = HEAD_DIM // 4
QKV_DIM = (N_Q_HEADS + 2 * N_KV_HEADS) * HEAD_DIM
D_RNN = D_MODEL
RNN_BLOCK_W = 256
RNN_BLOCKS = D_RNN // RNN_BLOCK_W
CONV_W = 4
CONV_LEFT = 2
LRU_C = 8.0
D_FF = 5632
EPS = 1e-6
Q_SCALE = HEAD_DIM ** -0.5 * 1.4426950408889634

SUBLANES = 8
BF16_SUBLANES = 16
LANES = 128
MXU_COLS = 256
VMEM_LIMIT_BYTES = 56 * 1024 * 1024

TM = 512
TM_OUT = 256
HEADS_PER_SLAB = MXU_COLS // HEAD_DIM
QKV_SLABS = QKV_DIM // MXU_COLS
QKV_Q_SLABS = N_Q_HEADS // HEADS_PER_SLAB
QKV_K_SLABS = N_KV_HEADS // HEADS_PER_SLAB
TM_FFN = 1024
TF = 256
TQ = 512
TK = 1024
TT = 512
SCAN_BLOCKS = SUBLANES
SCAN_ROWS = TT // SCAN_BLOCKS
SCAN_PITCH = SCAN_ROWS + SUBLANES
LANE_SLABS = RNN_BLOCK_W // LANES
VT_ROWS = HEAD_DIM + BF16_SUBLANES
EXP_ROWS = 32

def _params(*sem):
    return pltpu.CompilerParams(dimension_semantics=sem, vmem_limit_bytes=VMEM_LIMIT_BYTES)


def _rms_normed(x, g):
    ms = jnp.mean(x * x, axis=-1, keepdims=True)
    return x * lax.rsqrt(ms + EPS) * g


def _qkv_kernel(x_ref, g_ref, w_ref, qg_ref, kg_ref, cos_ref, sin_ref, o_ref, hn_ref, ya_ref, yb_ref):
    hn_ref[...] = _rms_normed(x_ref[...], g_ref[...]).astype(BF16)
    q_gain = qg_ref[...] * Q_SCALE
    lane = lax.broadcasted_iota(jnp.int32, (TM, HEAD_DIM), 1)
    first_half = (lane & (2 * ROPE_FREQS - 1)) < ROPE_FREQS

    def matmul(s, y_ref):
        y_ref[...] = jnp.dot(hn_ref[...], w_ref[s], preferred_element_type=F32)

    def epilogue(s, y_ref):
        is_v = s >= QKV_Q_SLABS + QKV_K_SLABS
        is_k = jnp.logical_and(s >= QKV_Q_SLABS, jnp.logical_not(is_v))
        gain = jnp.where(is_v, 1.0, jnp.where(is_k, kg_ref[...], q_gain))
        cos = jnp.where(is_v, 1.0, cos_ref[...])
        sin = jnp.where(is_v, 0.0, sin_ref[...])
        for h0 in range(0, MXU_COLS, HEAD_DIM):
            yh = y_ref[:, h0:h0 + HEAD_DIM]
            inv = lax.rsqrt(jnp.mean(yh * yh, axis=-1, keepdims=True) + EPS)
            yn = yh * jnp.where(is_v, 1.0, inv) * gain
            partner = jnp.where(first_half,
                                pltpu.roll(yn, HEAD_DIM - ROPE_FREQS, 1),
                                pltpu.roll(yn, ROPE_FREQS, 1))
            o_ref[s, :, h0:h0 + HEAD_DIM] = (yn * cos + partner * sin).astype(BF16)

    def trip(s, y_cur, y_prev):
        epilogue(s - 1, y_prev)
        matmul(s, y_cur)

    def step(s, carry):
        lax.cond(s % 2 == 1, lambda: trip(s, yb_ref, ya_ref), lambda: trip(s, ya_ref, yb_ref))
        return carry

    matmul(0, ya_ref)
    lax.fori_loop(1, QKV_SLABS, step, 0)
    epilogue(QKV_SLABS - 1, yb_ref if (QKV_SLABS - 1) % 2 == 1 else ya_ref)


def _qkv_proj(x, g, w_slabs, layer, q_gain, k_gain, cos_t, sin_t):
    s = x.shape[0]
    return pl.pallas_call(
        _qkv_kernel,
        grid=(s // TM,),
        in_specs=[
            pl.BlockSpec((TM, D_MODEL), lambda i: (i, 0)),
            pl.BlockSpec((1, D_MODEL), lambda i: (0, 0)),
            pl.BlockSpec((None, QKV_SLABS, D_MODEL, MXU_COLS), lambda i: (layer, 0, 0, 0),
                         pipeline_mode=pl.Buffered(1)),
            pl.BlockSpec((1, HEAD_DIM), lambda i: (0, 0)),
            pl.BlockSpec((1, HEAD_DIM), lambda i: (0, 0)),
            pl.BlockSpec((TM, HEAD_DIM), lambda i: (i, 0)),
            pl.BlockSpec((TM, HEAD_DIM), lambda i: (i, 0)),
        ],
        out_specs=pl.BlockSpec((QKV_SLABS, TM, MXU_COLS), lambda i: (0, i, 0)),
        out_shape=jax.ShapeDtypeStruct((QKV_SLABS, s, MXU_COLS), BF16),
        scratch_shapes=[pltpu.VMEM((TM, D_MODEL), BF16),
                        pltpu.VMEM((TM, MXU_COLS), F32),
                        pltpu.VMEM((TM, MXU_COLS), F32)],
        compiler_params=_params("arbitrary"),
        name="qkv_proj",
    )(x, g, w_slabs, q_gain, k_gain, cos_t, sin_t)


def _attn_kernel(q_ref, k_ref, v_ref, o_ref, vt_ref, acc_ref, sa_ref, sb_ref, pa_ref, pb_ref):
    n_chunks = k_ref.shape[0] // TK

    @pl.when(pl.program_id(1) == 0)
    def _():
        for c in range(n_chunks):
            vc = v_ref[c * TK:(c + 1) * TK, :].astype(F32)
            vt_ref[c, :HEAD_DIM, :] = vc.T.astype(BF16)
            vt_ref[c, HEAD_DIM:, :] = jnp.ones((VT_ROWS - HEAD_DIM, TK), BF16)

    acc_ref[...] = jnp.zeros_like(acc_ref)
    heads = range(GQA_GROUP)

    def score_stage(c, st_ref):
        kc = k_ref[pl.ds(pl.multiple_of(c * TK, TK), TK), :]
        chunk_max = []
        for g in heads:
            lane0 = (g % HEADS_PER_SLAB) * HEAD_DIM
            q = q_ref[g // HEADS_PER_SLAB, :, lane0:lane0 + HEAD_DIM]
            st = lax.dot_general(kc, q, (((1,), (1,)), ((), ())), preferred_element_type=F32)
            st_ref[g] = st
            chunk_max.append(jnp.max(st, axis=0, keepdims=True))
        return tuple(chunk_max)

    def exp_stage(st_ref, pt_ref, ms, chunk_max):
        new_ms, alphas = [], []
        for g in heads:
            m_new = jnp.maximum(ms[g], chunk_max[g])
            for r in range(0, TK, EXP_ROWS):
                pt_ref[g, r:r + EXP_ROWS, :] = jnp.exp2(st_ref[g, r:r + EXP_ROWS, :] - m_new).astype(BF16)
            new_ms.append(m_new)
            alphas.append(jnp.exp2(ms[g] - m_new))
        return tuple(new_ms), tuple(alphas)

    def out_stage(c, pt_ref, alphas):
        for g in heads:
            acc_ref[g] = alphas[g] * acc_ref[g] + jnp.dot(vt_ref[c], pt_ref[g], preferred_element_type=F32)

    def trip(c, carry, s_cur, p_cur, s_nxt, p_nxt):
        ms, alphas, chunk_max = carry
        out_stage(c, p_cur, alphas)
        ms, alphas = exp_stage(s_nxt, p_nxt, ms, chunk_max)
        chunk_max = score_stage(c + 2, s_cur)
        return ms, alphas, chunk_max

    def step(c, carry):
        return lax.cond(c % 2 == 0,
                        lambda cr: trip(c, cr, sa_ref, pa_ref, sb_ref, pb_ref),
                        lambda cr: trip(c, cr, sb_ref, pb_ref, sa_ref, pa_ref),
                        carry)

    ms = (jnp.full((1, TQ), -jnp.inf, F32),) * GQA_GROUP
    chunk_max = score_stage(0, sa_ref)
    ms, alphas = exp_stage(sa_ref, pa_ref, ms, chunk_max)
    chunk_max = score_stage(1, sb_ref)
    ms, alphas, chunk_max = lax.fori_loop(0, n_chunks - 2, step, (ms, alphas, chunk_max))
    out_stage(n_chunks - 2, pa_ref, alphas)
    ms, alphas = exp_stage(sb_ref, pb_ref, ms, chunk_max)
    out_stage(n_chunks - 1, pb_ref, alphas)
    for g in heads:
        acc = acc_ref[g]
        out_t = acc[:HEAD_DIM, :] / acc[HEAD_DIM:HEAD_DIM + 1, :]
        o_ref[:, g * HEAD_DIM:(g + 1) * HEAD_DIM] = out_t.T.astype(BF16)


def _attention(qkv):
    s = qkv.shape[1]
    group_w = GQA_GROUP * HEAD_DIM
    return pl.pallas_call(
        _attn_kernel,
        grid=(N_KV_HEADS, s // TQ),
        in_specs=[
            pl.BlockSpec((GQA_GROUP // HEADS_PER_SLAB, TQ, MXU_COLS), lambda h, i: (h, i, 0)),
            pl.BlockSpec((None, s, HEAD_DIM),
                         lambda h, i: (QKV_Q_SLABS + h // HEADS_PER_SLAB, 0, h % HEADS_PER_SLAB)),
            pl.BlockSpec((None, s, HEAD_DIM),
                         lambda h, i: (QKV_Q_SLABS + QKV_K_SLABS + h // HEADS_PER_SLAB, 0,
                                       h % HEADS_PER_SLAB)),
        ],
        out_specs=pl.BlockSpec((TQ, group_w), lambda h, i: (i, h)),
        out_shape=jax.ShapeDtypeStruct((s, N_Q_HEADS * HEAD_DIM), BF16),
        scratch_shapes=[pltpu.VMEM((s // TK, VT_ROWS, TK), BF16),
                        pltpu.VMEM((GQA_GROUP, VT_ROWS, TQ), F32),
                        pltpu.VMEM((GQA_GROUP, TK, TQ), F32),
                        pltpu.VMEM((GQA_GROUP, TK, TQ), F32),
                        pltpu.VMEM((GQA_GROUP, TK, TQ), BF16),
                        pltpu.VMEM((GQA_GROUP, TK, TQ), BF16)],
        compiler_params=_params("arbitrary", "arbitrary"),
        name="attention",
    )(qkv, qkv, qkv)


def _proj_res_kernel(a_ref, w_ref, x_ref, o_ref):
    o_ref[...] = x_ref[...] + jnp.dot(a_ref[...], w_ref[...], preferred_element_type=F32)


def _proj_residual(a, w_all, layer, x):
    s, k = a.shape
    return pl.pallas_call(
        _proj_res_kernel,
        grid=(s // TM,),
        in_specs=[
            pl.BlockSpec((TM, k), lambda i: (i, 0)),
            pl.BlockSpec((None, k, D_MODEL), lambda i: (layer, 0, 0), pipeline_mode=pl.Buffered(1)),
            pl.BlockSpec((TM, D_MODEL), lambda i: (i, 0)),
        ],
        out_specs=pl.BlockSpec((TM, D_MODEL), lambda i: (i, 0)),
        out_shape=jax.ShapeDtypeStruct((s, D_MODEL), F32),
        compiler_params=_params("arbitrary"),
        name="proj_residual",
    )(a, w_all, x)


def _rnn_in_kernel(x_ref, g_ref, w_ref, xb_ref):
    hn = _rms_normed(x_ref[...], g_ref[...]).astype(BF16)
    xb_ref[...] = jnp.dot(hn, w_ref[...], preferred_element_type=F32)


def _rnn_in_proj(x, g, w_all, layer):
    s = x.shape[0]
    return pl.pallas_call(
        _rnn_in_kernel,
        grid=(s // TM,),
        in_specs=[
            pl.BlockSpec((TM, D_MODEL), lambda i: (i, 0)),
            pl.BlockSpec((1, D_MODEL), lambda i: (0, 0)),
            pl.BlockSpec((None, D_MODEL, D_RNN), lambda i: (layer, 0, 0), pipeline_mode=pl.Buffered(1)),
        ],
        out_specs=pl.BlockSpec((TM, D_RNN), lambda i: (i, 0)),
        out_shape=jax.ShapeDtypeStruct((s, D_RNN), F32),
        compiler_params=_params("arbitrary"),
        name="rnn_in_proj",
    )(x, g, w_all)


def _sigmoid(z):
    return 0.5 * jnp.tanh(0.5 * z) + 0.5


def _softplus(z):
    return jnp.maximum(z, 0.0) + jnp.log1p(jnp.exp(-jnp.abs(z)))


def _lru_inputs(d, cur_ref, prev_ref, next_ref, first, last, cw_ref, cb_ref, wa_ref, ba_ref, wi_ref,
                bi_ref, lam_ref, ext_ref, a_ref, u_ref):
    ext_ref[d, 0:SUBLANES, :] = jnp.where(first, 0.0, prev_ref[...])
    ext_ref[d, SUBLANES:SUBLANES + TT, :] = cur_ref[...]
    ext_ref[d, SUBLANES + TT:2 * SUBLANES + TT, :] = jnp.where(last, 0.0, next_ref[...])
    ext = ext_ref[d]
    xc = cb_ref[...]
    for k in range(CONV_W):
        shift = (CONV_LEFT - k) % ext.shape[0]
        tap = ext if shift == 0 else pltpu.roll(ext, shift, 0)
        xc = xc + cw_ref[k:k + 1, :] * tap[SUBLANES:SUBLANES + TT, :]
    xb16 = xc.astype(BF16)
    r = _sigmoid(jnp.dot(xb16, wa_ref[d, 0], preferred_element_type=F32) + ba_ref[d:d + 1, :])
    i = _sigmoid(jnp.dot(xb16, wi_ref[d, 0], preferred_element_type=F32) + bi_ref[d:d + 1, :])
    log_a = r * (-LRU_C * _softplus(-lam_ref[d:d + 1, :]))
    a = jnp.exp(log_a)
    one_minus_a2 = -jnp.tanh(log_a) * (a * a + 1.0)
    root = jnp.where(one_minus_a2 > 0.0, one_minus_a2 * lax.rsqrt(one_minus_a2), 0.0)
    u = root * (i * xc)
    for s in range(SCAN_BLOCKS):
        for l in range(LANE_SLABS):
            rows = slice(s * SCAN_ROWS, (s + 1) * SCAN_ROWS)
            lanes = slice(l * LANES, (l + 1) * LANES)
            a_ref[d, l, s * SCAN_PITCH:s * SCAN_PITCH + SCAN_ROWS, :] = a[rows, lanes]
            u_ref[d, l, s * SCAN_PITCH:s * SCAN_PITCH + SCAN_ROWS, :] = u[rows, lanes]


def _rnn_core_kernel(fc_ref, fp_ref, fn_ref, bc_ref, bp_ref, bn_ref, cw_ref, cb_ref, wa_ref, ba_ref,
                     wi_ref, bi_ref, lam_ref, hf_ref, hb_ref, ext_ref, a_ref, u_ref, hl_ref, p_ref,
                     carry_ref):
    i = pl.program_id(1)
    n = pl.num_programs(1)

    @pl.when(i == 0)
    def _():
        carry_ref[...] = jnp.zeros_like(carry_ref)

    shared = (cw_ref, cb_ref, wa_ref, ba_ref, wi_ref, bi_ref, lam_ref, ext_ref, a_ref, u_ref)
    _lru_inputs(0, fc_ref, fp_ref, fn_ref, i == 0, i == n - 1, *shared)
    _lru_inputs(1, bc_ref, bp_ref, bn_ref, i == n - 1, i == 0, *shared)

    pairs = [(d, l) for d in range(2) for l in range(LANE_SLABS)]

    def body(jj, carry):
        out = []
        for (d, l), (h, p) in zip(pairs, carry):
            j = jj if d == 0 else SCAN_ROWS - 1 - jj
            rows = pl.ds(j, SCAN_BLOCKS, stride=SCAN_PITCH)
            a = a_ref[d, l, rows, :]
            h = a * h + u_ref[d, l, rows, :]
            p = a * p
            hl_ref[d, l, rows, :] = h
            p_ref[d, l, rows, :] = p
            out.append((h, p))
        return tuple(out)

    zero = jnp.zeros((SCAN_BLOCKS, LANES), F32)
    ends = lax.fori_loop(0, SCAN_ROWS, body, ((zero, zero + 1.0),) * len(pairs), unroll=4)

    for (d, l), (h_end, p_end) in zip(pairs, ends):
        lanes = slice(l * LANES, (l + 1) * LANES)
        out_ref = hf_ref if d == 0 else hb_ref
        c = carry_ref[d, 0:1, lanes]
        for s in (range(SCAN_BLOCKS) if d == 0 else reversed(range(SCAN_BLOCKS))):
            blk = slice(s * SCAN_PITCH, s * SCAN_PITCH + SCAN_ROWS)
            out_ref[s * SCAN_ROWS:(s + 1) * SCAN_ROWS, lanes] = hl_ref[d, l, blk, :] + p_ref[d, l, blk, :] * c
            c = h_end[s:s + 1, :] + p_end[s:s + 1, :] * c
        carry_ref[d, :, lanes] = jnp.broadcast_to(c, (SUBLANES, LANES))


def _rnn_core(xb, conv_w, conv_b, w_a, b_a, w_i, b_i, lam):
    s = xb.shape[0]
    n = s // TT
    halo = TT // SUBLANES
    last_halo = s // SUBLANES - 1
    cur = lambda t: pl.BlockSpec((TT, RNN_BLOCK_W), lambda c, i: (t(i, n), c))
    prv = lambda t: pl.BlockSpec((SUBLANES, RNN_BLOCK_W),
                                 lambda c, i: (jnp.maximum(t(i, n) * halo - 1, 0), c))
    nxt = lambda t: pl.BlockSpec((SUBLANES, RNN_BLOCK_W),
                                 lambda c, i: (jnp.minimum((t(i, n) + 1) * halo, last_halo), c))
    fwd = lambda i, n: i
    bwd = lambda i, n: n - 1 - i
    chan = lambda rows: pl.BlockSpec((rows, RNN_BLOCK_W), lambda c, i: (0, c))
    gate_w = pl.BlockSpec((2, 1, RNN_BLOCK_W, RNN_BLOCK_W), lambda c, i: (0, c, 0, 0))
    return pl.pallas_call(
        _rnn_core_kernel,
        grid=(RNN_BLOCKS, n),
        in_specs=[cur(fwd), prv(fwd), nxt(fwd), cur(bwd), prv(bwd), nxt(bwd),
                  chan(CONV_W), chan(1), gate_w, chan(2), gate_w, chan(2), chan(2)],
        out_specs=[pl.BlockSpec((TT, RNN_BLOCK_W), lambda c, i: (i, c)),
                   pl.BlockSpec((TT, RNN_BLOCK_W), lambda c, i: (n - 1 - i, c))],
        out_shape=[jax.ShapeDtypeStruct((s, D_RNN), F32), jax.ShapeDtypeStruct((s, D_RNN), F32)],
        scratch_shapes=[
            pltpu.VMEM((2, TT + 2 * SUBLANES, RNN_BLOCK_W), F32),
            pltpu.VMEM((2, LANE_SLABS, SCAN_BLOCKS * SCAN_PITCH, LANES), F32),
            pltpu.VMEM((2, LANE_SLABS, SCAN_BLOCKS * SCAN_PITCH, LANES), F32),
            pltpu.VMEM((2, LANE_SLABS, SCAN_BLOCKS * SCAN_PITCH, LANES), F32),
            pltpu.VMEM((2, LANE_SLABS, SCAN_BLOCKS * SCAN_PITCH, LANES), F32),
            pltpu.VMEM((2, SUBLANES, RNN_BLOCK_W), F32),
        ],
        compiler_params=_params("arbitrary", "arbitrary"),
        name="rnn_core",
    )(xb, xb, xb, xb, xb, xb, conv_w, conv_b, w_a, b_a, w_i, b_i, lam)


def _rnn_out_kernel(x_ref, g_ref, hf_ref, hb_ref, wy_ref, wo_ref, o_ref):
    x = x_ref[...]
    hn = _rms_normed(x, g_ref[...]).astype(BF16)
    gate = jax.nn.gelu(jnp.dot(hn, wy_ref[...], preferred_element_type=F32), approximate=True)
    a = ((hf_ref[...] + hb_ref[...]) * gate).astype(BF16)
    o_ref[...] = x + jnp.dot(a, wo_ref[...], preferred_element_type=F32)


def _rnn_out_proj(x, g, hf, hb, w_in_all, w_out_all, layer):
    s = x.shape[0]
    row = pl.BlockSpec((TM_OUT, D_MODEL), lambda i: (i, 0))
    return pl.pallas_call(
        _rnn_out_kernel,
        grid=(s // TM_OUT,),
        in_specs=[row,
                  pl.BlockSpec((1, D_MODEL), lambda i: (0, 0)),
                  row, row,
                  pl.BlockSpec((None, D_MODEL, D_RNN), lambda i: (layer, 0, 1),
                               pipeline_mode=pl.Buffered(1)),
                  pl.BlockSpec((None, D_RNN, D_MODEL), lambda i: (layer, 0, 0),
                               pipeline_mode=pl.Buffered(1))],
        out_specs=row,
        out_shape=jax.ShapeDtypeStruct((s, D_MODEL), F32),
        compiler_params=_params("arbitrary"),
        name="rnn_out_proj",
    )(x, g, hf, hb, w_in_all, w_out_all)


def _ffn_kernel(x_ref, g_ref, wg_ref, wu_ref, wd_ref, o_ref, hn_ref):
    @pl.when(pl.program_id(1) == 0)
    def _():
        x = x_ref[...]
        hn_ref[...] = _rms_normed(x, g_ref[...]).astype(BF16)
        o_ref[...] = x

    hn = hn_ref[...]
    gate = jnp.dot(hn, wg_ref[...].astype(BF16), preferred_element_type=F32)
    up = jnp.dot(hn, wu_ref[...].astype(BF16), preferred_element_type=F32)
    act = (jax.nn.silu(gate) * up).astype(BF16)
    o_ref[...] += jnp.dot(act, wd_ref[...].astype(BF16), preferred_element_type=F32)


def _ffn(x, g, w_gate, w_up, w_down, layer):
    s = x.shape[0]
    return pl.pallas_call(
        _ffn_kernel,
        grid=(s // TM_FFN, D_FF // TF),
        in_specs=[
            pl.BlockSpec((TM_FFN, D_MODEL), lambda i, f: (i, 0)),
            pl.BlockSpec((1, D_MODEL), lambda i, f: (0, 0)),
            pl.BlockSpec((None, D_MODEL, TF), lambda i, f: (layer, 0, f)),
            pl.BlockSpec((None, D_MODEL, TF), lambda i, f: (layer, 0, f)),
            pl.BlockSpec((None, TF, D_MODEL), lambda i, f: (layer, f, 0)),
        ],
        out_specs=pl.BlockSpec((TM_FFN, D_MODEL), lambda i, f: (i, 0)),
        out_shape=jax.ShapeDtypeStruct((s, D_MODEL), F32),
        scratch_shapes=[pltpu.VMEM((TM_FFN, D_MODEL), BF16)],
        compiler_params=_params("arbitrary", "arbitrary"),
        name="ffn",
    )(x, g, w_gate, w_up, w_down)


def _rope_tables(seq_len):
    rows_n = seq_len // GRID_W
    freqs = ROPE_THETA ** (-jnp.arange(ROPE_FREQS, dtype=F32) / ROPE_FREQS)
    row_ang = jnp.arange(rows_n, dtype=F32)[:, None, None] * freqs
    col_ang = jnp.arange(GRID_W, dtype=F32)[None, :, None] * freqs
    shape = (rows_n, GRID_W, ROPE_FREQS)
    row_ang = jnp.broadcast_to(row_ang, shape).reshape(seq_len, ROPE_FREQS)
    col_ang = jnp.broadcast_to(col_ang, shape).reshape(seq_len, ROPE_FREQS)
    cr, sr, cc, sc = jnp.cos(row_ang), jnp.sin(row_ang), jnp.cos(col_ang), jnp.sin(col_ang)
    return (jnp.concatenate([cr, cr, cc, cc], axis=-1),
            jnp.concatenate([-sr, sr, -sc, sc], axis=-1))


def kernel(x, norm_mix, norm_ffn, attn_w_qkv, attn_q_gain, attn_k_gain, attn_w_o, rnn_w_in, rnn_conv_w, rnn_conv_b, rnn_w_a, rnn_b_a, rnn_w_i, rnn_b_i, rnn_lambda, rnn_w_out, ffn_w_gate, ffn_w_up, ffn_w_down):
    b, s, d = x.shape
    assert (b, d) == (1, D_MODEL) and s % max(TM, TQ, TK, TT) == 0
    cos_t, sin_t = _rope_tables(s)
    w_o = attn_w_o.astype(BF16)
    w_qkv = attn_w_qkv.reshape(-1, D_MODEL, QKV_SLABS, MXU_COLS).transpose(0, 2, 1, 3).astype(BF16)
    w_in, w_out = rnn_w_in.astype(BF16), rnn_w_out.astype(BF16)
    w_a, w_i = rnn_w_a.astype(BF16), rnn_w_i.astype(BF16)
    h = x.reshape(s, d)
    for i in range(DEPTH):
        j = i // N_MIXERS
        g_mix = norm_mix[i].reshape(1, d)
        if i % N_MIXERS == 0:
            qkv = _qkv_proj(h, g_mix, w_qkv, j,
                            attn_q_gain[j].reshape(1, HEAD_DIM), attn_k_gain[j].reshape(1, HEAD_DIM),
                            cos_t, sin_t)
            o = _attention(qkv)
            h = _proj_residual(o, w_o, j, h)
        else:
            xb = _rnn_in_proj(h, g_mix, w_in, j)
            hf, hb = _rnn_core(xb, rnn_conv_w[j], rnn_conv_b[j].reshape(1, D_RNN),
                               w_a[j], rnn_b_a[j], w_i[j], rnn_b_i[j], rnn_lambda[j])
            h = _rnn_out_proj(h, g_mix, hf, hb, w_in, w_out, j)
        h = _ffn(h, norm_ffn[i].reshape(1, d), ffn_w_gate, ffn_w_up, ffn_w_down, i)
    return h.reshape(b, s, d)
```

```python
import jax
import jax.numpy as jnp
from jax import lax
from jax.experimental import pallas as pl
from jax.experimental.pallas import tpu as pltpu

F32 = jnp.float32
BF16 = jnp.bfloat16

D_MODEL = 2048
DEPTH = 4
N_MIXERS = 2
GRID_W = 64
ROPE_THETA = 10000.0
HEAD_DIM = 128
N_Q_HEADS = 16
N_KV_HEADS = 4
GQA_GROUP = N_Q_HEADS // N_KV_HEADS
ROPE_FREQS = HEAD_DIM // 4
QKV_DIM = (N_Q_HEADS + 2 * N_KV_HEADS) * HEAD_DIM
D_RNN = D_MODEL
RNN_BLOCK_W = 256
RNN_BLOCKS = D_RNN // RNN_BLOCK_W
CONV_W = 4
CONV_LEFT = 2
LRU_C = 8.0
D_FF = 5632
EPS = 1e-6
Q_SCALE = HEAD_DIM ** -0.5 * 1.4426950408889634

SUBLANES = 8
BF16_SUBLANES = 16
LANES = 128
MXU_COLS = 256
VMEM_LIMIT_BYTES = 56 * 1024 * 1024

TM = 512
TM_OUT = 256
HEADS_PER_SLAB = MXU_COLS // HEAD_DIM
QKV_SLABS = QKV_DIM // MXU_COLS
QKV_Q_SLABS = N_Q_HEADS // HEADS_PER_SLAB
QKV_K_SLABS = N_KV_HEADS // HEADS_PER_SLAB
TM_FFN = 1024
TF = 256
TQ = 512
TK = 1024
TT = 512
SCAN_BLOCKS = SUBLANES
SCAN_ROWS = TT // SCAN_BLOCKS
SCAN_PITCH = SCAN_ROWS + SUBLANES
LANE_SLABS = RNN_BLOCK_W // LANES
VT_ROWS = HEAD_DIM + BF16_SUBLANES
EXP_ROWS = 32

def _params(*sem):
    return pltpu.CompilerParams(dimension_semantics=sem, vmem_limit_bytes=VMEM_LIMIT_BYTES)


def _rms_normed(x, g):
    ms = jnp.mean(x * x, axis=-1, keepdims=True)
    return x * lax.rsqrt(ms + EPS) * g


def _qkv_kernel(x_ref, g_ref, w_ref, qg_ref, kg_ref, cos_ref, sin_ref, o_ref, hn_ref, ya_ref, yb_ref):
    hn_ref[...] = _rms_normed(x_ref[...], g_ref[...]).astype(BF16)
    q_gain = qg_ref[...] * Q_SCALE
    lane = lax.broadcasted_iota(jnp.int32, (TM, HEAD_DIM), 1)
    first_half = (lane & (2 * ROPE_FREQS - 1)) < ROPE_FREQS

    y_refs = (ya_ref, yb_ref)

    def matmul(s):
        w = w_ref[:, s * MXU_COLS:(s + 1) * MXU_COLS].astype(BF16)
        y_refs[s % 2][...] = jnp.dot(hn_ref[...], w, preferred_element_type=F32)

    def epilogue(s):
        for h in range(s * HEADS_PER_SLAB, (s + 1) * HEADS_PER_SLAB):
            lanes = slice((h % HEADS_PER_SLAB) * HEAD_DIM, (h % HEADS_PER_SLAB + 1) * HEAD_DIM)
            yh = y_refs[s % 2][:, lanes]
            if h >= N_Q_HEADS + N_KV_HEADS:
                out = yh
            else:
                gain = q_gain if h < N_Q_HEADS else kg_ref[...]
                yn = yh * lax.rsqrt(jnp.mean(yh * yh, axis=-1, keepdims=True) + EPS) * gain
                partner = jnp.where(first_half,
                                    pltpu.roll(yn, HEAD_DIM - ROPE_FREQS, 1),
                                    pltpu.roll(yn, ROPE_FREQS, 1))
                out = yn * cos_ref[...] + partner * sin_ref[...]
            o_ref[:, h * HEAD_DIM:(h + 1) * HEAD_DIM] = out.astype(BF16)

    every_tile = pl.program_id(0) >= 0
    matmul(0)
    for s in range(1, QKV_SLABS):
        @pl.when(every_tile)
        def _(s=s):
            epilogue(s - 1)
            matmul(s)
    epilogue(QKV_SLABS - 1)


def _qkv_proj(x, g, w_all, layer, q_gain, k_gain, cos_t, sin_t):
    s = x.shape[0]
    return pl.pallas_call(
        _qkv_kernel,
        grid=(s // TM,),
        in_specs=[
            pl.BlockSpec((TM, D_MODEL), lambda i: (i, 0)),
            pl.BlockSpec((1, D_MODEL), lambda i: (0, 0)),
            pl.BlockSpec((None, D_MODEL, QKV_DIM), lambda i: (layer, 0, 0),
                         pipeline_mode=pl.Buffered(1)),
            pl.BlockSpec((1, HEAD_DIM), lambda i: (0, 0)),
            pl.BlockSpec((1, HEAD_DIM), lambda i: (0, 0)),
            pl.BlockSpec((TM, HEAD_DIM), lambda i: (i, 0)),
            pl.BlockSpec((TM, HEAD_DIM), lambda i: (i, 0)),
        ],
        out_specs=pl.BlockSpec((TM, QKV_DIM), lambda i: (i, 0)),
        out_shape=jax.ShapeDtypeStruct((s, QKV_DIM), BF16),
        scratch_shapes=[pltpu.VMEM((TM, D_MODEL), BF16),
                        pltpu.VMEM((TM, MXU_COLS), F32),
                        pltpu.VMEM((TM, MXU_COLS), F32)],
        compiler_params=_params("arbitrary"),
        name="qkv_proj",
    )(x, g, w_all, q_gain, k_gain, cos_t, sin_t)


def _attn_kernel(q_ref, k_ref, v_ref, o_ref, vt_ref, acc_ref, sa_ref, sb_ref, pa_ref, pb_ref):
    n_chunks = k_ref.shape[0] // TK

    @pl.when(pl.program_id(1) == 0)
    def _():
        for c in range(n_chunks):
            vc = v_ref[c * TK:(c + 1) * TK, :].astype(F32)
            vt_ref[c, :HEAD_DIM, :] = vc.T.astype(BF16)
            vt_ref[c, HEAD_DIM:, :] = jnp.ones((VT_ROWS - HEAD_DIM, TK), BF16)

    acc_ref[...] = jnp.zeros_like(acc_ref)
    heads = range(GQA_GROUP)

    def score_stage(c, st_ref):
        kc = k_ref[pl.ds(pl.multiple_of(c * TK, TK), TK), :]
        chunk_max = []
        for g in heads:
            q = q_ref[:, g * HEAD_DIM:(g + 1) * HEAD_DIM]
            st = lax.dot_general(kc, q, (((1,), (1,)), ((), ())), preferred_element_type=F32)
            st_ref[g] = st
            chunk_max.append(jnp.max(st, axis=0, keepdims=True))
        return tuple(chunk_max)

    def exp_stage(st_ref, pt_ref, ms, chunk_max):
        new_ms, alphas = [], []
        for g in heads:
            m_new = jnp.maximum(ms[g], chunk_max[g])
            for r in range(0, TK, EXP_ROWS):
                pt_ref[g, r:r + EXP_ROWS, :] = jnp.exp2(st_ref[g, r:r + EXP_ROWS, :] - m_new).astype(BF16)
            new_ms.append(m_new)
            alphas.append(jnp.exp2(ms[g] - m_new))
        return tuple(new_ms), tuple(alphas)

    def out_stage(c, pt_ref, alphas):
        for g in heads:
            acc_ref[g] = alphas[g] * acc_ref[g] + jnp.dot(vt_ref[c], pt_ref[g], preferred_element_type=F32)

    def trip(c, carry, s_cur, p_cur, s_nxt, p_nxt):
        ms, alphas, chunk_max = carry
        out_stage(c, p_cur, alphas)
        ms, alphas = exp_stage(s_nxt, p_nxt, ms, chunk_max)
        chunk_max = score_stage(c + 2, s_cur)
        return ms, alphas, chunk_max

    def step(c, carry):
        return lax.cond(c % 2 == 0,
                        lambda cr: trip(c, cr, sa_ref, pa_ref, sb_ref, pb_ref),
                        lambda cr: trip(c, cr, sb_ref, pb_ref, sa_ref, pa_ref),
                        carry)

    ms = (jnp.full((1, TQ), -jnp.inf, F32),) * GQA_GROUP
    chunk_max = score_stage(0, sa_ref)
    ms, alphas = exp_stage(sa_ref, pa_ref, ms, chunk_max)
    chunk_max = score_stage(1, sb_ref)
    ms, alphas, chunk_max = lax.fori_loop(0, n_chunks - 2, step, (ms, alphas, chunk_max))
    out_stage(n_chunks - 2, pa_ref, alphas)
    ms, alphas = exp_stage(sb_ref, pb_ref, ms, chunk_max)
    out_stage(n_chunks - 1, pb_ref, alphas)
    for g in heads:
        acc = acc_ref[g]
        out_t = acc[:HEAD_DIM, :] / acc[HEAD_DIM:HEAD_DIM + 1, :]
        o_ref[:, g * HEAD_DIM:(g + 1) * HEAD_DIM] = out_t.T.astype(BF16)


def _attention(qkv):
    s = qkv.shape[0]
    group_w = GQA_GROUP * HEAD_DIM
    return pl.pallas_call(
        _attn_kernel,
        grid=(N_KV_HEADS, s // TQ),
        in_specs=[
            pl.BlockSpec((TQ, group_w), lambda h, i: (i, h)),
            pl.BlockSpec((s, HEAD_DIM), lambda h, i: (0, N_Q_HEADS + h)),
            pl.BlockSpec((s, HEAD_DIM), lambda h, i: (0, N_Q_HEADS + N_KV_HEADS + h)),
        ],
        out_specs=pl.BlockSpec((TQ, group_w), lambda h, i: (i, h)),
        out_shape=jax.ShapeDtypeStruct((s, N_Q_HEADS * HEAD_DIM), BF16),
        scratch_shapes=[pltpu.VMEM((s // TK, VT_ROWS, TK), BF16),
                        pltpu.VMEM((GQA_GROUP, VT_ROWS, TQ), F32),
                        pltpu.VMEM((GQA_GROUP, TK, TQ), F32),
                        pltpu.VMEM((GQA_GROUP, TK, TQ), F32),
                        pltpu.VMEM((GQA_GROUP, TK, TQ), BF16),
                        pltpu.VMEM((GQA_GROUP, TK, TQ), BF16)],
        compiler_params=_params("arbitrary", "arbitrary"),
        name="attention",
    )(qkv, qkv, qkv)


def _proj_res_kernel(a_ref, w_ref, x_ref, o_ref):
    o_ref[...] = x_ref[...] + jnp.dot(a_ref[...], w_ref[...], preferred_element_type=F32)


def _proj_residual(a, w_all, layer, x):
    s, k = a.shape
    return pl.pallas_call(
        _proj_res_kernel,
        grid=(s // TM,),
        in_specs=[
            pl.BlockSpec((TM, k), lambda i: (i, 0)),
            pl.BlockSpec((None, k, D_MODEL), lambda i: (layer, 0, 0), pipeline_mode=pl.Buffered(1)),
            pl.BlockSpec((TM, D_MODEL), lambda i: (i, 0)),
        ],
        out_specs=pl.BlockSpec((TM, D_MODEL), lambda i: (i, 0)),
        out_shape=jax.ShapeDtypeStruct((s, D_MODEL), F32),
        compiler_params=_params("arbitrary"),
        name="proj_residual",
    )(a, w_all, x)


def _rnn_in_kernel(x_ref, g_ref, w_ref, xb_ref):
    hn = _rms_normed(x_ref[...], g_ref[...]).astype(BF16)
    xb_ref[...] = jnp.dot(hn, w_ref[...], preferred_element_type=F32)


def _rnn_in_proj(x, g, w_all, layer):
    s = x.shape[0]
    return pl.pallas_call(
        _rnn_in_kernel,
        grid=(s // TM,),
        in_specs=[
            pl.BlockSpec((TM, D_MODEL), lambda i: (i, 0)),
            pl.BlockSpec((1, D_MODEL), lambda i: (0, 0)),
            pl.BlockSpec((None, D_MODEL, D_RNN), lambda i: (layer, 0, 0), pipeline_mode=pl.Buffered(1)),
        ],
        out_specs=pl.BlockSpec((TM, D_RNN), lambda i: (i, 0)),
        out_shape=jax.ShapeDtypeStruct((s, D_RNN), F32),
        compiler_params=_params("arbitrary"),
        name="rnn_in_proj",
    )(x, g, w_all)


def _softplus(z):
    return jnp.maximum(z, 0.0) + jnp.log1p(jnp.exp(-jnp.abs(z)))


def _lru_inputs(d, cur_ref, prev_ref, next_ref, first, last, cw_ref, cb_ref, wa_ref, ba_ref, wi_ref,
                bi_ref, lam_ref, ext_ref, a_ref, u_ref):
    ext_ref[d, 0:SUBLANES, :] = jnp.where(first, 0.0, prev_ref[...])
    ext_ref[d, SUBLANES:SUBLANES + TT, :] = cur_ref[...]
    ext_ref[d, SUBLANES + TT:2 * SUBLANES + TT, :] = jnp.where(last, 0.0, next_ref[...])
    ext = ext_ref[d]
    xc = cb_ref[...]
    for k in range(CONV_W):
        shift = (CONV_LEFT - k) % ext.shape[0]
        tap = ext if shift == 0 else pltpu.roll(ext, shift, 0)
        xc = xc + cw_ref[k:k + 1, :] * tap[SUBLANES:SUBLANES + TT, :]
    xb16 = xc.astype(BF16)
    ta = jnp.tanh(jnp.dot(xb16, wa_ref[d, 0], preferred_element_type=F32) + 0.5 * ba_ref[d:d + 1, :])
    ti = jnp.tanh(jnp.dot(xb16, wi_ref[d, 0], preferred_element_type=F32) + 0.5 * bi_ref[d:d + 1, :])
    i = 0.5 * ti + 0.5
    half_c = (-0.5 * LRU_C) * _softplus(-lam_ref[d:d + 1, :])
    log_a = ta * half_c + half_c
    a = jnp.exp(log_a)
    one_minus_a2 = -jnp.tanh(log_a) * (a * a + 1.0)
    root = jnp.where(one_minus_a2 > 0.0, one_minus_a2 * lax.rsqrt(one_minus_a2), 0.0)
    u = root * (i * xc)
    for s in range(SCAN_BLOCKS):
        for l in range(LANE_SLABS):
            rows = slice(s * SCAN_ROWS, (s + 1) * SCAN_ROWS)
            lanes = slice(l * LANES, (l + 1) * LANES)
            a_ref[d, l, s * SCAN_PITCH:s * SCAN_PITCH + SCAN_ROWS, :] = a[rows, lanes]
            u_ref[d, l, s * SCAN_PITCH:s * SCAN_PITCH + SCAN_ROWS, :] = u[rows, lanes]


def _rnn_core_kernel(fc_ref, fp_ref, fn_ref, bc_ref, bp_ref, bn_ref, cw_ref, cb_ref, wa_ref, ba_ref,
                     wi_ref, bi_ref, lam_ref, hf_ref, hb_ref, ext_ref, a_ref, u_ref, hl_ref, p_ref,
                     carry_ref):
    i = pl.program_id(1)
    n = pl.num_programs(1)

    @pl.when(i == 0)
    def _():
        carry_ref[...] = jnp.zeros_like(carry_ref)

    shared = (cw_ref, cb_ref, wa_ref, ba_ref, wi_ref, bi_ref, lam_ref, ext_ref, a_ref, u_ref)
    _lru_inputs(0, fc_ref, fp_ref, fn_ref, i == 0, i == n - 1, *shared)
    _lru_inputs(1, bc_ref, bp_ref, bn_ref, i == n - 1, i == 0, *shared)

    pairs = [(d, l) for d in range(2) for l in range(LANE_SLABS)]

    def body(jj, carry):
        out = []
        for (d, l), (h, p) in zip(pairs, carry):
            j = jj if d == 0 else SCAN_ROWS - 1 - jj
            rows = pl.ds(j, SCAN_BLOCKS, stride=SCAN_PITCH)
            a = a_ref[d, l, rows, :]
            h = a * h + u_ref[d, l, rows, :]
            p = a * p
            hl_ref[d, l, rows, :] = h
            p_ref[d, l, rows, :] = p
            out.append((h, p))
        return tuple(out)

    zero = jnp.zeros((SCAN_BLOCKS, LANES), F32)
    ends = lax.fori_loop(0, SCAN_ROWS, body, ((zero, zero + 1.0),) * len(pairs), unroll=4)

    for (d, l), (h_end, p_end) in zip(pairs, ends):
        lanes = slice(l * LANES, (l + 1) * LANES)
        out_ref = hf_ref if d == 0 else hb_ref
        c = carry_ref[d, 0:1, lanes]
        for s in (range(SCAN_BLOCKS) if d == 0 else reversed(range(SCAN_BLOCKS))):
            blk = slice(s * SCAN_PITCH, s * SCAN_PITCH + SCAN_ROWS)
            out_ref[s * SCAN_ROWS:(s + 1) * SCAN_ROWS, lanes] = hl_ref[d, l, blk, :] + p_ref[d, l, blk, :] * c
            c = h_end[s:s + 1, :] + p_end[s:s + 1, :] * c
        carry_ref[d, :, lanes] = jnp.broadcast_to(c, (SUBLANES, LANES))


def _rnn_core(xb, conv_w, conv_b, w_a, b_a, w_i, b_i, lam):
    s = xb.shape[0]
    n = s // TT
    halo = TT // SUBLANES
    last_halo = s // SUBLANES - 1
    cur = lambda t: pl.BlockSpec((TT, RNN_BLOCK_W), lambda c, i: (t(i, n), c))
    prv = lambda t: pl.BlockSpec((SUBLANES, RNN_BLOCK_W),
                                 lambda c, i: (jnp.maximum(t(i, n) * halo - 1, 0), c))
    nxt = lambda t: pl.BlockSpec((SUBLANES, RNN_BLOCK_W),
                                 lambda c, i: (jnp.minimum((t(i, n) + 1) * halo, last_halo), c))
    fwd = lambda i, n: i
    bwd = lambda i, n: n - 1 - i
    chan = lambda rows: pl.BlockSpec((rows, RNN_BLOCK_W), lambda c, i: (0, c))
    gate_w = pl.BlockSpec((2, 1, RNN_BLOCK_W, RNN_BLOCK_W), lambda c, i: (0, c, 0, 0))
    return pl.pallas_call(
        _rnn_core_kernel,
        grid=(RNN_BLOCKS, n),
        in_specs=[cur(fwd), prv(fwd), nxt(fwd), cur(bwd), prv(bwd), nxt(bwd),
                  chan(CONV_W), chan(1), gate_w, chan(2), gate_w, chan(2), chan(2)],
        out_specs=[pl.BlockSpec((TT, RNN_BLOCK_W), lambda c, i: (i, c)),
                   pl.BlockSpec((TT, RNN_BLOCK_W), lambda c, i: (n - 1 - i, c))],
        out_shape=[jax.ShapeDtypeStruct((s, D_RNN), F32), jax.ShapeDtypeStruct((s, D_RNN), F32)],
        scratch_shapes=[
            pltpu.VMEM((2, TT + 2 * SUBLANES, RNN_BLOCK_W), F32),
            pltpu.VMEM((2, LANE_SLABS, SCAN_BLOCKS * SCAN_PITCH, LANES), F32),
            pltpu.VMEM((2, LANE_SLABS, SCAN_BLOCKS * SCAN_PITCH, LANES), F32),
            pltpu.VMEM((2, LANE_SLABS, SCAN_BLOCKS * SCAN_PITCH, LANES), F32),
            pltpu.VMEM((2, LANE_SLABS, SCAN_BLOCKS * SCAN_PITCH, LANES), F32),
            pltpu.VMEM((2, SUBLANES, RNN_BLOCK_W), F32),
        ],
        compiler_params=_params("arbitrary", "arbitrary"),
        name="rnn_core",
    )(xb, xb, xb, xb, xb, xb, conv_w, conv_b, w_a, b_a, w_i, b_i, lam)


def _rnn_out_kernel(x_ref, g_ref, hf_ref, hb_ref, wy_ref, wo_ref, o_ref):
    x = x_ref[...]
    hn = _rms_normed(x, g_ref[...]).astype(BF16)
    gate = jax.nn.gelu(jnp.dot(hn, wy_ref[...], preferred_element_type=F32), approximate=True)
    a = ((hf_ref[...] + hb_ref[...]) * gate).astype(BF16)
    o_ref[...] = x + jnp.dot(a, wo_ref[...], preferred_element_type=F32)


def _rnn_out_proj(x, g, hf, hb, w_in_all, w_out_all, layer):
    s = x.shape[0]
    row = pl.BlockSpec((TM_OUT, D_MODEL), lambda i: (i, 0))
    return pl.pallas_call(
        _rnn_out_kernel,
        grid=(s // TM_OUT,),
        in_specs=[row,
                  pl.BlockSpec((1, D_MODEL), lambda i: (0, 0)),
                  row, row,
                  pl.BlockSpec((None, D_MODEL, D_RNN), lambda i: (layer, 0, 1),
                               pipeline_mode=pl.Buffered(1)),
                  pl.BlockSpec((None, D_RNN, D_MODEL), lambda i: (layer, 0, 0),
                               pipeline_mode=pl.Buffered(1))],
        out_specs=row,
        out_shape=jax.ShapeDtypeStruct((s, D_MODEL), F32),
        compiler_params=_params("arbitrary"),
        name="rnn_out_proj",
    )(x, g, hf, hb, w_in_all, w_out_all)


def _ffn_kernel(x_ref, g_ref, wg_ref, wu_ref, wd_ref, o_ref, hn_ref):
    @pl.when(pl.program_id(1) == 0)
    def _():
        x = x_ref[...]
        hn_ref[...] = _rms_normed(x, g_ref[...]).astype(BF16)
        o_ref[...] = x

    hn = hn_ref[...]
    gate = jnp.dot(hn, wg_ref[...].astype(BF16), preferred_element_type=F32)
    up = jnp.dot(hn, wu_ref[...].astype(BF16), preferred_element_type=F32)
    act = (jax.nn.silu(gate) * up).astype(BF16)
    o_ref[...] += jnp.dot(act, wd_ref[...].astype(BF16), preferred_element_type=F32)


def _ffn(x, g, w_gate, w_up, w_down, layer):
    s = x.shape[0]
    return pl.pallas_call(
        _ffn_kernel,
        grid=(s // TM_FFN, D_FF // TF),
        in_specs=[
            pl.BlockSpec((TM_FFN, D_MODEL), lambda i, f: (i, 0)),
            pl.BlockSpec((1, D_MODEL), lambda i, f: (0, 0)),
            pl.BlockSpec((None, D_MODEL, TF), lambda i, f: (layer, 0, f)),
            pl.BlockSpec((None, D_MODEL, TF), lambda i, f: (layer, 0, f)),
            pl.BlockSpec((None, TF, D_MODEL), lambda i, f: (layer, f, 0)),
        ],
        out_specs=pl.BlockSpec((TM_FFN, D_MODEL), lambda i, f: (i, 0)),
        out_shape=jax.ShapeDtypeStruct((s, D_MODEL), F32),
        scratch_shapes=[pltpu.VMEM((TM_FFN, D_MODEL), BF16)],
        compiler_params=_params("arbitrary", "arbitrary"),
        name="ffn",
    )(x, g, w_gate, w_up, w_down)


def _rope_tables(seq_len):
    rows_n = seq_len // GRID_W
    freqs = ROPE_THETA ** (-jnp.arange(ROPE_FREQS, dtype=F32) / ROPE_FREQS)
    row_ang = jnp.arange(rows_n, dtype=F32)[:, None, None] * freqs
    col_ang = jnp.arange(GRID_W, dtype=F32)[None, :, None] * freqs
    shape = (rows_n, GRID_W, ROPE_FREQS)
    row_ang = jnp.broadcast_to(row_ang, shape).reshape(seq_len, ROPE_FREQS)
    col_ang = jnp.broadcast_to(col_ang, shape).reshape(seq_len, ROPE_FREQS)
    cr, sr, cc, sc = jnp.cos(row_ang), jnp.sin(row_ang), jnp.cos(col_ang), jnp.sin(col_ang)
    return (jnp.concatenate([cr, cr, cc, cc], axis=-1),
            jnp.concatenate([-sr, sr, -sc, sc], axis=-1))


def kernel(x, norm_mix, norm_ffn, attn_w_qkv, attn_q_gain, attn_k_gain, attn_w_o, rnn_w_in, rnn_conv_w, rnn_conv_b, rnn_w_a, rnn_b_a, rnn_w_i, rnn_b_i, rnn_lambda, rnn_w_out, ffn_w_gate, ffn_w_up, ffn_w_down):
    b, s, d = x.shape
    assert (b, d) == (1, D_MODEL) and s % max(TM, TQ, TK, TT) == 0
    cos_t, sin_t = _rope_tables(s)
    w_o = attn_w_o.astype(BF16)
    w_in, w_out = rnn_w_in.astype(BF16), rnn_w_out.astype(BF16)
    w_a, w_i = (0.5 * rnn_w_a).astype(BF16), (0.5 * rnn_w_i).astype(BF16)
    h = x.reshape(s, d)
    for i in range(DEPTH):
        j = i // N_MIXERS
        g_mix = norm_mix[i].reshape(1, d)
        if i % N_MIXERS == 0:
            qkv = _qkv_proj(h, g_mix, attn_w_qkv, j,
                            attn_q_gain[j].reshape(1, HEAD_DIM), attn_k_gain[j].reshape(1, HEAD_DIM),
                            cos_t, sin_t)
            o = _attention(qkv)
            h = _proj_residual(o, w_o, j, h)
        else:
            xb = _rnn_in_proj(h, g_mix, w_in, j)
            hf, hb = _rnn_core(xb, rnn_conv_w[j], rnn_conv_b[j].reshape(1, D_RNN),
                               w_a[j], rnn_b_a[j], w_i[j], rnn_b_i[j], rnn_lambda[j])
            h = _rnn_out_proj(h, g_mix, hf, hb, w_in, w_out, j)
        h = _ffn(h, norm_ffn[i].reshape(1, d), ffn_w_gate, ffn_w_up, ffn_w_down, i)
    return h.reshape(b, s, d)
```

```python
import jax
import jax.numpy as jnp
from jax import lax
from jax.experimental import pallas as pl
from jax.experimental.pallas import tpu as pltpu

F32 = jnp.float32
BF16 = jnp.bfloat16

D_MODEL = 2048
DEPTH = 4
N_MIXERS = 2
GRID_W = 64
ROPE_THETA = 10000.0
HEAD_DIM = 128
N_Q_HEADS = 16
N_KV_HEADS = 4
GQA_GROUP = N_Q_HEADS // N_KV_HEADS
ROPE_FREQS = HEAD_DIM // 4
QKV_DIM = (N_Q_HEADS + 2 * N_KV_HEADS) * HEAD_DIM
D_RNN = D_MODEL
RNN_BLOCK_W = 256
RNN_BLOCKS = D_RNN // RNN_BLOCK_W
CONV_W = 4
CONV_LEFT = 2
LRU_C = 8.0
D_FF = 5632
EPS = 1e-6
Q_SCALE = HEAD_DIM ** -0.5 * 1.4426950408889634

SUBLANES = 8
BF16_SUBLANES = 16
LANES = 128
MXU_COLS = 256
VMEM_LIMIT_BYTES = 56 * 1024 * 1024

TM = 512
TM_OUT = 256
HEADS_PER_SLAB = MXU_COLS // HEAD_DIM
QKV_SLABS = QKV_DIM // MXU_COLS
QKV_Q_SLABS = N_Q_HEADS // HEADS_PER_SLAB
QKV_K_SLABS = N_KV_HEADS // HEADS_PER_SLAB
TM_FFN = 1024
TF = 256
TQ = 512
TK = 1024
TT = 512
SCAN_BLOCKS = SUBLANES
SCAN_ROWS = TT // SCAN_BLOCKS
SCAN_PITCH = SCAN_ROWS + SUBLANES
LANE_SLABS = RNN_BLOCK_W // LANES
VT_ROWS = HEAD_DIM + BF16_SUBLANES
EXP_ROWS = 32

def _params(*sem):
    return pltpu.CompilerParams(dimension_semantics=sem, vmem_limit_bytes=VMEM_LIMIT_BYTES)


def _rms_normed(x, g):
    ms = jnp.mean(x * x, axis=-1, keepdims=True)
    return x * lax.rsqrt(ms + EPS) * g


def _qkv_kernel(x_ref, g_ref, w_ref, qg_ref, kg_ref, cos_ref, sin_ref, o_ref, hn_ref, ya_ref, yb_ref):
    hn_ref[...] = _rms_normed(x_ref[...], g_ref[...]).astype(BF16)
    q_gain = qg_ref[...] * Q_SCALE
    lane = lax.broadcasted_iota(jnp.int32, (TM, HEAD_DIM), 1)
    first_half = (lane & (2 * ROPE_FREQS - 1)) < ROPE_FREQS

    y_refs = (ya_ref, yb_ref)

    def matmul(s):
        w = w_ref[:, s * MXU_COLS:(s + 1) * MXU_COLS].astype(BF16)
        y_refs[s % 2][...] = jnp.dot(hn_ref[...], w, preferred_element_type=F32)

    def epilogue(s):
        for h in range(s * HEADS_PER_SLAB, (s + 1) * HEADS_PER_SLAB):
            lanes = slice((h % HEADS_PER_SLAB) * HEAD_DIM, (h % HEADS_PER_SLAB + 1) * HEAD_DIM)
            yh = y_refs[s % 2][:, lanes]
            if h >= N_Q_HEADS + N_KV_HEADS:
                out = yh
            else:
                gain = q_gain if h < N_Q_HEADS else kg_ref[...]
                yn = yh * lax.rsqrt(jnp.mean(yh * yh, axis=-1, keepdims=True) + EPS) * gain
                partner = jnp.where(first_half,
                                    pltpu.roll(yn, HEAD_DIM - ROPE_FREQS, 1),
                                    pltpu.roll(yn, ROPE_FREQS, 1))
                out = yn * cos_ref[...] + partner * sin_ref[...]
            o_ref[:, h * HEAD_DIM:(h + 1) * HEAD_DIM] = out.astype(BF16)

    every_tile = pl.program_id(0) >= 0
    matmul(0)
    for s in range(1, QKV_SLABS):
        @pl.when(every_tile)
        def _(s=s):
            epilogue(s - 1)
            matmul(s)
    epilogue(QKV_SLABS - 1)


def _qkv_proj(x, g, w_all, layer, q_gain, k_gain, cos_t, sin_t):
    s = x.shape[0]
    return pl.pallas_call(
        _qkv_kernel,
        grid=(s // TM,),
        in_specs=[
            pl.BlockSpec((TM, D_MODEL), lambda i: (i, 0)),
            pl.BlockSpec((1, D_MODEL), lambda i: (0, 0)),
            pl.BlockSpec((None, D_MODEL, QKV_DIM), lambda i: (layer, 0, 0),
                         pipeline_mode=pl.Buffered(1)),
            pl.BlockSpec((1, HEAD_DIM), lambda i: (0, 0)),
            pl.BlockSpec((1, HEAD_DIM), lambda i: (0, 0)),
            pl.BlockSpec((TM, HEAD_DIM), lambda i: (i, 0)),
            pl.BlockSpec((TM, HEAD_DIM), lambda i: (i, 0)),
        ],
        out_specs=pl.BlockSpec((TM, QKV_DIM), lambda i: (i, 0)),
        out_shape=jax.ShapeDtypeStruct((s, QKV_DIM), BF16),
        scratch_shapes=[pltpu.VMEM((TM, D_MODEL), BF16),
                        pltpu.VMEM((TM, MXU_COLS), F32),
                        pltpu.VMEM((TM, MXU_COLS), F32)],
        compiler_params=_params("arbitrary"),
        name="qkv_proj",
    )(x, g, w_all, q_gain, k_gain, cos_t, sin_t)


def _attn_kernel(q_ref, k_ref, v_ref, o_ref, vt_ref, acc_ref, sa_ref, sb_ref, pa_ref, pb_ref, state_ref):
    n_chunks = k_ref.shape[0] // TK
    i = pl.program_id(1)
    last_tile = pl.num_programs(1) - 1
    heads = range(GQA_GROUP)

    def score_stage(u, st_ref):
        k0 = (u % n_chunks) * TK
        k0 = k0 if isinstance(k0, int) else pl.multiple_of(k0, TK)
        q0 = pl.multiple_of(jnp.minimum(u // n_chunks, last_tile) * TQ, TQ)
        kc = k_ref[pl.ds(k0, TK), :]
        chunk_max = []
        for g in heads:
            q = q_ref[pl.ds(q0, TQ), g * HEAD_DIM:(g + 1) * HEAD_DIM]
            st = lax.dot_general(kc, q, (((1,), (1,)), ((), ())), preferred_element_type=F32)
            st_ref[g] = st
            chunk_max.append(jnp.max(st, axis=0, keepdims=True))
        return tuple(chunk_max)

    def exp_stage(u, st_ref, pt_ref, ms, chunk_max):
        first = u % n_chunks == 0
        new_ms, alphas = [], []
        for g in heads:
            m_old = jnp.where(first, -jnp.inf, ms[g])
            m_new = jnp.maximum(m_old, chunk_max[g])
            for r in range(0, TK, EXP_ROWS):
                pt_ref[g, r:r + EXP_ROWS, :] = jnp.exp2(st_ref[g, r:r + EXP_ROWS, :] - m_new).astype(BF16)
            new_ms.append(m_new)
            alphas.append(jnp.exp2(m_old - m_new))
        return tuple(new_ms), tuple(alphas)

    def out_stage(c, pt_ref, alphas):
        for g in heads:
            acc_ref[g] = alphas[g] * acc_ref[g] + jnp.dot(vt_ref[c], pt_ref[g], preferred_element_type=F32)

    def save_state(ms, alphas, chunk_max):
        for k, rows in enumerate((ms, alphas, chunk_max)):
            for g in heads:
                state_ref[k, g] = rows[g]

    @pl.when(i == 0)
    def _():
        for c in range(n_chunks):
            vc = v_ref[c * TK:(c + 1) * TK, :].astype(F32)
            vt_ref[c, :HEAD_DIM, :] = vc.T.astype(BF16)
            vt_ref[c, HEAD_DIM:, :] = jnp.ones((VT_ROWS - HEAD_DIM, TK), BF16)
        acc_ref[...] = jnp.zeros_like(acc_ref)
        ms = (jnp.full((1, TQ), -jnp.inf, F32),) * GQA_GROUP
        chunk_max = score_stage(0, sa_ref)
        ms, alphas = exp_stage(0, sa_ref, pa_ref, ms, chunk_max)
        save_state(ms, alphas, score_stage(1, sb_ref))

    def trip(c, carry, s_cur, p_cur, s_nxt, p_nxt):
        u = i * n_chunks + c
        ms, alphas, chunk_max = carry
        out_stage(c, p_cur, alphas)
        ms, alphas = exp_stage(u + 1, s_nxt, p_nxt, ms, chunk_max)
        chunk_max = score_stage(u + 2, s_cur)
        return ms, alphas, chunk_max

    def step(c, carry):
        return lax.cond(c % 2 == 0,
                        lambda cr: trip(c, cr, sa_ref, pa_ref, sb_ref, pb_ref),
                        lambda cr: trip(c, cr, sb_ref, pb_ref, sa_ref, pa_ref),
                        carry)

    carry = tuple(tuple(state_ref[k, g] for g in heads) for k in range(3))
    save_state(*lax.fori_loop(0, n_chunks, step, carry))
    for g in heads:
        acc = acc_ref[g]
        out_t = acc[:HEAD_DIM, :] / acc[HEAD_DIM:HEAD_DIM + 1, :]
        o_ref[:, g * HEAD_DIM:(g + 1) * HEAD_DIM] = out_t.T.astype(BF16)


def _attention(qkv):
    s = qkv.shape[0]
    assert (s // TK) % 2 == 0, "the a / b buffer parity of the attention pipeline needs an even chunk count"
    group_w = GQA_GROUP * HEAD_DIM
    return pl.pallas_call(
        _attn_kernel,
        grid=(N_KV_HEADS, s // TQ),
        in_specs=[
            pl.BlockSpec((s, group_w), lambda h, i: (0, h), pipeline_mode=pl.Buffered(1)),
            pl.BlockSpec((s, HEAD_DIM), lambda h, i: (0, N_Q_HEADS + h)),
            pl.BlockSpec((s, HEAD_DIM), lambda h, i: (0, N_Q_HEADS + N_KV_HEADS + h)),
        ],
        out_specs=pl.BlockSpec((TQ, group_w), lambda h, i: (i, h)),
        out_shape=jax.ShapeDtypeStruct((s, N_Q_HEADS * HEAD_DIM), BF16),
        scratch_shapes=[pltpu.VMEM((s // TK, VT_ROWS, TK), BF16),
                        pltpu.VMEM((GQA_GROUP, VT_ROWS, TQ), F32),
                        pltpu.VMEM((GQA_GROUP, TK, TQ), F32),
                        pltpu.VMEM((GQA_GROUP, TK, TQ), F32),
                        pltpu.VMEM((GQA_GROUP, TK, TQ), BF16),
                        pltpu.VMEM((GQA_GROUP, TK, TQ), BF16),
                        pltpu.VMEM((3, GQA_GROUP, 1, TQ), F32)],
        compiler_params=_params("arbitrary", "arbitrary"),
        name="attention",
    )(qkv, qkv, qkv)


def _proj_res_kernel(a_ref, w_ref, x_ref, o_ref):
    o_ref[...] = x_ref[...] + jnp.dot(a_ref[...], w_ref[...], preferred_element_type=F32)


def _proj_residual(a, w_all, layer, x):
    s, k = a.shape
    return pl.pallas_call(
        _proj_res_kernel,
        grid=(s // TM,),
        in_specs=[
            pl.BlockSpec((TM, k), lambda i: (i, 0)),
            pl.BlockSpec((None, k, D_MODEL), lambda i: (layer, 0, 0), pipeline_mode=pl.Buffered(1)),
            pl.BlockSpec((TM, D_MODEL), lambda i: (i, 0)),
        ],
        out_specs=pl.BlockSpec((TM, D_MODEL), lambda i: (i, 0)),
        out_shape=jax.ShapeDtypeStruct((s, D_MODEL), F32),
        compiler_params=_params("arbitrary"),
        name="proj_residual",
    )(a, w_all, x)


def _rnn_in_kernel(x_ref, g_ref, w_ref, xb_ref):
    hn = _rms_normed(x_ref[...], g_ref[...]).astype(BF16)
    xb_ref[...] = jnp.dot(hn, w_ref[...], preferred_element_type=F32)


def _rnn_in_proj(x, g, w_all, layer):
    s = x.shape[0]
    return pl.pallas_call(
        _rnn_in_kernel,
        grid=(s // TM,),
        in_specs=[
            pl.BlockSpec((TM, D_MODEL), lambda i: (i, 0)),
            pl.BlockSpec((1, D_MODEL), lambda i: (0, 0)),
            pl.BlockSpec((None, D_MODEL, D_RNN), lambda i: (layer, 0, 0), pipeline_mode=pl.Buffered(1)),
        ],
        out_specs=pl.BlockSpec((TM, D_RNN), lambda i: (i, 0)),
        out_shape=jax.ShapeDtypeStruct((s, D_RNN), F32),
        compiler_params=_params("arbitrary"),
        name="rnn_in_proj",
    )(x, g, w_all)


def _softplus(z):
    return jnp.maximum(z, 0.0) + jnp.log1p(jnp.exp(-jnp.abs(z)))


def _lru_inputs(d, cur_ref, prev_ref, next_ref, first, last, cw_ref, cb_ref, wa_ref, ba_ref, wi_ref,
                bi_ref, lam_ref, ext_ref, a_ref, u_ref):
    ext_ref[d, 0:SUBLANES, :] = jnp.where(first, 0.0, prev_ref[...])
    ext_ref[d, SUBLANES:SUBLANES + TT, :] = cur_ref[...]
    ext_ref[d, SUBLANES + TT:2 * SUBLANES + TT, :] = jnp.where(last, 0.0, next_ref[...])
    ext = ext_ref[d]
    xc = cb_ref[...]
    for k in range(CONV_W):
        shift = (CONV_LEFT - k) % ext.shape[0]
        tap = ext if shift == 0 else pltpu.roll(ext, shift, 0)
        xc = xc + cw_ref[k:k + 1, :] * tap[SUBLANES:SUBLANES + TT, :]
    xb16 = xc.astype(BF16)
    ta = jnp.tanh(jnp.dot(xb16, wa_ref[d, 0], preferred_element_type=F32) + 0.5 * ba_ref[d:d + 1, :])
    ti = jnp.tanh(jnp.dot(xb16, wi_ref[d, 0], preferred_element_type=F32) + 0.5 * bi_ref[d:d + 1, :])
    i = 0.5 * ti + 0.5
    half_c = (-0.5 * LRU_C) * _softplus(-lam_ref[d:d + 1, :])
    log_a = ta * half_c + half_c
    a = jnp.exp(log_a)
    one_minus_a2 = -jnp.tanh(log_a) * (a * a + 1.0)
    root = jnp.where(one_minus_a2 > 0.0, one_minus_a2 * lax.rsqrt(one_minus_a2), 0.0)
    u = root * (i * xc)
    for s in range(SCAN_BLOCKS):
        for l in range(LANE_SLABS):
            rows = slice(s * SCAN_ROWS, (s + 1) * SCAN_ROWS)
            lanes = slice(l * LANES, (l + 1) * LANES)
            a_ref[d, l, s * SCAN_PITCH:s * SCAN_PITCH + SCAN_ROWS, :] = a[rows, lanes]
            u_ref[d, l, s * SCAN_PITCH:s * SCAN_PITCH + SCAN_ROWS, :] = u[rows, lanes]


def _rnn_core_kernel(fc_ref, fp_ref, fn_ref, bc_ref, bp_ref, bn_ref, cw_ref, cb_ref, wa_ref, ba_ref,
                     wi_ref, bi_ref, lam_ref, hf_ref, hb_ref, ext_ref, a_ref, u_ref, hl_ref, p_ref,
                     carry_ref):
    i = pl.program_id(1)
    n = pl.num_programs(1)

    @pl.when(i == 0)
    def _():
        carry_ref[...] = jnp.zeros_like(carry_ref)

    shared = (cw_ref, cb_ref, wa_ref, ba_ref, wi_ref, bi_ref, lam_ref, ext_ref, a_ref, u_ref)
    _lru_inputs(0, fc_ref, fp_ref, fn_ref, i == 0, i == n - 1, *shared)
    _lru_inputs(1, bc_ref, bp_ref, bn_ref, i == n - 1, i == 0, *shared)

    pairs = [(d, l) for d in range(2) for l in range(LANE_SLABS)]

    def body(jj, carry):
        out = []
        for (d, l), (h, p) in zip(pairs, carry):
            j = jj if d == 0 else SCAN_ROWS - 1 - jj
            rows = pl.ds(j, SCAN_BLOCKS, stride=SCAN_PITCH)
            a = a_ref[d, l, rows, :]
            h = a * h + u_ref[d, l, rows, :]
            p = a * p
            hl_ref[d, l, rows, :] = h
            p_ref[d, l, rows, :] = p
            out.append((h, p))
        return tuple(out)

    zero = jnp.zeros((SCAN_BLOCKS, LANES), F32)
    ends = lax.fori_loop(0, SCAN_ROWS, body, ((zero, zero + 1.0),) * len(pairs), unroll=4)

    for (d, l), (h_end, p_end) in zip(pairs, ends):
        lanes = slice(l * LANES, (l + 1) * LANES)
        out_ref = hf_ref if d == 0 else hb_ref
        c = carry_ref[d, 0:1, lanes]
        for s in (range(SCAN_BLOCKS) if d == 0 else reversed(range(SCAN_BLOCKS))):
            blk = slice(s * SCAN_PITCH, s * SCAN_PITCH + SCAN_ROWS)
            out_ref[s * SCAN_ROWS:(s + 1) * SCAN_ROWS, lanes] = hl_ref[d, l, blk, :] + p_ref[d, l, blk, :] * c
            c = h_end[s:s + 1, :] + p_end[s:s + 1, :] * c
        carry_ref[d, :, lanes] = jnp.broadcast_to(c, (SUBLANES, LANES))


def _rnn_core(xb, conv_w, conv_b, w_a, b_a, w_i, b_i, lam):
    s = xb.shape[0]
    n = s // TT
    halo = TT // SUBLANES
    last_halo = s // SUBLANES - 1
    cur = lambda t: pl.BlockSpec((TT, RNN_BLOCK_W), lambda c, i: (t(i, n), c))
    prv = lambda t: pl.BlockSpec((SUBLANES, RNN_BLOCK_W),
                                 lambda c, i: (jnp.maximum(t(i, n) * halo - 1, 0), c))
    nxt = lambda t: pl.BlockSpec((SUBLANES, RNN_BLOCK_W),
                                 lambda c, i: (jnp.minimum((t(i, n) + 1) * halo, last_halo), c))
    fwd = lambda i, n: i
    bwd = lambda i, n: n - 1 - i
    chan = lambda rows: pl.BlockSpec((rows, RNN_BLOCK_W), lambda c, i: (0, c))
    gate_w = pl.BlockSpec((2, 1, RNN_BLOCK_W, RNN_BLOCK_W), lambda c, i: (0, c, 0, 0))
    return pl.pallas_call(
        _rnn_core_kernel,
        grid=(RNN_BLOCKS, n),
        in_specs=[cur(fwd), prv(fwd), nxt(fwd), cur(bwd), prv(bwd), nxt(bwd),
                  chan(CONV_W), chan(1), gate_w, chan(2), gate_w, chan(2), chan(2)],
        out_specs=[pl.BlockSpec((TT, RNN_BLOCK_W), lambda c, i: (i, c)),
                   pl.BlockSpec((TT, RNN_BLOCK_W), lambda c, i: (n - 1 - i, c))],
        out_shape=[jax.ShapeDtypeStruct((s, D_RNN), F32), jax.ShapeDtypeStruct((s, D_RNN), F32)],
        scratch_shapes=[
            pltpu.VMEM((2, TT + 2 * SUBLANES, RNN_BLOCK_W), F32),
            pltpu.VMEM((2, LANE_SLABS, SCAN_BLOCKS * SCAN_PITCH, LANES), F32),
            pltpu.VMEM((2, LANE_SLABS, SCAN_BLOCKS * SCAN_PITCH, LANES), F32),
            pltpu.VMEM((2, LANE_SLABS, SCAN_BLOCKS * SCAN_PITCH, LANES), F32),
            pltpu.VMEM((2, LANE_SLABS, SCAN_BLOCKS * SCAN_PITCH, LANES), F32),
            pltpu.VMEM((2, SUBLANES, RNN_BLOCK_W), F32),
        ],
        compiler_params=_params("arbitrary", "arbitrary"),
        name="rnn_core",
    )(xb, xb, xb, xb, xb, xb, conv_w, conv_b, w_a, b_a, w_i, b_i, lam)


def _rnn_out_kernel(x_ref, g_ref, hf_ref, hb_ref, wy_ref, wo_ref, o_ref):
    x = x_ref[...]
    hn = _rms_normed(x, g_ref[...]).astype(BF16)
    gate = jax.nn.gelu(jnp.dot(hn, wy_ref[...], preferred_element_type=F32), approximate=True)
    a = ((hf_ref[...] + hb_ref[...]) * gate).astype(BF16)
    o_ref[...] = x + jnp.dot(a, wo_ref[...], preferred_element_type=F32)


def _rnn_out_proj(x, g, hf, hb, w_in_all, w_out_all, layer):
    s = x.shape[0]
    row = pl.BlockSpec((TM_OUT, D_MODEL), lambda i: (i, 0))
    return pl.pallas_call(
        _rnn_out_kernel,
        grid=(s // TM_OUT,),
        in_specs=[row,
                  pl.BlockSpec((1, D_MODEL), lambda i: (0, 0)),
                  row, row,
                  pl.BlockSpec((None, D_MODEL, D_RNN), lambda i: (layer, 0, 1),
                               pipeline_mode=pl.Buffered(1)),
                  pl.BlockSpec((None, D_RNN, D_MODEL), lambda i: (layer, 0, 0),
                               pipeline_mode=pl.Buffered(1))],
        out_specs=row,
        out_shape=jax.ShapeDtypeStruct((s, D_MODEL), F32),
        compiler_params=_params("arbitrary"),
        name="rnn_out_proj",
    )(x, g, hf, hb, w_in_all, w_out_all)


def _ffn_kernel(x_ref, g_ref, wg_ref, wu_ref, wd_ref, o_ref, hn_ref):
    @pl.when(pl.program_id(1) == 0)
    def _():
        x = x_ref[...]
        hn_ref[...] = _rms_normed(x, g_ref[...]).astype(BF16)
        o_ref[...] = x

    hn = hn_ref[...]
    gate = jnp.dot(hn, wg_ref[...].astype(BF16), preferred_element_type=F32)
    up = jnp.dot(hn, wu_ref[...].astype(BF16), preferred_element_type=F32)
    act = (jax.nn.silu(gate) * up).astype(BF16)
    o_ref[...] += jnp.dot(act, wd_ref[...].astype(BF16), preferred_element_type=F32)


def _ffn(x, g, w_gate, w_up, w_down, layer):
    s = x.shape[0]
    return pl.pallas_call(
        _ffn_kernel,
        grid=(s // TM_FFN, D_FF // TF),
        in_specs=[
            pl.BlockSpec((TM_FFN, D_MODEL), lambda i, f: (i, 0)),
            pl.BlockSpec((1, D_MODEL), lambda i, f: (0, 0)),
            pl.BlockSpec((None, D_MODEL, TF), lambda i, f: (layer, 0, f)),
            pl.BlockSpec((None, D_MODEL, TF), lambda i, f: (layer, 0, f)),
            pl.BlockSpec((None, TF, D_MODEL), lambda i, f: (layer, f, 0)),
        ],
        out_specs=pl.BlockSpec((TM_FFN, D_MODEL), lambda i, f: (i, 0)),
        out_shape=jax.ShapeDtypeStruct((s, D_MODEL), F32),
        scratch_shapes=[pltpu.VMEM((TM_FFN, D_MODEL), BF16)],
        compiler_params=_params("arbitrary", "arbitrary"),
        name="ffn",
    )(x, g, w_gate, w_up, w_down)


def _rope_tables(seq_len):
    rows_n = seq_len // GRID_W
    freqs = ROPE_THETA ** (-jnp.arange(ROPE_FREQS, dtype=F32) / ROPE_FREQS)
    row_ang = jnp.arange(rows_n, dtype=F32)[:, None, None] * freqs
    col_ang = jnp.arange(GRID_W, dtype=F32)[None, :, None] * freqs
    shape = (rows_n, GRID_W, ROPE_FREQS)
    row_ang = jnp.broadcast_to(row_ang, shape).reshape(seq_len, ROPE_FREQS)
    col_ang = jnp.broadcast_to(col_ang, shape).reshape(seq_len, ROPE_FREQS)
    cr, sr, cc, sc = jnp.cos(row_ang), jnp.sin(row_ang), jnp.cos(col_ang), jnp.sin(col_ang)
    return (jnp.concatenate([cr, cr, cc, cc], axis=-1),
            jnp.concatenate([-sr, sr, -sc, sc], axis=-1))


def kernel(x, norm_mix, norm_ffn, attn_w_qkv, attn_q_gain, attn_k_gain, attn_w_o, rnn_w_in, rnn_conv_w, rnn_conv_b, rnn_w_a, rnn_b_a, rnn_w_i, rnn_b_i, rnn_lambda, rnn_w_out, ffn_w_gate, ffn_w_up, ffn_w_down):
    b, s, d = x.shape
    assert (b, d) == (1, D_MODEL) and s % max(TM, TQ, TK, TT) == 0
    cos_t, sin_t = _rope_tables(s)
    w_o = attn_w_o.astype(BF16)
    w_in, w_out = rnn_w_in.astype(BF16), rnn_w_out.astype(BF16)
    w_a, w_i = (0.5 * rnn_w_a).astype(BF16), (0.5 * rnn_w_i).astype(BF16)
    h = x.reshape(s, d)
    for i in range(DEPTH):
        j = i // N_MIXERS
        g_mix = norm_mix[i].reshape(1, d)
        if i % N_MIXERS == 0:
            qkv = _qkv_proj(h, g_mix, attn_w_qkv, j,
                            attn_q_gain[j].reshape(1, HEAD_DIM), attn_k_gain[j].reshape(1, HEAD_DIM),
                            cos_t, sin_t)
            o = _attention(qkv)
            h = _proj_residual(o, w_o, j, h)
        else:
            xb = _rnn_in_proj(h, g_mix, w_in, j)
            hf, hb = _rnn_core(xb, rnn_conv_w[j], rnn_conv_b[j].reshape(1, D_RNN),
                               w_a[j], rnn_b_a[j], w_i[j], rnn_b_i[j], rnn_lambda[j])
            h = _rnn_out_proj(h, g_mix, hf, hb, w_in, w_out, j)
        h = _ffn(h, norm_ffn[i].reshape(1, d), ffn_w_gate, ffn_w_up, ffn_w_down, i)
    return h.reshape(b, s, d)
```

```python
import jax
import jax.numpy as jnp
from jax import lax
from jax.experimental import pallas as pl
from jax.experimental.pallas import tpu as pltpu

F32 = jnp.float32
BF16 = jnp.bfloat16

D_MODEL = 2048
DEPTH = 4
N_MIXERS = 2
GRID_W = 64
ROPE_THETA = 10000.0
HEAD_DIM = 128
N_Q_HEADS = 16
N_KV_HEADS = 4
GQA_GROUP = N_Q_HEADS // N_KV_HEADS
ROPE_FREQS = HEAD_DIM // 4
QKV_DIM = (N_Q_HEADS + 2 * N_KV_HEADS) * HEAD_DIM
D_RNN = D_MODEL
RNN_BLOCK_W = 256
RNN_BLOCKS = D_RNN // RNN_BLOCK_W
CONV_W = 4
CONV_LEFT = 2
LRU_C = 8.0
D_FF = 5632
EPS = 1e-6
Q_SCALE = HEAD_DIM ** -0.5 * 1.4426950408889634

SUBLANES = 8
BF16_SUBLANES = 16
LANES = 128
MXU_COLS = 256
VMEM_LIMIT_BYTES = 56 * 1024 * 1024

TM = 512
TM_OUT = 256
HEADS_PER_SLAB = MXU_COLS // HEAD_DIM
QKV_SLABS = QKV_DIM // MXU_COLS
QKV_Q_SLABS = N_Q_HEADS // HEADS_PER_SLAB
QKV_K_SLABS = N_KV_HEADS // HEADS_PER_SLAB
TM_FFN = 1024
TF = 256
TQ = 512
TK = 1024
TT = 512
SCAN_BLOCKS = SUBLANES
SCAN_ROWS = TT // SCAN_BLOCKS
SCAN_PITCH = SCAN_ROWS + SUBLANES
LANE_SLABS = RNN_BLOCK_W // LANES
VT_ROWS = HEAD_DIM + BF16_SUBLANES
EXP_ROWS = 32

def _params(*sem):
    return pltpu.CompilerParams(dimension_semantics=sem, vmem_limit_bytes=VMEM_LIMIT_BYTES)


def _rms_normed(x, g):
    ms = jnp.mean(x * x, axis=-1, keepdims=True)
    return x * lax.rsqrt(ms + EPS) * g


def _qkv_kernel(x_ref, g_ref, w_ref, qg_ref, kg_ref, cos_ref, sin_ref, o_ref, hn_ref, ya_ref, yb_ref):
    hn_ref[...] = _rms_normed(x_ref[...], g_ref[...]).astype(BF16)
    q_gain = qg_ref[...] * Q_SCALE
    lane = lax.broadcasted_iota(jnp.int32, (TM, HEAD_DIM), 1)
    first_half = (lane & (2 * ROPE_FREQS - 1)) < ROPE_FREQS

    y_refs = (ya_ref, yb_ref)

    def matmul(s):
        w = w_ref[:, s * MXU_COLS:(s + 1) * MXU_COLS].astype(BF16)
        y_refs[s % 2][...] = jnp.dot(hn_ref[...], w, preferred_element_type=F32)

    def epilogue(s):
        for h in range(s * HEADS_PER_SLAB, (s + 1) * HEADS_PER_SLAB):
            lanes = slice((h % HEADS_PER_SLAB) * HEAD_DIM, (h % HEADS_PER_SLAB + 1) * HEAD_DIM)
            yh = y_refs[s % 2][:, lanes]
            if h >= N_Q_HEADS + N_KV_HEADS:
                out = yh
            else:
                gain = q_gain if h < N_Q_HEADS else kg_ref[...]
                yn = yh * lax.rsqrt(jnp.mean(yh * yh, axis=-1, keepdims=True) + EPS) * gain
                partner = jnp.where(first_half,
                                    pltpu.roll(yn, HEAD_DIM - ROPE_FREQS, 1),
                                    pltpu.roll(yn, ROPE_FREQS, 1))
                out = yn * cos_ref[...] + partner * sin_ref[...]
            o_ref[:, h * HEAD_DIM:(h + 1) * HEAD_DIM] = out.astype(BF16)

    every_tile = pl.program_id(0) >= 0
    matmul(0)
    for s in range(1, QKV_SLABS):
        @pl.when(every_tile)
        def _(s=s):
            epilogue(s - 1)
            matmul(s)
    epilogue(QKV_SLABS - 1)


def _qkv_proj(x, g, w_all, layer, q_gain, k_gain, cos_t, sin_t):
    s = x.shape[0]
    return pl.pallas_call(
        _qkv_kernel,
        grid=(s // TM,),
        in_specs=[
            pl.BlockSpec((TM, D_MODEL), lambda i: (i, 0)),
            pl.BlockSpec((1, D_MODEL), lambda i: (0, 0)),
            pl.BlockSpec((None, D_MODEL, QKV_DIM), lambda i: (layer, 0, 0),
                         pipeline_mode=pl.Buffered(1)),
            pl.BlockSpec((1, HEAD_DIM), lambda i: (0, 0)),
            pl.BlockSpec((1, HEAD_DIM), lambda i: (0, 0)),
            pl.BlockSpec((TM, HEAD_DIM), lambda i: (i, 0)),
            pl.BlockSpec((TM, HEAD_DIM), lambda i: (i, 0)),
        ],
        out_specs=pl.BlockSpec((TM, QKV_DIM), lambda i: (i, 0)),
        out_shape=jax.ShapeDtypeStruct((s, QKV_DIM), BF16),
        scratch_shapes=[pltpu.VMEM((TM, D_MODEL), BF16),
                        pltpu.VMEM((TM, MXU_COLS), F32),
                        pltpu.VMEM((TM, MXU_COLS), F32)],
        compiler_params=_params("arbitrary"),
        name="qkv_proj",
    )(x, g, w_all, q_gain, k_gain, cos_t, sin_t)


def _attn_kernel(q_ref, k_ref, v_ref, o_ref, vt_ref, acc_ref, sa_ref, sb_ref, pa_ref, pb_ref, state_ref):
    n_chunks = k_ref.shape[0] // TK
    i = pl.program_id(1)
    last_tile = pl.num_programs(1) - 1
    heads = range(GQA_GROUP)

    def score_stage(u, st_ref):
        k0 = (u % n_chunks) * TK
        k0 = k0 if isinstance(k0, int) else pl.multiple_of(k0, TK)
        q0 = pl.multiple_of(jnp.minimum(u // n_chunks, last_tile) * TQ, TQ)
        kc = k_ref[pl.ds(k0, TK), :]
        chunk_max = []
        for g in heads:
            q = q_ref[pl.ds(q0, TQ), g * HEAD_DIM:(g + 1) * HEAD_DIM]
            st = lax.dot_general(kc, q, (((1,), (1,)), ((), ())), preferred_element_type=F32)
            st_ref[g] = st
            chunk_max.append(jnp.max(st, axis=0, keepdims=True))
        return tuple(chunk_max)

    def exp_stage(u, st_ref, pt_ref, ms, chunk_max):
        first = u % n_chunks == 0
        new_ms, alphas = [], []
        for g in heads:
            m_old = jnp.where(first, -jnp.inf, ms[g])
            m_new = jnp.maximum(m_old, chunk_max[g])
            for r in range(0, TK, EXP_ROWS):
                pt_ref[g, r:r + EXP_ROWS, :] = jnp.exp2(st_ref[g, r:r + EXP_ROWS, :] - m_new).astype(BF16)
            new_ms.append(m_new)
            alphas.append(jnp.exp2(m_old - m_new))
        return tuple(new_ms), tuple(alphas)

    def out_stage(c, pt_ref, alphas):
        for g in heads:
            acc_ref[g] = alphas[g] * acc_ref[g] + jnp.dot(vt_ref[c], pt_ref[g], preferred_element_type=F32)

    def save_state(ms, alphas, chunk_max):
        for k, rows in enumerate((ms, alphas, chunk_max)):
            for g in heads:
                state_ref[k, g] = rows[g]

    @pl.when(i == 0)
    def _():
        for c in range(n_chunks):
            vc = v_ref[c * TK:(c + 1) * TK, :].astype(F32)
            vt_ref[c, :HEAD_DIM, :] = vc.T.astype(BF16)
            vt_ref[c, HEAD_DIM:, :] = jnp.ones((VT_ROWS - HEAD_DIM, TK), BF16)
        acc_ref[...] = jnp.zeros_like(acc_ref)
        ms = (jnp.full((1, TQ), -jnp.inf, F32),) * GQA_GROUP
        chunk_max = score_stage(0, sa_ref)
        ms, alphas = exp_stage(0, sa_ref, pa_ref, ms, chunk_max)
        save_state(ms, alphas, score_stage(1, sb_ref))

    def trip(c, carry, s_cur, p_cur, s_nxt, p_nxt):
        u = i * n_chunks + c
        ms, alphas, chunk_max = carry
        out_stage(c, p_cur, alphas)
        ms, alphas = exp_stage(u + 1, s_nxt, p_nxt, ms, chunk_max)
        chunk_max = score_stage(u + 2, s_cur)
        return ms, alphas, chunk_max

    def step(c, carry):
        return lax.cond(c % 2 == 0,
                        lambda cr: trip(c, cr, sa_ref, pa_ref, sb_ref, pb_ref),
                        lambda cr: trip(c, cr, sb_ref, pb_ref, sa_ref, pa_ref),
                        carry)

    carry = tuple(tuple(state_ref[k, g] for g in heads) for k in range(3))
    save_state(*lax.fori_loop(0, n_chunks, step, carry))
    for g in heads:
        acc = acc_ref[g]
        out_t = acc[:HEAD_DIM, :] / acc[HEAD_DIM:HEAD_DIM + 1, :]
        o_ref[:, g * HEAD_DIM:(g + 1) * HEAD_DIM] = out_t.T.astype(BF16)


def _attention(qkv):
    s = qkv.shape[0]
    assert (s // TK) % 2 == 0, "the a / b buffer parity of the attention pipeline needs an even chunk count"
    group_w = GQA_GROUP * HEAD_DIM
    return pl.pallas_call(
        _attn_kernel,
        grid=(N_KV_HEADS, s // TQ),
        in_specs=[
            pl.BlockSpec((s, group_w), lambda h, i: (0, h), pipeline_mode=pl.Buffered(1)),
            pl.BlockSpec((s, HEAD_DIM), lambda h, i: (0, N_Q_HEADS + h)),
            pl.BlockSpec((s, HEAD_DIM), lambda h, i: (0, N_Q_HEADS + N_KV_HEADS + h)),
        ],
        out_specs=pl.BlockSpec((TQ, group_w), lambda h, i: (i, h)),
        out_shape=jax.ShapeDtypeStruct((s, N_Q_HEADS * HEAD_DIM), BF16),
        scratch_shapes=[pltpu.VMEM((s // TK, VT_ROWS, TK), BF16),
                        pltpu.VMEM((GQA_GROUP, VT_ROWS, TQ), F32),
                        pltpu.VMEM((GQA_GROUP, TK, TQ), F32),
                        pltpu.VMEM((GQA_GROUP, TK, TQ), F32),
                        pltpu.VMEM((GQA_GROUP, TK, TQ), BF16),
                        pltpu.VMEM((GQA_GROUP, TK, TQ), BF16),
                        pltpu.VMEM((3, GQA_GROUP, 1, TQ), F32)],
        compiler_params=_params("arbitrary", "arbitrary"),
        name="attention",
    )(qkv, qkv, qkv)


def _proj_res_kernel(a_ref, w_ref, x_ref, o_ref, w16_ref):
    @pl.when(pl.program_id(0) == 0)
    def _():
        w16_ref[...] = w_ref[...].astype(BF16)

    o_ref[...] = x_ref[...] + jnp.dot(a_ref[...], w16_ref[...], preferred_element_type=F32)


def _proj_residual(a, w_all, layer, x):
    s, k = a.shape
    return pl.pallas_call(
        _proj_res_kernel,
        grid=(s // TM,),
        in_specs=[
            pl.BlockSpec((TM, k), lambda i: (i, 0)),
            pl.BlockSpec((None, k, D_MODEL), lambda i: (layer, 0, 0), pipeline_mode=pl.Buffered(1)),
            pl.BlockSpec((TM, D_MODEL), lambda i: (i, 0)),
        ],
        out_specs=pl.BlockSpec((TM, D_MODEL), lambda i: (i, 0)),
        out_shape=jax.ShapeDtypeStruct((s, D_MODEL), F32),
        scratch_shapes=[pltpu.VMEM((k, D_MODEL), BF16)],
        compiler_params=_params("arbitrary"),
        name="proj_residual",
    )(a, w_all, x)


def _rnn_in_kernel(x_ref, g_ref, w_ref, xb_ref, w16_ref):
    @pl.when(pl.program_id(0) == 0)
    def _():
        w16_ref[...] = w_ref[...].astype(BF16)

    hn = _rms_normed(x_ref[...], g_ref[...]).astype(BF16)
    xb_ref[...] = jnp.dot(hn, w16_ref[...], preferred_element_type=F32)


def _rnn_in_proj(x, g, w_all, layer):
    s = x.shape[0]
    return pl.pallas_call(
        _rnn_in_kernel,
        grid=(s // TM,),
        in_specs=[
            pl.BlockSpec((TM, D_MODEL), lambda i: (i, 0)),
            pl.BlockSpec((1, D_MODEL), lambda i: (0, 0)),
            pl.BlockSpec((None, D_MODEL, D_RNN), lambda i: (layer, 0, 0), pipeline_mode=pl.Buffered(1)),
        ],
        out_specs=pl.BlockSpec((TM, D_RNN), lambda i: (i, 0)),
        out_shape=jax.ShapeDtypeStruct((s, D_RNN), F32),
        scratch_shapes=[pltpu.VMEM((D_MODEL, D_RNN), BF16)],
        compiler_params=_params("arbitrary"),
        name="rnn_in_proj",
    )(x, g, w_all)


def _softplus(z):
    return jnp.maximum(z, 0.0) + jnp.log1p(jnp.exp(-jnp.abs(z)))


def _lru_inputs(d, cur_ref, prev_ref, next_ref, first, last, cw_ref, cb_ref, wa_ref, ba_ref, wi_ref,
                bi_ref, lam_ref, ext_ref, a_ref, u_ref):
    ext_ref[d, 0:SUBLANES, :] = jnp.where(first, 0.0, prev_ref[...])
    ext_ref[d, SUBLANES:SUBLANES + TT, :] = cur_ref[...]
    ext_ref[d, SUBLANES + TT:2 * SUBLANES + TT, :] = jnp.where(last, 0.0, next_ref[...])
    ext = ext_ref[d]
    xc = cb_ref[...]
    for k in range(CONV_W):
        shift = (CONV_LEFT - k) % ext.shape[0]
        tap = ext if shift == 0 else pltpu.roll(ext, shift, 0)
        xc = xc + cw_ref[k:k + 1, :] * tap[SUBLANES:SUBLANES + TT, :]
    xb16 = xc.astype(BF16)
    ta = jnp.tanh(jnp.dot(xb16, wa_ref[d, 0], preferred_element_type=F32) + 0.5 * ba_ref[d:d + 1, :])
    ti = jnp.tanh(jnp.dot(xb16, wi_ref[d, 0], preferred_element_type=F32) + 0.5 * bi_ref[d:d + 1, :])
    i = 0.5 * ti + 0.5
    half_c = (-0.5 * LRU_C) * _softplus(-lam_ref[d:d + 1, :])
    log_a = ta * half_c + half_c
    a = jnp.exp(log_a)
    one_minus_a2 = -jnp.tanh(log_a) * (a * a + 1.0)
    root = jnp.where(one_minus_a2 > 0.0, one_minus_a2 * lax.rsqrt(one_minus_a2), 0.0)
    u = root * (i * xc)
    for s in range(SCAN_BLOCKS):
        for l in range(LANE_SLABS):
            rows = slice(s * SCAN_ROWS, (s + 1) * SCAN_ROWS)
            lanes = slice(l * LANES, (l + 1) * LANES)
            a_ref[d, l, s * SCAN_PITCH:s * SCAN_PITCH + SCAN_ROWS, :] = a[rows, lanes]
            u_ref[d, l, s * SCAN_PITCH:s * SCAN_PITCH + SCAN_ROWS, :] = u[rows, lanes]


def _rnn_core_kernel(fc_ref, fp_ref, fn_ref, bc_ref, bp_ref, bn_ref, cw_ref, cb_ref, wa_ref, ba_ref,
                     wi_ref, bi_ref, lam_ref, hf_ref, hb_ref, ext_ref, a_ref, u_ref, hl_ref, p_ref,
                     carry_ref):
    i = pl.program_id(1)
    n = pl.num_programs(1)

    @pl.when(i == 0)
    def _():
        carry_ref[...] = jnp.zeros_like(carry_ref)

    shared = (cw_ref, cb_ref, wa_ref, ba_ref, wi_ref, bi_ref, lam_ref, ext_ref, a_ref, u_ref)
    _lru_inputs(0, fc_ref, fp_ref, fn_ref, i == 0, i == n - 1, *shared)
    _lru_inputs(1, bc_ref, bp_ref, bn_ref, i == n - 1, i == 0, *shared)

    pairs = [(d, l) for d in range(2) for l in range(LANE_SLABS)]

    def body(jj, carry):
        out = []
        for (d, l), (h, p) in zip(pairs, carry):
            j = jj if d == 0 else SCAN_ROWS - 1 - jj
            rows = pl.ds(j, SCAN_BLOCKS, stride=SCAN_PITCH)
            a = a_ref[d, l, rows, :]
            h = a * h + u_ref[d, l, rows, :]
            p = a * p
            hl_ref[d, l, rows, :] = h
            p_ref[d, l, rows, :] = p
            out.append((h, p))
        return tuple(out)

    zero = jnp.zeros((SCAN_BLOCKS, LANES), F32)
    ends = ((zero, zero + 1.0),) * len(pairs)
    for jj in range(SCAN_ROWS):
        ends = body(jj, ends)

    for (d, l), (h_end, p_end) in zip(pairs, ends):
        lanes = slice(l * LANES, (l + 1) * LANES)
        out_ref = hf_ref if d == 0 else hb_ref
        c = carry_ref[d, 0:1, lanes]
        for s in (range(SCAN_BLOCKS) if d == 0 else reversed(range(SCAN_BLOCKS))):
            blk = slice(s * SCAN_PITCH, s * SCAN_PITCH + SCAN_ROWS)
            out_ref[s * SCAN_ROWS:(s + 1) * SCAN_ROWS, lanes] = hl_ref[d, l, blk, :] + p_ref[d, l, blk, :] * c
            c = h_end[s:s + 1, :] + p_end[s:s + 1, :] * c
        carry_ref[d, :, lanes] = jnp.broadcast_to(c, (SUBLANES, LANES))


def _rnn_core(xb, conv_w, conv_b, w_a, b_a, w_i, b_i, lam):
    s = xb.shape[0]
    n = s // TT
    halo = TT // SUBLANES
    last_halo = s // SUBLANES - 1
    cur = lambda t: pl.BlockSpec((TT, RNN_BLOCK_W), lambda c, i: (t(i, n), c))
    prv = lambda t: pl.BlockSpec((SUBLANES, RNN_BLOCK_W),
                                 lambda c, i: (jnp.maximum(t(i, n) * halo - 1, 0), c))
    nxt = lambda t: pl.BlockSpec((SUBLANES, RNN_BLOCK_W),
                                 lambda c, i: (jnp.minimum((t(i, n) + 1) * halo, last_halo), c))
    fwd = lambda i, n: i
    bwd = lambda i, n: n - 1 - i
    chan = lambda rows: pl.BlockSpec((rows, RNN_BLOCK_W), lambda c, i: (0, c))
    gate_w = pl.BlockSpec((2, 1, RNN_BLOCK_W, RNN_BLOCK_W), lambda c, i: (0, c, 0, 0))
    return pl.pallas_call(
        _rnn_core_kernel,
        grid=(RNN_BLOCKS, n),
        in_specs=[cur(fwd), prv(fwd), nxt(fwd), cur(bwd), prv(bwd), nxt(bwd),
                  chan(CONV_W), chan(1), gate_w, chan(2), gate_w, chan(2), chan(2)],
        out_specs=[pl.BlockSpec((TT, RNN_BLOCK_W), lambda c, i: (i, c)),
                   pl.BlockSpec((TT, RNN_BLOCK_W), lambda c, i: (n - 1 - i, c))],
        out_shape=[jax.ShapeDtypeStruct((s, D_RNN), F32), jax.ShapeDtypeStruct((s, D_RNN), F32)],
        scratch_shapes=[
            pltpu.VMEM((2, TT + 2 * SUBLANES, RNN_BLOCK_W), F32),
            pltpu.VMEM((2, LANE_SLABS, SCAN_BLOCKS * SCAN_PITCH, LANES), F32),
            pltpu.VMEM((2, LANE_SLABS, SCAN_BLOCKS * SCAN_PITCH, LANES), F32),
            pltpu.VMEM((2, LANE_SLABS, SCAN_BLOCKS * SCAN_PITCH, LANES), F32),
            pltpu.VMEM((2, LANE_SLABS, SCAN_BLOCKS * SCAN_PITCH, LANES), F32),
            pltpu.VMEM((2, SUBLANES, RNN_BLOCK_W), F32),
        ],
        compiler_params=_params("arbitrary", "arbitrary"),
        name="rnn_core",
    )(xb, xb, xb, xb, xb, xb, conv_w, conv_b, w_a, b_a, w_i, b_i, lam)


def _rnn_out_kernel(x_ref, g_ref, hf_ref, hb_ref, wy_ref, wo_ref, o_ref):
    x = x_ref[...]
    hn = _rms_normed(x, g_ref[...]).astype(BF16)
    gate = jax.nn.gelu(jnp.dot(hn, wy_ref[...], preferred_element_type=F32), approximate=True)
    a = ((hf_ref[...] + hb_ref[...]) * gate).astype(BF16)
    o_ref[...] = x + jnp.dot(a, wo_ref[...], preferred_element_type=F32)


def _rnn_out_proj(x, g, hf, hb, w_gate_all, w_out_all, layer):
    s = x.shape[0]
    row = pl.BlockSpec((TM_OUT, D_MODEL), lambda i: (i, 0))
    return pl.pallas_call(
        _rnn_out_kernel,
        grid=(s // TM_OUT,),
        in_specs=[row,
                  pl.BlockSpec((1, D_MODEL), lambda i: (0, 0)),
                  row, row,
                  pl.BlockSpec((None, D_MODEL, D_RNN), lambda i: (layer, 0, 0),
                               pipeline_mode=pl.Buffered(1)),
                  pl.BlockSpec((None, D_RNN, D_MODEL), lambda i: (layer, 0, 0),
                               pipeline_mode=pl.Buffered(1))],
        out_specs=row,
        out_shape=jax.ShapeDtypeStruct((s, D_MODEL), F32),
        compiler_params=_params("arbitrary"),
        name="rnn_out_proj",
    )(x, g, hf, hb, w_gate_all, w_out_all)


def _ffn_kernel(x_ref, g_ref, wg_ref, wu_ref, wd_ref, o_ref, hn_ref):
    @pl.when(pl.program_id(1) == 0)
    def _():
        x = x_ref[...]
        hn_ref[...] = _rms_normed(x, g_ref[...]).astype(BF16)
        o_ref[...] = x

    hn = hn_ref[...]
    gate = jnp.dot(hn, wg_ref[...].astype(BF16), preferred_element_type=F32)
    up = jnp.dot(hn, wu_ref[...].astype(BF16), preferred_element_type=F32)
    act = (jax.nn.silu(gate) * up).astype(BF16)
    o_ref[...] += jnp.dot(act, wd_ref[...].astype(BF16), preferred_element_type=F32)


def _ffn(x, g, w_gate, w_up, w_down, layer):
    s = x.shape[0]
    return pl.pallas_call(
        _ffn_kernel,
        grid=(s // TM_FFN, D_FF // TF),
        in_specs=[
            pl.BlockSpec((TM_FFN, D_MODEL), lambda i, f: (i, 0)),
            pl.BlockSpec((1, D_MODEL), lambda i, f: (0, 0)),
            pl.BlockSpec((None, D_MODEL, TF), lambda i, f: (layer, 0, f)),
            pl.BlockSpec((None, D_MODEL, TF), lambda i, f: (layer, 0, f)),
            pl.BlockSpec((None, TF, D_MODEL), lambda i, f: (layer, f, 0)),
        ],
        out_specs=pl.BlockSpec((TM_FFN, D_MODEL), lambda i, f: (i, 0)),
        out_shape=jax.ShapeDtypeStruct((s, D_MODEL), F32),
        scratch_shapes=[pltpu.VMEM((TM_FFN, D_MODEL), BF16)],
        compiler_params=_params("arbitrary", "arbitrary"),
        name="ffn",
    )(x, g, w_gate, w_up, w_down)


def _rope_tables(seq_len):
    rows_n = seq_len // GRID_W
    freqs = ROPE_THETA ** (-jnp.arange(ROPE_FREQS, dtype=F32) / ROPE_FREQS)
    row_ang = jnp.arange(rows_n, dtype=F32)[:, None, None] * freqs
    col_ang = jnp.arange(GRID_W, dtype=F32)[None, :, None] * freqs
    shape = (rows_n, GRID_W, ROPE_FREQS)
    row_ang = jnp.broadcast_to(row_ang, shape).reshape(seq_len, ROPE_FREQS)
    col_ang = jnp.broadcast_to(col_ang, shape).reshape(seq_len, ROPE_FREQS)
    cr, sr, cc, sc = jnp.cos(row_ang), jnp.sin(row_ang), jnp.cos(col_ang), jnp.sin(col_ang)
    return (jnp.concatenate([cr, cr, cc, cc], axis=-1),
            jnp.concatenate([-sr, sr, -sc, sc], axis=-1))


def kernel(x, norm_mix, norm_ffn, attn_w_qkv, attn_q_gain, attn_k_gain, attn_w_o, rnn_w_in, rnn_conv_w, rnn_conv_b, rnn_w_a, rnn_b_a, rnn_w_i, rnn_b_i, rnn_lambda, rnn_w_out, ffn_w_gate, ffn_w_up, ffn_w_down):
    b, s, d = x.shape
    assert (b, d) == (1, D_MODEL) and s % max(TM, TQ, TK, TT) == 0
    cos_t, sin_t = _rope_tables(s)
    w_gate_rnn, w_out = rnn_w_in[:, :, D_RNN:].astype(BF16), rnn_w_out.astype(BF16)
    w_a, w_i = (0.5 * rnn_w_a).astype(BF16), (0.5 * rnn_w_i).astype(BF16)
    h = x.reshape(s, d)
    for i in range(DEPTH):
        j = i // N_MIXERS
        g_mix = norm_mix[i].reshape(1, d)
        if i % N_MIXERS == 0:
            qkv = _qkv_proj(h, g_mix, attn_w_qkv, j,
                            attn_q_gain[j].reshape(1, HEAD_DIM), attn_k_gain[j].reshape(1, HEAD_DIM),
                            cos_t, sin_t)
            o = _attention(qkv)
            h = _proj_residual(o, attn_w_o, j, h)
        else:
            xb = _rnn_in_proj(h, g_mix, rnn_w_in, j)
            hf, hb = _rnn_core(xb, rnn_conv_w[j], rnn_conv_b[j].reshape(1, D_RNN),
                               w_a[j], rnn_b_a[j], w_i[j], rnn_b_i[j], rnn_lambda[j])
            h = _rnn_out_proj(h, g_mix, hf, hb, w_gate_rnn, w_out, j)
        h = _ffn(h, norm_ffn[i].reshape(1, d), ffn_w_gate, ffn_w_up, ffn_w_down, i)
    return h.reshape(b, s, d)
```

```python
import jax
import jax.numpy as jnp
from jax import lax
from jax.experimental import pallas as pl
from jax.experimental.pallas import tpu as pltpu

F32 = jnp.float32
BF16 = jnp.bfloat16

D_MODEL = 2048
DEPTH = 4
N_MIXERS = 2
GRID_W = 64
ROPE_THETA = 10000.0
HEAD_DIM = 128
N_Q_HEADS = 16
N_KV_HEADS = 4
GQA_GROUP = N_Q_HEADS // N_KV_HEADS
ROPE_FREQS = HEAD_DIM // 4
QKV_DIM = (N_Q_HEADS + 2 * N_KV_HEADS) * HEAD_DIM
D_RNN = D_MODEL
RNN_BLOCK_W = 256
RNN_BLOCKS = D_RNN // RNN_BLOCK_W
CONV_W = 4
CONV_LEFT = 2
LRU_C = 8.0
D_FF = 5632
EPS = 1e-6
Q_SCALE = HEAD_DIM ** -0.5 * 1.4426950408889634

SUBLANES = 8
BF16_SUBLANES = 16
LANES = 128
MXU_COLS = 256
VMEM_LIMIT_BYTES = 56 * 1024 * 1024

TM = 512
TM_OUT = 256
HEADS_PER_SLAB = MXU_COLS // HEAD_DIM
QKV_SLABS = QKV_DIM // MXU_COLS
TM_FFN = 1024
TF = 256
TQ = 512
TK = 1024
TT = 512
SCAN_BLOCKS = SUBLANES
SCAN_ROWS = TT // SCAN_BLOCKS
SCAN_PITCH = SCAN_ROWS + SUBLANES
LANE_SLABS = RNN_BLOCK_W // LANES
VT_ROWS = HEAD_DIM + BF16_SUBLANES
EXP_ROWS = 32

def _params(*sem):
    return pltpu.CompilerParams(dimension_semantics=sem, vmem_limit_bytes=VMEM_LIMIT_BYTES)


def _rms_normed(x, g):
    ms = jnp.mean(x * x, axis=-1, keepdims=True)
    return x * lax.rsqrt(ms + EPS) * g


def _qkv_kernel(x_ref, g_ref, w_ref, qg_ref, kg_ref, cos_ref, sin_ref, o_ref, hn_ref, ya_ref, yb_ref):
    hn_ref[...] = _rms_normed(x_ref[...], g_ref[...]).astype(BF16)
    q_gain = qg_ref[...] * Q_SCALE
    lane = lax.broadcasted_iota(jnp.int32, (TM, HEAD_DIM), 1)
    first_half = (lane & (2 * ROPE_FREQS - 1)) < ROPE_FREQS

    y_refs = (ya_ref, yb_ref)

    def matmul(s):
        w = w_ref[:, s * MXU_COLS:(s + 1) * MXU_COLS].astype(BF16)
        y_refs[s % 2][...] = jnp.dot(hn_ref[...], w, preferred_element_type=F32)

    def epilogue(s):
        for h in range(s * HEADS_PER_SLAB, (s + 1) * HEADS_PER_SLAB):
            lanes = slice((h % HEADS_PER_SLAB) * HEAD_DIM, (h % HEADS_PER_SLAB + 1) * HEAD_DIM)
            yh = y_refs[s % 2][:, lanes]
            if h >= N_Q_HEADS + N_KV_HEADS:
                out = yh
            else:
                gain = q_gain if h < N_Q_HEADS else kg_ref[...]
                yn = yh * lax.rsqrt(jnp.mean(yh * yh, axis=-1, keepdims=True) + EPS) * gain
                partner = jnp.where(first_half,
                                    pltpu.roll(yn, HEAD_DIM - ROPE_FREQS, 1),
                                    pltpu.roll(yn, ROPE_FREQS, 1))
                out = yn * cos_ref[...] + partner * sin_ref[...]
            o_ref[:, h * HEAD_DIM:(h + 1) * HEAD_DIM] = out.astype(BF16)

    matmul(0)
    for s in range(1, QKV_SLABS):
        epilogue(s - 1)
        matmul(s)
    epilogue(QKV_SLABS - 1)


def _qkv_proj(x, g, w_all, layer, q_gain, k_gain, cos_t, sin_t):
    s = x.shape[0]
    return pl.pallas_call(
        _qkv_kernel,
        grid=(s // TM,),
        in_specs=[
            pl.BlockSpec((TM, D_MODEL), lambda i: (i, 0)),
            pl.BlockSpec((1, D_MODEL), lambda i: (0, 0)),
            pl.BlockSpec((None, D_MODEL, QKV_DIM), lambda i: (layer, 0, 0),
                         pipeline_mode=pl.Buffered(1)),
            pl.BlockSpec((1, HEAD_DIM), lambda i: (0, 0)),
            pl.BlockSpec((1, HEAD_DIM), lambda i: (0, 0)),
            pl.BlockSpec((TM, HEAD_DIM), lambda i: (i, 0)),
            pl.BlockSpec((TM, HEAD_DIM), lambda i: (i, 0)),
        ],
        out_specs=pl.BlockSpec((TM, QKV_DIM), lambda i: (i, 0)),
        out_shape=jax.ShapeDtypeStruct((s, QKV_DIM), BF16),
        scratch_shapes=[pltpu.VMEM((TM, D_MODEL), BF16),
                        pltpu.VMEM((TM, MXU_COLS), F32),
                        pltpu.VMEM((TM, MXU_COLS), F32)],
        compiler_params=_params("arbitrary"),
        name="qkv_proj",
    )(x, g, w_all, q_gain, k_gain, cos_t, sin_t)


def _attn_kernel(q_ref, k_ref, v_ref, o_ref, vt_ref, acc_ref, sa_ref, sb_ref, pa_ref, pb_ref, state_ref):
    n_chunks = k_ref.shape[0] // TK
    i = pl.program_id(1)
    last_tile = pl.num_programs(1) - 1
    heads = range(GQA_GROUP)

    def score_stage(u, st_ref):
        k0 = (u % n_chunks) * TK
        k0 = k0 if isinstance(k0, int) else pl.multiple_of(k0, TK)
        q0 = pl.multiple_of(jnp.minimum(u // n_chunks, last_tile) * TQ, TQ)
        kc = k_ref[pl.ds(k0, TK), :]
        chunk_max = []
        for g in heads:
            q = q_ref[pl.ds(q0, TQ), g * HEAD_DIM:(g + 1) * HEAD_DIM]
            st = lax.dot_general(kc, q, (((1,), (1,)), ((), ())), preferred_element_type=F32)
            st_ref[g] = st
            chunk_max.append(jnp.max(st, axis=0, keepdims=True))
        return tuple(chunk_max)

    def exp_stage(u, st_ref, pt_ref, ms, chunk_max):
        first = u % n_chunks == 0
        new_ms, alphas = [], []
        for g in heads:
            m_old = jnp.where(first, -jnp.inf, ms[g])
            m_new = jnp.maximum(m_old, chunk_max[g])
            for r in range(0, TK, EXP_ROWS):
                pt_ref[g, r:r + EXP_ROWS, :] = jnp.exp2(st_ref[g, r:r + EXP_ROWS, :] - m_new).astype(BF16)
            new_ms.append(m_new)
            alphas.append(jnp.exp2(m_old - m_new))
        return tuple(new_ms), tuple(alphas)

    def out_stage(c, pt_ref, alphas):
        for g in heads:
            acc_ref[g] = alphas[g] * acc_ref[g] + jnp.dot(vt_ref[c], pt_ref[g], preferred_element_type=F32)

    def save_state(ms, alphas, chunk_max):
        for k, rows in enumerate((ms, alphas, chunk_max)):
            for g in heads:
                state_ref[k, g] = rows[g]

    @pl.when(i == 0)
    def _():
        for c in range(n_chunks):
            vc = v_ref[c * TK:(c + 1) * TK, :].astype(F32)
            vt_ref[c, :HEAD_DIM, :] = vc.T.astype(BF16)
            vt_ref[c, HEAD_DIM:, :] = jnp.ones((VT_ROWS - HEAD_DIM, TK), BF16)
        acc_ref[...] = jnp.zeros_like(acc_ref)
        ms = (jnp.full((1, TQ), -jnp.inf, F32),) * GQA_GROUP
        chunk_max = score_stage(0, sa_ref)
        ms, alphas = exp_stage(0, sa_ref, pa_ref, ms, chunk_max)
        save_state(ms, alphas, score_stage(1, sb_ref))

    def trip(c, carry, s_cur, p_cur, s_nxt, p_nxt):
        u = i * n_chunks + c
        ms, alphas, chunk_max = carry
        out_stage(c, p_cur, alphas)
        ms, alphas = exp_stage(u + 1, s_nxt, p_nxt, ms, chunk_max)
        chunk_max = score_stage(u + 2, s_cur)
        return ms, alphas, chunk_max

    def step(c, carry):
        return lax.cond(c % 2 == 0,
                        lambda cr: trip(c, cr, sa_ref, pa_ref, sb_ref, pb_ref),
                        lambda cr: trip(c, cr, sb_ref, pb_ref, sa_ref, pa_ref),
                        carry)

    carry = tuple(tuple(state_ref[k, g] for g in heads) for k in range(3))
    save_state(*lax.fori_loop(0, n_chunks, step, carry))
    for g in heads:
        acc = acc_ref[g]
        out_t = acc[:HEAD_DIM, :] / acc[HEAD_DIM:HEAD_DIM + 1, :]
        o_ref[:, g * HEAD_DIM:(g + 1) * HEAD_DIM] = out_t.T.astype(BF16)


def _attention(qkv):
    s = qkv.shape[0]
    assert (s // TK) % 2 == 0, "the a / b buffer parity of the attention pipeline needs an even chunk count"
    group_w = GQA_GROUP * HEAD_DIM
    return pl.pallas_call(
        _attn_kernel,
        grid=(N_KV_HEADS, s // TQ),
        in_specs=[
            pl.BlockSpec((s, group_w), lambda h, i: (0, h), pipeline_mode=pl.Buffered(1)),
            pl.BlockSpec((s, HEAD_DIM), lambda h, i: (0, N_Q_HEADS + h)),
            pl.BlockSpec((s, HEAD_DIM), lambda h, i: (0, N_Q_HEADS + N_KV_HEADS + h)),
        ],
        out_specs=pl.BlockSpec((TQ, group_w), lambda h, i: (i, h)),
        out_shape=jax.ShapeDtypeStruct((s, N_Q_HEADS * HEAD_DIM), BF16),
        scratch_shapes=[pltpu.VMEM((s // TK, VT_ROWS, TK), BF16),
                        pltpu.VMEM((GQA_GROUP, VT_ROWS, TQ), F32),
                        pltpu.VMEM((GQA_GROUP, TK, TQ), F32),
                        pltpu.VMEM((GQA_GROUP, TK, TQ), F32),
                        pltpu.VMEM((GQA_GROUP, TK, TQ), BF16),
                        pltpu.VMEM((GQA_GROUP, TK, TQ), BF16),
                        pltpu.VMEM((3, GQA_GROUP, 1, TQ), F32)],
        compiler_params=_params("arbitrary", "arbitrary"),
        name="attention",
    )(qkv, qkv, qkv)


def _proj_res_kernel(a_ref, w_ref, x_ref, o_ref, w16_ref):
    @pl.when(pl.program_id(0) == 0)
    def _():
        w16_ref[...] = w_ref[...].astype(BF16)

    o_ref[...] = x_ref[...] + jnp.dot(a_ref[...], w16_ref[...], preferred_element_type=F32)


def _proj_residual(a, w_all, layer, x):
    s, k = a.shape
    return pl.pallas_call(
        _proj_res_kernel,
        grid=(s // TM,),
        in_specs=[
            pl.BlockSpec((TM, k), lambda i: (i, 0)),
            pl.BlockSpec((None, k, D_MODEL), lambda i: (layer, 0, 0), pipeline_mode=pl.Buffered(1)),
            pl.BlockSpec((TM, D_MODEL), lambda i: (i, 0)),
        ],
        out_specs=pl.BlockSpec((TM, D_MODEL), lambda i: (i, 0)),
        out_shape=jax.ShapeDtypeStruct((s, D_MODEL), F32),
        scratch_shapes=[pltpu.VMEM((k, D_MODEL), BF16)],
        compiler_params=_params("arbitrary"),
        name="proj_residual",
    )(a, w_all, x)


def _rnn_in_kernel(x_ref, g_ref, w_ref, xb_ref, w16_ref):
    @pl.when(pl.program_id(0) == 0)
    def _():
        w16_ref[...] = w_ref[...].astype(BF16)

    hn = _rms_normed(x_ref[...], g_ref[...]).astype(BF16)
    xb_ref[...] = jnp.dot(hn, w16_ref[...], preferred_element_type=F32)


def _rnn_in_proj(x, g, w_all, layer):
    s = x.shape[0]
    return pl.pallas_call(
        _rnn_in_kernel,
        grid=(s // TM,),
        in_specs=[
            pl.BlockSpec((TM, D_MODEL), lambda i: (i, 0)),
            pl.BlockSpec((1, D_MODEL), lambda i: (0, 0)),
            pl.BlockSpec((None, D_MODEL, D_RNN), lambda i: (layer, 0, 0), pipeline_mode=pl.Buffered(1)),
        ],
        out_specs=pl.BlockSpec((TM, D_RNN), lambda i: (i, 0)),
        out_shape=jax.ShapeDtypeStruct((s, D_RNN), F32),
        scratch_shapes=[pltpu.VMEM((D_MODEL, D_RNN), BF16)],
        compiler_params=_params("arbitrary"),
        name="rnn_in_proj",
    )(x, g, w_all)


def _softplus(z):
    return jnp.maximum(z, 0.0) + jnp.log1p(jnp.exp(-jnp.abs(z)))


def _lru_inputs(d, cur_ref, prev_ref, next_ref, first, last, cw_ref, cb_ref, wa_ref, ba_ref, wi_ref,
                bi_ref, lam_ref, ext_ref, a_ref, u_ref):
    ext_ref[d, 0:SUBLANES, :] = jnp.where(first, 0.0, prev_ref[...])
    ext_ref[d, SUBLANES:SUBLANES + TT, :] = cur_ref[...]
    ext_ref[d, SUBLANES + TT:2 * SUBLANES + TT, :] = jnp.where(last, 0.0, next_ref[...])
    ext = ext_ref[d]
    xc = cb_ref[...]
    for k in range(CONV_W):
        shift = (CONV_LEFT - k) % ext.shape[0]
        tap = ext if shift == 0 else pltpu.roll(ext, shift, 0)
        xc = xc + cw_ref[k:k + 1, :] * tap[SUBLANES:SUBLANES + TT, :]
    xb16 = xc.astype(BF16)
    ta = jnp.tanh(jnp.dot(xb16, wa_ref[d, 0], preferred_element_type=F32) + 0.5 * ba_ref[d:d + 1, :])
    ti = jnp.tanh(jnp.dot(xb16, wi_ref[d, 0], preferred_element_type=F32) + 0.5 * bi_ref[d:d + 1, :])
    i = 0.5 * ti + 0.5
    half_c = (-0.5 * LRU_C) * _softplus(-lam_ref[d:d + 1, :])
    log_a = ta * half_c + half_c
    a = jnp.exp(log_a)
    one_minus_a2 = -jnp.tanh(log_a) * (a * a + 1.0)
    root = jnp.where(one_minus_a2 > 0.0, one_minus_a2 * lax.rsqrt(one_minus_a2), 0.0)
    u = root * (i * xc)
    for s in range(SCAN_BLOCKS):
        for l in range(LANE_SLABS):
            rows = slice(s * SCAN_ROWS, (s + 1) * SCAN_ROWS)
            lanes = slice(l * LANES, (l + 1) * LANES)
            a_ref[d, l, s * SCAN_PITCH:s * SCAN_PITCH + SCAN_ROWS, :] = a[rows, lanes]
            u_ref[d, l, s * SCAN_PITCH:s * SCAN_PITCH + SCAN_ROWS, :] = u[rows, lanes]


def _rnn_core_kernel(fc_ref, fp_ref, fn_ref, bc_ref, bp_ref, bn_ref, cw_ref, cb_ref, wa_ref, ba_ref,
                     wi_ref, bi_ref, lam_ref, hf_ref, hb_ref, ext_ref, a_ref, u_ref, hl_ref, p_ref,
                     carry_ref):
    i = pl.program_id(1)
    n = pl.num_programs(1)

    @pl.when(i == 0)
    def _():
        carry_ref[...] = jnp.zeros_like(carry_ref)

    shared = (cw_ref, cb_ref, wa_ref, ba_ref, wi_ref, bi_ref, lam_ref, ext_ref, a_ref, u_ref)
    _lru_inputs(0, fc_ref, fp_ref, fn_ref, i == 0, i == n - 1, *shared)
    _lru_inputs(1, bc_ref, bp_ref, bn_ref, i == n - 1, i == 0, *shared)

    pairs = [(d, l) for d in range(2) for l in range(LANE_SLABS)]

    def body(jj, carry):
        out = []
        for (d, l), (h, p) in zip(pairs, carry):
            j = jj if d == 0 else SCAN_ROWS - 1 - jj
            rows = pl.ds(j, SCAN_BLOCKS, stride=SCAN_PITCH)
            a = a_ref[d, l, rows, :]
            h = a * h + u_ref[d, l, rows, :]
            p = a * p
            hl_ref[d, l, rows, :] = h
            p_ref[d, l, rows, :] = p
            out.append((h, p))
        return tuple(out)

    zero = jnp.zeros((SCAN_BLOCKS, LANES), F32)
    ends = ((zero, zero + 1.0),) * len(pairs)
    for jj in range(SCAN_ROWS):
        ends = body(jj, ends)

    for (d, l), (h_end, p_end) in zip(pairs, ends):
        lanes = slice(l * LANES, (l + 1) * LANES)
        out_ref = hf_ref if d == 0 else hb_ref
        c = carry_ref[d, 0:1, lanes]
        for s in (range(SCAN_BLOCKS) if d == 0 else reversed(range(SCAN_BLOCKS))):
            blk = slice(s * SCAN_PITCH, s * SCAN_PITCH + SCAN_ROWS)
            out_ref[s * SCAN_ROWS:(s + 1) * SCAN_ROWS, lanes] = hl_ref[d, l, blk, :] + p_ref[d, l, blk, :] * c
            c = h_end[s:s + 1, :] + p_end[s:s + 1, :] * c
        carry_ref[d, :, lanes] = jnp.broadcast_to(c, (SUBLANES, LANES))


def _rnn_core(xb, conv_w, conv_b, w_a, b_a, w_i, b_i, lam):
    s = xb.shape[0]
    n = s // TT
    halo = TT // SUBLANES
    last_halo = s // SUBLANES - 1
    cur = lambda t: pl.BlockSpec((TT, RNN_BLOCK_W), lambda c, i: (t(i, n), c))
    prv = lambda t: pl.BlockSpec((SUBLANES, RNN_BLOCK_W),
                                 lambda c, i: (jnp.maximum(t(i, n) * halo - 1, 0), c))
    nxt = lambda t: pl.BlockSpec((SUBLANES, RNN_BLOCK_W),
                                 lambda c, i: (jnp.minimum((t(i, n) + 1) * halo, last_halo), c))
    fwd = lambda i, n: i
    bwd = lambda i, n: n - 1 - i
    chan = lambda rows: pl.BlockSpec((rows, RNN_BLOCK_W), lambda c, i: (0, c))
    gate_w = pl.BlockSpec((2, 1, RNN_BLOCK_W, RNN_BLOCK_W), lambda c, i: (0, c, 0, 0))
    return pl.pallas_call(
        _rnn_core_kernel,
        grid=(RNN_BLOCKS, n),
        in_specs=[cur(fwd), prv(fwd), nxt(fwd), cur(bwd), prv(bwd), nxt(bwd),
                  chan(CONV_W), chan(1), gate_w, chan(2), gate_w, chan(2), chan(2)],
        out_specs=[pl.BlockSpec((TT, RNN_BLOCK_W), lambda c, i: (i, c)),
                   pl.BlockSpec((TT, RNN_BLOCK_W), lambda c, i: (n - 1 - i, c))],
        out_shape=[jax.ShapeDtypeStruct((s, D_RNN), F32), jax.ShapeDtypeStruct((s, D_RNN), F32)],
        scratch_shapes=[
            pltpu.VMEM((2, TT + 2 * SUBLANES, RNN_BLOCK_W), F32),
            pltpu.VMEM((2, LANE_SLABS, SCAN_BLOCKS * SCAN_PITCH, LANES), F32),
            pltpu.VMEM((2, LANE_SLABS, SCAN_BLOCKS * SCAN_PITCH, LANES), F32),
            pltpu.VMEM((2, LANE_SLABS, SCAN_BLOCKS * SCAN_PITCH, LANES), F32),
            pltpu.VMEM((2, LANE_SLABS, SCAN_BLOCKS * SCAN_PITCH, LANES), F32),
            pltpu.VMEM((2, SUBLANES, RNN_BLOCK_W), F32),
        ],
        compiler_params=_params("arbitrary", "arbitrary"),
        name="rnn_core",
    )(xb, xb, xb, xb, xb, xb, conv_w, conv_b, w_a, b_a, w_i, b_i, lam)


def _rnn_out_kernel(x_ref, g_ref, hf_ref, hb_ref, wy_ref, wo_ref, o_ref):
    x = x_ref[...]
    hn = _rms_normed(x, g_ref[...]).astype(BF16)
    gate = jax.nn.gelu(jnp.dot(hn, wy_ref[...], preferred_element_type=F32), approximate=True)
    a = ((hf_ref[...] + hb_ref[...]) * gate).astype(BF16)
    o_ref[...] = x + jnp.dot(a, wo_ref[...], preferred_element_type=F32)


def _rnn_out_proj(x, g, hf, hb, w_gate_all, w_out_all, layer):
    s = x.shape[0]
    row = pl.BlockSpec((TM_OUT, D_MODEL), lambda i: (i, 0))
    return pl.pallas_call(
        _rnn_out_kernel,
        grid=(s // TM_OUT,),
        in_specs=[row,
                  pl.BlockSpec((1, D_MODEL), lambda i: (0, 0)),
                  row, row,
                  pl.BlockSpec((None, D_MODEL, D_RNN), lambda i: (layer, 0, 0),
                               pipeline_mode=pl.Buffered(1)),
                  pl.BlockSpec((None, D_RNN, D_MODEL), lambda i: (layer, 0, 0),
                               pipeline_mode=pl.Buffered(1))],
        out_specs=row,
        out_shape=jax.ShapeDtypeStruct((s, D_MODEL), F32),
        compiler_params=_params("arbitrary"),
        name="rnn_out_proj",
    )(x, g, hf, hb, w_gate_all, w_out_all)


def _ffn_kernel(x_ref, g_ref, wg_ref, wu_ref, wd_ref, o_ref, hn_ref):
    @pl.when(pl.program_id(1) == 0)
    def _():
        x = x_ref[...]
        hn_ref[...] = _rms_normed(x, g_ref[...]).astype(BF16)
        o_ref[...] = x

    hn = hn_ref[...]
    gate = jnp.dot(hn, wg_ref[...].astype(BF16), preferred_element_type=F32)
    up = jnp.dot(hn, wu_ref[...].astype(BF16), preferred_element_type=F32)
    act = (jax.nn.silu(gate) * up).astype(BF16)
    o_ref[...] += jnp.dot(act, wd_ref[...].astype(BF16), preferred_element_type=F32)


def _ffn(x, g, w_gate, w_up, w_down, layer):
    s = x.shape[0]
    return pl.pallas_call(
        _ffn_kernel,
        grid=(s // TM_FFN, D_FF // TF),
        in_specs=[
            pl.BlockSpec((TM_FFN, D_MODEL), lambda i, f: (i, 0)),
            pl.BlockSpec((1, D_MODEL), lambda i, f: (0, 0)),
            pl.BlockSpec((None, D_MODEL, TF), lambda i, f: (layer, 0, f)),
            pl.BlockSpec((None, D_MODEL, TF), lambda i, f: (layer, 0, f)),
            pl.BlockSpec((None, TF, D_MODEL), lambda i, f: (layer, f, 0)),
        ],
        out_specs=pl.BlockSpec((TM_FFN, D_MODEL), lambda i, f: (i, 0)),
        out_shape=jax.ShapeDtypeStruct((s, D_MODEL), F32),
        scratch_shapes=[pltpu.VMEM((TM_FFN, D_MODEL), BF16)],
        compiler_params=_params("arbitrary", "arbitrary"),
        name="ffn",
    )(x, g, w_gate, w_up, w_down)


def _rope_tables(seq_len):
    rows_n = seq_len // GRID_W
    freqs = ROPE_THETA ** (-jnp.arange(ROPE_FREQS, dtype=F32) / ROPE_FREQS)
    row_ang = jnp.arange(rows_n, dtype=F32)[:, None, None] * freqs
    col_ang = jnp.arange(GRID_W, dtype=F32)[None, :, None] * freqs
    shape = (rows_n, GRID_W, ROPE_FREQS)
    row_ang = jnp.broadcast_to(row_ang, shape).reshape(seq_len, ROPE_FREQS)
    col_ang = jnp.broadcast_to(col_ang, shape).reshape(seq_len, ROPE_FREQS)
    cr, sr, cc, sc = jnp.cos(row_ang), jnp.sin(row_ang), jnp.cos(col_ang), jnp.sin(col_ang)
    return (jnp.concatenate([cr, cr, cc, cc], axis=-1),
            jnp.concatenate([-sr, sr, -sc, sc], axis=-1))


def kernel(x, norm_mix, norm_ffn, attn_w_qkv, attn_q_gain, attn_k_gain, attn_w_o, rnn_w_in, rnn_conv_w, rnn_conv_b, rnn_w_a, rnn_b_a, rnn_w_i, rnn_b_i, rnn_lambda, rnn_w_out, ffn_w_gate, ffn_w_up, ffn_w_down):
    b, s, d = x.shape
    assert (b, d) == (1, D_MODEL) and s % max(TM, TQ, TK, TT) == 0
    cos_t, sin_t = _rope_tables(s)
    w_gate_rnn, w_out = rnn_w_in[:, :, D_RNN:].astype(BF16), rnn_w_out.astype(BF16)
    w_a, w_i = (0.5 * rnn_w_a).astype(BF16), (0.5 * rnn_w_i).astype(BF16)
    h = x.reshape(s, d)
    for i in range(DEPTH):
        j = i // N_MIXERS
        g_mix = norm_mix[i].reshape(1, d)
        if i % N_MIXERS == 0:
            qkv = _qkv_proj(h, g_mix, attn_w_qkv, j,
                            attn_q_gain[j].reshape(1, HEAD_DIM), attn_k_gain[j].reshape(1, HEAD_DIM),
                            cos_t, sin_t)
            o = _attention(qkv)
            h = _proj_residual(o, attn_w_o, j, h)
        else:
            xb = _rnn_in_proj(h, g_mix, rnn_w_in, j)
            hf, hb = _rnn_core(xb, rnn_conv_w[j], rnn_conv_b[j].reshape(1, D_RNN),
                               w_a[j], rnn_b_a[j], w_i[j], rnn_b_i[j], rnn_lambda[j])
            h = _rnn_out_proj(h, g_mix, hf, hb, w_gate_rnn, w_out, j)
        h = _ffn(h, norm_ffn[i].reshape(1, d), ffn_w_gate, ffn_w_up, ffn_w_down, i)
    return h.reshape(b, s, d)
```

```python
import jax
import jax.numpy as jnp
from jax import lax
from jax.experimental import pallas as pl
from jax.experimental.pallas import tpu as pltpu

F32 = jnp.float32
BF16 = jnp.bfloat16

D_MODEL = 2048
DEPTH = 4
N_MIXERS = 2
GRID_W = 64
ROPE_THETA = 10000.0
HEAD_DIM = 128
N_Q_HEADS = 16
N_KV_HEADS = 4
GQA_GROUP = N_Q_HEADS // N_KV_HEADS
ROPE_FREQS = HEAD_DIM // 4
QKV_DIM = (N_Q_HEADS + 2 * N_KV_HEADS) * HEAD_DIM
D_RNN = D_MODEL
RNN_BLOCK_W = 256
RNN_BLOCKS = D_RNN // RNN_BLOCK_W
CONV_W = 4
CONV_LEFT = 2
LRU_C = 8.0
D_FF = 5632
EPS = 1e-6
Q_SCALE = HEAD_DIM ** -0.5 * 1.4426950408889634

SUBLANES = 8
BF16_SUBLANES = 16
LANES = 128
MXU_COLS = 256
VMEM_LIMIT_BYTES = 56 * 1024 * 1024

TM = 512
TM_OUT = 256
HEADS_PER_SLAB = MXU_COLS // HEAD_DIM
QKV_SLABS = QKV_DIM // MXU_COLS
TM_FFN = 1024
TF = 256
TQ = 512
TK = 1024
TT = 512
SCAN_BLOCKS = SUBLANES
SCAN_ROWS = TT // SCAN_BLOCKS
SCAN_PITCH = SCAN_ROWS + SUBLANES
LANE_SLABS = RNN_BLOCK_W // LANES
VT_ROWS = HEAD_DIM + BF16_SUBLANES
EXP_ROWS = 32

def _params(*sem):
    return pltpu.CompilerParams(dimension_semantics=sem, vmem_limit_bytes=VMEM_LIMIT_BYTES)


def _rms_normed(x, g):
    ms = jnp.mean(x * x, axis=-1, keepdims=True)
    return x * lax.rsqrt(ms + EPS) * g


def _qkv_kernel(x_ref, g_ref, w_ref, qg_ref, kg_ref, cos_ref, sin_ref, o_ref, hn_ref, ya_ref, yb_ref):
    hn_ref[...] = _rms_normed(x_ref[...], g_ref[...]).astype(BF16)
    q_gain = qg_ref[...] * Q_SCALE
    lane = lax.broadcasted_iota(jnp.int32, (TM, HEAD_DIM), 1)
    first_half = (lane & (2 * ROPE_FREQS - 1)) < ROPE_FREQS

    y_refs = (ya_ref, yb_ref)

    def matmul(s):
        w = w_ref[:, s * MXU_COLS:(s + 1) * MXU_COLS].astype(BF16)
        y_refs[s % 2][...] = jnp.dot(hn_ref[...], w, preferred_element_type=F32)

    def epilogue(s):
        for h in range(s * HEADS_PER_SLAB, (s + 1) * HEADS_PER_SLAB):
            lanes = slice((h % HEADS_PER_SLAB) * HEAD_DIM, (h % HEADS_PER_SLAB + 1) * HEAD_DIM)
            yh = y_refs[s % 2][:, lanes]
            if h >= N_Q_HEADS + N_KV_HEADS:
                out = yh
            else:
                gain = q_gain if h < N_Q_HEADS else kg_ref[...]
                yn = yh * lax.rsqrt(jnp.mean(yh * yh, axis=-1, keepdims=True) + EPS) * gain
                partner = jnp.where(first_half,
                                    pltpu.roll(yn, HEAD_DIM - ROPE_FREQS, 1),
                                    pltpu.roll(yn, ROPE_FREQS, 1))
                out = yn * cos_ref[...] + partner * sin_ref[...]
            o_ref[:, h * HEAD_DIM:(h + 1) * HEAD_DIM] = out.astype(BF16)

    matmul(0)
    for s in range(1, QKV_SLABS):
        epilogue(s - 1)
        matmul(s)
    epilogue(QKV_SLABS - 1)


def _qkv_proj(x, g, w_all, layer, q_gain, k_gain, cos_t, sin_t):
    s = x.shape[0]
    return pl.pallas_call(
        _qkv_kernel,
        grid=(s // TM,),
        in_specs=[
            pl.BlockSpec((TM, D_MODEL), lambda i: (i, 0)),
            pl.BlockSpec((1, D_MODEL), lambda i: (0, 0)),
            pl.BlockSpec((None, D_MODEL, QKV_DIM), lambda i: (layer, 0, 0),
                         pipeline_mode=pl.Buffered(1)),
            pl.BlockSpec((1, HEAD_DIM), lambda i: (0, 0)),
            pl.BlockSpec((1, HEAD_DIM), lambda i: (0, 0)),
            pl.BlockSpec((TM, HEAD_DIM), lambda i: (i, 0)),
            pl.BlockSpec((TM, HEAD_DIM), lambda i: (i, 0)),
        ],
        out_specs=pl.BlockSpec((TM, QKV_DIM), lambda i: (i, 0)),
        out_shape=jax.ShapeDtypeStruct((s, QKV_DIM), BF16),
        scratch_shapes=[pltpu.VMEM((TM, D_MODEL), BF16),
                        pltpu.VMEM((TM, MXU_COLS), F32),
                        pltpu.VMEM((TM, MXU_COLS), F32)],
        compiler_params=_params("arbitrary"),
        name="qkv_proj",
    )(x, g, w_all, q_gain, k_gain, cos_t, sin_t)


def _attn_kernel(q_ref, k_ref, v_ref, o_ref, vt_ref, acc_ref, sa_ref, sb_ref, pa_ref, pb_ref, state_ref):
    n_chunks = k_ref.shape[0] // TK
    i = pl.program_id(1)
    last_tile = pl.num_programs(1) - 1
    heads = range(GQA_GROUP)

    def score_stage(u, st_ref):
        k0 = (u % n_chunks) * TK
        k0 = k0 if isinstance(k0, int) else pl.multiple_of(k0, TK)
        q0 = pl.multiple_of(jnp.minimum(u // n_chunks, last_tile) * TQ, TQ)
        kc = k_ref[pl.ds(k0, TK), :]
        chunk_max = []
        for g in heads:
            q = q_ref[pl.ds(q0, TQ), g * HEAD_DIM:(g + 1) * HEAD_DIM]
            st = lax.dot_general(kc, q, (((1,), (1,)), ((), ())), preferred_element_type=F32)
            st_ref[g] = st
            chunk_max.append(jnp.max(st, axis=0, keepdims=True))
        return tuple(chunk_max)

    def exp_stage(u, st_ref, pt_ref, ms, chunk_max):
        first = u % n_chunks == 0
        new_ms, alphas = [], []
        for g in heads:
            m_old = jnp.where(first, -jnp.inf, ms[g])
            m_new = jnp.maximum(m_old, chunk_max[g])
            for r in range(0, TK, EXP_ROWS):
                pt_ref[g, r:r + EXP_ROWS, :] = jnp.exp2(st_ref[g, r:r + EXP_ROWS, :] - m_new).astype(BF16)
            new_ms.append(m_new)
            alphas.append(jnp.exp2(m_old - m_new))
        return tuple(new_ms), tuple(alphas)

    def out_stage(c, pt_ref, alphas):
        for g in heads:
            acc_ref[g] = alphas[g] * acc_ref[g] + jnp.dot(vt_ref[c], pt_ref[g], preferred_element_type=F32)

    def save_state(ms, alphas, chunk_max):
        for k, rows in enumerate((ms, alphas, chunk_max)):
            for g in heads:
                state_ref[k, g] = rows[g]

    @pl.when(i == 0)
    def _():
        for c in range(n_chunks):
            vc = v_ref[c * TK:(c + 1) * TK, :].astype(F32)
            vt_ref[c, :HEAD_DIM, :] = vc.T.astype(BF16)
            vt_ref[c, HEAD_DIM:, :] = jnp.ones((VT_ROWS - HEAD_DIM, TK), BF16)
        acc_ref[...] = jnp.zeros_like(acc_ref)
        ms = (jnp.full((1, TQ), -jnp.inf, F32),) * GQA_GROUP
        chunk_max = score_stage(0, sa_ref)
        ms, alphas = exp_stage(0, sa_ref, pa_ref, ms, chunk_max)
        save_state(ms, alphas, score_stage(1, sb_ref))

    def trip(c, carry, s_cur, p_cur, s_nxt, p_nxt):
        u = i * n_chunks + c
        ms, alphas, chunk_max = carry
        out_stage(c, p_cur, alphas)
        ms, alphas = exp_stage(u + 1, s_nxt, p_nxt, ms, chunk_max)
        chunk_max = score_stage(u + 2, s_cur)
        return ms, alphas, chunk_max

    def step(c, carry):
        return lax.cond(c % 2 == 0,
                        lambda cr: trip(c, cr, sa_ref, pa_ref, sb_ref, pb_ref),
                        lambda cr: trip(c, cr, sb_ref, pb_ref, sa_ref, pa_ref),
                        carry)

    carry = tuple(tuple(state_ref[k, g] for g in heads) for k in range(3))
    save_state(*lax.fori_loop(0, n_chunks, step, carry))
    for g in heads:
        acc = acc_ref[g]
        out_t = acc[:HEAD_DIM, :] / acc[HEAD_DIM:HEAD_DIM + 1, :]
        o_ref[:, g * HEAD_DIM:(g + 1) * HEAD_DIM] = out_t.T.astype(BF16)


def _attention(qkv):
    s = qkv.shape[0]
    assert (s // TK) % 2 == 0, "the a / b buffer parity of the attention pipeline needs an even chunk count"
    group_w = GQA_GROUP * HEAD_DIM
    return pl.pallas_call(
        _attn_kernel,
        grid=(N_KV_HEADS, s // TQ),
        in_specs=[
            pl.BlockSpec((s, group_w), lambda h, i: (0, h), pipeline_mode=pl.Buffered(1)),
            pl.BlockSpec((s, HEAD_DIM), lambda h, i: (0, N_Q_HEADS + h)),
            pl.BlockSpec((s, HEAD_DIM), lambda h, i: (0, N_Q_HEADS + N_KV_HEADS + h)),
        ],
        out_specs=pl.BlockSpec((TQ, group_w), lambda h, i: (i, h)),
        out_shape=jax.ShapeDtypeStruct((s, N_Q_HEADS * HEAD_DIM), BF16),
        scratch_shapes=[pltpu.VMEM((s // TK, VT_ROWS, TK), BF16),
                        pltpu.VMEM((GQA_GROUP, VT_ROWS, TQ), F32),
                        pltpu.VMEM((GQA_GROUP, TK, TQ), F32),
                        pltpu.VMEM((GQA_GROUP, TK, TQ), F32),
                        pltpu.VMEM((GQA_GROUP, TK, TQ), BF16),
                        pltpu.VMEM((GQA_GROUP, TK, TQ), BF16),
                        pltpu.VMEM((3, GQA_GROUP, 1, TQ), F32)],
        compiler_params=_params("arbitrary", "arbitrary"),
        name="attention",
    )(qkv, qkv, qkv)


def _proj_res_kernel(a_ref, w_ref, x_ref, o_ref, w16_ref):
    @pl.when(pl.program_id(0) == 0)
    def _():
        w16_ref[...] = w_ref[...].astype(BF16)

    o_ref[...] = x_ref[...] + jnp.dot(a_ref[...], w16_ref[...], preferred_element_type=F32)


def _proj_residual(a, w_all, layer, x):
    s, k = a.shape
    return pl.pallas_call(
        _proj_res_kernel,
        grid=(s // TM,),
        in_specs=[
            pl.BlockSpec((TM, k), lambda i: (i, 0)),
            pl.BlockSpec((None, k, D_MODEL), lambda i: (layer, 0, 0), pipeline_mode=pl.Buffered(1)),
            pl.BlockSpec((TM, D_MODEL), lambda i: (i, 0)),
        ],
        out_specs=pl.BlockSpec((TM, D_MODEL), lambda i: (i, 0)),
        out_shape=jax.ShapeDtypeStruct((s, D_MODEL), F32),
        scratch_shapes=[pltpu.VMEM((k, D_MODEL), BF16)],
        compiler_params=_params("arbitrary"),
        name="proj_residual",
    )(a, w_all, x)


def _rnn_in_kernel(x_ref, g_ref, w_ref, xb_ref, w16_ref):
    @pl.when(pl.program_id(0) == 0)
    def _():
        w16_ref[...] = w_ref[...].astype(BF16)

    hn = _rms_normed(x_ref[...], g_ref[...]).astype(BF16)
    xb_ref[...] = jnp.dot(hn, w16_ref[...], preferred_element_type=F32)


def _rnn_in_proj(x, g, w_all, layer):
    s = x.shape[0]
    return pl.pallas_call(
        _rnn_in_kernel,
        grid=(s // TM,),
        in_specs=[
            pl.BlockSpec((TM, D_MODEL), lambda i: (i, 0)),
            pl.BlockSpec((1, D_MODEL), lambda i: (0, 0)),
            pl.BlockSpec((None, D_MODEL, D_RNN), lambda i: (layer, 0, 0), pipeline_mode=pl.Buffered(1)),
        ],
        out_specs=pl.BlockSpec((TM, D_RNN), lambda i: (i, 0)),
        out_shape=jax.ShapeDtypeStruct((s, D_RNN), F32),
        scratch_shapes=[pltpu.VMEM((D_MODEL, D_RNN), BF16)],
        compiler_params=_params("arbitrary"),
        name="rnn_in_proj",
    )(x, g, w_all)


def _softplus(z):
    return jnp.maximum(z, 0.0) + jnp.log1p(jnp.exp(-jnp.abs(z)))


def _lru_inputs(d, cur_ref, prev_ref, next_ref, first, last, cw_ref, cb_ref, wa_ref, ba_ref, wi_ref,
                bi_ref, lam_ref, ext_ref, a_ref, u_ref):
    ext_ref[d, 0:SUBLANES, :] = jnp.where(first, 0.0, prev_ref[...])
    ext_ref[d, SUBLANES:SUBLANES + TT, :] = cur_ref[...]
    ext_ref[d, SUBLANES + TT:2 * SUBLANES + TT, :] = jnp.where(last, 0.0, next_ref[...])
    ext = ext_ref[d]
    xc = cb_ref[...]
    for k in range(CONV_W):
        shift = (CONV_LEFT - k) % ext.shape[0]
        tap = ext if shift == 0 else pltpu.roll(ext, shift, 0)
        xc = xc + cw_ref[k:k + 1, :] * tap[SUBLANES:SUBLANES + TT, :]
    xb16 = xc.astype(BF16)
    ta = jnp.tanh(jnp.dot(xb16, wa_ref[d, 0], preferred_element_type=F32) + 0.5 * ba_ref[d:d + 1, :])
    ti = jnp.tanh(jnp.dot(xb16, wi_ref[d, 0], preferred_element_type=F32) + 0.5 * bi_ref[d:d + 1, :])
    i = 0.5 * ti + 0.5
    half_c = (-0.5 * LRU_C) * _softplus(-lam_ref[d:d + 1, :])
    log_a = ta * half_c + half_c
    a = jnp.exp(log_a)
    one_minus_a2 = -jnp.tanh(log_a) * (a * a + 1.0)
    root = jnp.where(one_minus_a2 > 0.0, one_minus_a2 * lax.rsqrt(one_minus_a2), 0.0)
    u = root * (i * xc)
    for s in range(SCAN_BLOCKS):
        for l in range(LANE_SLABS):
            rows = slice(s * SCAN_ROWS, (s + 1) * SCAN_ROWS)
            lanes = slice(l * LANES, (l + 1) * LANES)
            a_ref[d, l, s * SCAN_PITCH:s * SCAN_PITCH + SCAN_ROWS, :] = a[rows, lanes]
            u_ref[d, l, s * SCAN_PITCH:s * SCAN_PITCH + SCAN_ROWS, :] = u[rows, lanes]


def _rnn_core_kernel(fc_ref, fp_ref, fn_ref, bc_ref, bp_ref, bn_ref, cw_ref, cb_ref, wa_ref, ba_ref,
                     wi_ref, bi_ref, lam_ref, hf_ref, hb_ref, ext_ref, a_ref, u_ref, hl_ref, p_ref,
                     carry_ref):
    i = pl.program_id(1)
    n = pl.num_programs(1)

    @pl.when(i == 0)
    def _():
        carry_ref[...] = jnp.zeros_like(carry_ref)

    shared = (cw_ref, cb_ref, wa_ref, ba_ref, wi_ref, bi_ref, lam_ref, ext_ref, a_ref, u_ref)
    _lru_inputs(0, fc_ref, fp_ref, fn_ref, i == 0, i == n - 1, *shared)
    _lru_inputs(1, bc_ref, bp_ref, bn_ref, i == n - 1, i == 0, *shared)

    pairs = [(d, l) for d in range(2) for l in range(LANE_SLABS)]

    def body(jj, carry):
        out = []
        for (d, l), (h, p) in zip(pairs, carry):
            j = jj if d == 0 else SCAN_ROWS - 1 - jj
            rows = pl.ds(j, SCAN_BLOCKS, stride=SCAN_PITCH)
            a = a_ref[d, l, rows, :]
            h = a * h + u_ref[d, l, rows, :]
            p = a * p
            hl_ref[d, l, rows, :] = h
            p_ref[d, l, rows, :] = p
            out.append((h, p))
        return tuple(out)

    zero = jnp.zeros((SCAN_BLOCKS, LANES), F32)
    ends = ((zero, zero + 1.0),) * len(pairs)
    for jj in range(SCAN_ROWS):
        ends = body(jj, ends)

    for (d, l), (h_end, p_end) in zip(pairs, ends):
        lanes = slice(l * LANES, (l + 1) * LANES)
        out_ref = hf_ref if d == 0 else hb_ref
        c = carry_ref[d, 0:1, lanes]
        for s in (range(SCAN_BLOCKS) if d == 0 else reversed(range(SCAN_BLOCKS))):
            blk = slice(s * SCAN_PITCH, s * SCAN_PITCH + SCAN_ROWS)
            out_ref[s * SCAN_ROWS:(s + 1) * SCAN_ROWS, lanes] = hl_ref[d, l, blk, :] + p_ref[d, l, blk, :] * c
            c = h_end[s:s + 1, :] + p_end[s:s + 1, :] * c
        carry_ref[d, :, lanes] = jnp.broadcast_to(c, (SUBLANES, LANES))


def _rnn_core(xb, conv_w, conv_b, w_a, b_a, w_i, b_i, lam):
    s = xb.shape[0]
    n = s // TT
    halo = TT // SUBLANES
    last_halo = s // SUBLANES - 1
    cur = lambda t: pl.BlockSpec((TT, RNN_BLOCK_W), lambda c, i: (t(i, n), c))
    prv = lambda t: pl.BlockSpec((SUBLANES, RNN_BLOCK_W),
                                 lambda c, i: (jnp.maximum(t(i, n) * halo - 1, 0), c))
    nxt = lambda t: pl.BlockSpec((SUBLANES, RNN_BLOCK_W),
                                 lambda c, i: (jnp.minimum((t(i, n) + 1) * halo, last_halo), c))
    fwd = lambda i, n: i
    bwd = lambda i, n: n - 1 - i
    chan = lambda rows: pl.BlockSpec((rows, RNN_BLOCK_W), lambda c, i: (0, c))
    gate_w = pl.BlockSpec((2, 1, RNN_BLOCK_W, RNN_BLOCK_W), lambda c, i: (0, c, 0, 0))
    return pl.pallas_call(
        _rnn_core_kernel,
        grid=(RNN_BLOCKS, n),
        in_specs=[cur(fwd), prv(fwd), nxt(fwd), cur(bwd), prv(bwd), nxt(bwd),
                  chan(CONV_W), chan(1), gate_w, chan(2), gate_w, chan(2), chan(2)],
        out_specs=[pl.BlockSpec((TT, RNN_BLOCK_W), lambda c, i: (i, c)),
                   pl.BlockSpec((TT, RNN_BLOCK_W), lambda c, i: (n - 1 - i, c))],
        out_shape=[jax.ShapeDtypeStruct((s, D_RNN), F32), jax.ShapeDtypeStruct((s, D_RNN), F32)],
        scratch_shapes=[
            pltpu.VMEM((2, TT + 2 * SUBLANES, RNN_BLOCK_W), F32),
            pltpu.VMEM((2, LANE_SLABS, SCAN_BLOCKS * SCAN_PITCH, LANES), F32),
            pltpu.VMEM((2, LANE_SLABS, SCAN_BLOCKS * SCAN_PITCH, LANES), F32),
            pltpu.VMEM((2, LANE_SLABS, SCAN_BLOCKS * SCAN_PITCH, LANES), F32),
            pltpu.VMEM((2, LANE_SLABS, SCAN_BLOCKS * SCAN_PITCH, LANES), F32),
            pltpu.VMEM((2, SUBLANES, RNN_BLOCK_W), F32),
        ],
        compiler_params=_params("arbitrary", "arbitrary"),
        name="rnn_core",
    )(xb, xb, xb, xb, xb, xb, conv_w, conv_b, w_a, b_a, w_i, b_i, lam)


def _rnn_out_kernel(x_ref, g_ref, hf_ref, hb_ref, wy_ref, wo_ref, o_ref, wo16_ref):
    @pl.when(pl.program_id(0) == 0)
    def _():
        wo16_ref[...] = wo_ref[...].astype(BF16)

    x = x_ref[...]
    hn = _rms_normed(x, g_ref[...]).astype(BF16)
    gate = jax.nn.gelu(jnp.dot(hn, wy_ref[...], preferred_element_type=F32), approximate=True)
    a = ((hf_ref[...] + hb_ref[...]) * gate).astype(BF16)
    o_ref[...] = x + jnp.dot(a, wo16_ref[...], preferred_element_type=F32)


def _rnn_out_proj(x, g, hf, hb, w_gate_all, w_out_all, layer):
    s = x.shape[0]
    row = pl.BlockSpec((TM_OUT, D_MODEL), lambda i: (i, 0))
    return pl.pallas_call(
        _rnn_out_kernel,
        grid=(s // TM_OUT,),
        in_specs=[row,
                  pl.BlockSpec((1, D_MODEL), lambda i: (0, 0)),
                  row, row,
                  pl.BlockSpec((None, D_MODEL, D_RNN), lambda i: (layer, 0, 0),
                               pipeline_mode=pl.Buffered(1)),
                  pl.BlockSpec((None, D_RNN, D_MODEL), lambda i: (layer, 0, 0),
                               pipeline_mode=pl.Buffered(1))],
        out_specs=row,
        out_shape=jax.ShapeDtypeStruct((s, D_MODEL), F32),
        scratch_shapes=[pltpu.VMEM((D_RNN, D_MODEL), BF16)],
        compiler_params=_params("arbitrary"),
        name="rnn_out_proj",
    )(x, g, hf, hb, w_gate_all, w_out_all)


def _ffn_kernel(x_ref, g_ref, wg_ref, wu_ref, wd_ref, o_ref, hn_ref):
    @pl.when(pl.program_id(1) == 0)
    def _():
        x = x_ref[...]
        hn_ref[...] = _rms_normed(x, g_ref[...]).astype(BF16)
        o_ref[...] = x

    hn = hn_ref[...]
    gate = jnp.dot(hn, wg_ref[...].astype(BF16), preferred_element_type=F32)
    up = jnp.dot(hn, wu_ref[...].astype(BF16), preferred_element_type=F32)
    act = (jax.nn.silu(gate) * up).astype(BF16)
    o_ref[...] += jnp.dot(act, wd_ref[...].astype(BF16), preferred_element_type=F32)


def _ffn(x, g, w_gate, w_up, w_down, layer):
    s = x.shape[0]
    return pl.pallas_call(
        _ffn_kernel,
        grid=(s // TM_FFN, D_FF // TF),
        in_specs=[
            pl.BlockSpec((TM_FFN, D_MODEL), lambda i, f: (i, 0)),
            pl.BlockSpec((1, D_MODEL), lambda i, f: (0, 0)),
            pl.BlockSpec((None, D_MODEL, TF), lambda i, f: (layer, 0, f)),
            pl.BlockSpec((None, D_MODEL, TF), lambda i, f: (layer, 0, f)),
            pl.BlockSpec((None, TF, D_MODEL), lambda i, f: (layer, f, 0)),
        ],
        out_specs=pl.BlockSpec((TM_FFN, D_MODEL), lambda i, f: (i, 0)),
        out_shape=jax.ShapeDtypeStruct((s, D_MODEL), F32),
        scratch_shapes=[pltpu.VMEM((TM_FFN, D_MODEL), BF16)],
        compiler_params=_params("arbitrary", "arbitrary"),
        name="ffn",
    )(x, g, w_gate, w_up, w_down)


def _rope_tables(seq_len):
    freqs = ROPE_THETA ** (-jnp.arange(ROPE_FREQS, dtype=F32) / ROPE_FREQS)
    token = jnp.arange(seq_len, dtype=jnp.int32)[:, None]
    lane = jnp.arange(HEAD_DIM, dtype=jnp.int32)[None, :]
    pos = jnp.where(lane < 2 * ROPE_FREQS, token // GRID_W, token % GRID_W).astype(F32)
    ang = pos * jnp.tile(freqs, HEAD_DIM // ROPE_FREQS)[None, :]
    sign = jnp.where(lane % (2 * ROPE_FREQS) < ROPE_FREQS, -1.0, 1.0).astype(F32)
    return jnp.cos(ang), sign * jnp.sin(ang)


def kernel(x, norm_mix, norm_ffn, attn_w_qkv, attn_q_gain, attn_k_gain, attn_w_o, rnn_w_in, rnn_conv_w, rnn_conv_b, rnn_w_a, rnn_b_a, rnn_w_i, rnn_b_i, rnn_lambda, rnn_w_out, ffn_w_gate, ffn_w_up, ffn_w_down):
    b, s, d = x.shape
    assert (b, d) == (1, D_MODEL) and s % max(TM, TQ, TK, TT) == 0
    cos_t, sin_t = _rope_tables(s)
    w_gate_rnn = rnn_w_in[:, :, D_RNN:].astype(BF16)
    w_a, w_i = (0.5 * rnn_w_a).astype(BF16), (0.5 * rnn_w_i).astype(BF16)
    h = x.reshape(s, d)
    for i in range(DEPTH):
        j = i // N_MIXERS
        g_mix = norm_mix[i].reshape(1, d)
        if i % N_MIXERS == 0:
            qkv = _qkv_proj(h, g_mix, attn_w_qkv, j,
                            attn_q_gain[j].reshape(1, HEAD_DIM), attn_k_gain[j].reshape(1, HEAD_DIM),
                            cos_t, sin_t)
            o = _attention(qkv)
            h = _proj_residual(o, attn_w_o, j, h)
        else:
            xb = _rnn_in_proj(h, g_mix, rnn_w_in, j)
            hf, hb = _rnn_core(xb, rnn_conv_w[j], rnn_conv_b[j].reshape(1, D_RNN),
                               w_a[j], rnn_b_a[j], w_i[j], rnn_b_i[j], rnn_lambda[j])
            h = _rnn_out_proj(h, g_mix, hf, hb, w_gate_rnn, rnn_w_out, j)
        h = _ffn(h, norm_ffn[i].reshape(1, d), ffn_w_gate, ffn_w_up, ffn_w_down, i)
    return h.reshape(b, s, d)
```

```python
import jax
import jax.numpy as jnp
from jax import lax
from jax.experimental import pallas as pl
from jax.experimental.pallas import tpu as pltpu

F32 = jnp.float32
BF16 = jnp.bfloat16

D_MODEL = 2048
DEPTH = 4
N_MIXERS = 2
GRID_W = 64
ROPE_THETA = 10000.0
HEAD_DIM = 128
N_Q_HEADS = 16
N_KV_HEADS = 4
GQA_GROUP = N_Q_HEADS // N_KV_HEADS
ROPE_FREQS = HEAD_DIM // 4
QKV_DIM = (N_Q_HEADS + 2 * N_KV_HEADS) * HEAD_DIM
D_RNN = D_MODEL
RNN_BLOCK_W = 256
RNN_BLOCKS = D_RNN // RNN_BLOCK_W
CONV_W = 4
CONV_LEFT = 2
LRU_C = 8.0
D_FF = 5632
EPS = 1e-6
Q_SCALE = HEAD_DIM ** -0.5 * 1.4426950408889634

SUBLANES = 8
BF16_SUBLANES = 16
LANES = 128
MXU_COLS = 256
VMEM_LIMIT_BYTES = 56 * 1024 * 1024

TM = 512
TM_OUT = 256
HEADS_PER_SLAB = MXU_COLS // HEAD_DIM
QKV_SLABS = QKV_DIM // MXU_COLS
TM_FFN = 1024
TF = 256
TQ = 512
TK = 1024
TT = 512
SCAN_BLOCKS = SUBLANES
SCAN_ROWS = TT // SCAN_BLOCKS
SCAN_PITCH = SCAN_ROWS + SUBLANES
LANE_SLABS = RNN_BLOCK_W // LANES
VT_ROWS = HEAD_DIM + BF16_SUBLANES
EXP_ROWS = 32

def _params(*sem):
    return pltpu.CompilerParams(dimension_semantics=sem, vmem_limit_bytes=VMEM_LIMIT_BYTES)


def _rms_normed(x, g):
    ms = jnp.mean(x * x, axis=-1, keepdims=True)
    return x * lax.rsqrt(ms + EPS) * g


def _qkv_kernel(x_ref, g_ref, w_ref, qg_ref, kg_ref, cos_ref, sin_ref, o_ref, hn_ref, ya_ref, yb_ref):
    hn_ref[...] = _rms_normed(x_ref[...], g_ref[...]).astype(BF16)
    q_gain = qg_ref[...] * Q_SCALE
    lane = lax.broadcasted_iota(jnp.int32, (TM, HEAD_DIM), 1)
    first_half = (lane & (2 * ROPE_FREQS - 1)) < ROPE_FREQS

    y_refs = (ya_ref, yb_ref)

    def matmul(s):
        w = w_ref[:, s * MXU_COLS:(s + 1) * MXU_COLS].astype(BF16)
        y_refs[s % 2][...] = jnp.dot(hn_ref[...], w, preferred_element_type=F32)

    def epilogue(s):
        for h in range(s * HEADS_PER_SLAB, (s + 1) * HEADS_PER_SLAB):
            lanes = slice((h % HEADS_PER_SLAB) * HEAD_DIM, (h % HEADS_PER_SLAB + 1) * HEAD_DIM)
            yh = y_refs[s % 2][:, lanes]
            if h >= N_Q_HEADS + N_KV_HEADS:
                out = yh
            else:
                gain = q_gain if h < N_Q_HEADS else kg_ref[...]
                yn = yh * lax.rsqrt(jnp.mean(yh * yh, axis=-1, keepdims=True) + EPS) * gain
                partner = jnp.where(first_half,
                                    pltpu.roll(yn, HEAD_DIM - ROPE_FREQS, 1),
                                    pltpu.roll(yn, ROPE_FREQS, 1))
                out = yn * cos_ref[...] + partner * sin_ref[...]
            o_ref[:, h * HEAD_DIM:(h + 1) * HEAD_DIM] = out.astype(BF16)

    matmul(0)
    for s in range(1, QKV_SLABS):
        epilogue(s - 1)
        matmul(s)
    epilogue(QKV_SLABS - 1)


def _qkv_proj(x, g, w_all, layer, q_gain, k_gain, cos_t, sin_t):
    s = x.shape[0]
    return pl.pallas_call(
        _qkv_kernel,
        grid=(s // TM,),
        in_specs=[
            pl.BlockSpec((TM, D_MODEL), lambda i: (i, 0)),
            pl.BlockSpec((1, D_MODEL), lambda i: (0, 0)),
            pl.BlockSpec((None, D_MODEL, QKV_DIM), lambda i: (layer, 0, 0),
                         pipeline_mode=pl.Buffered(1)),
            pl.BlockSpec((1, HEAD_DIM), lambda i: (0, 0)),
            pl.BlockSpec((1, HEAD_DIM), lambda i: (0, 0)),
            pl.BlockSpec((TM, HEAD_DIM), lambda i: (i, 0)),
            pl.BlockSpec((TM, HEAD_DIM), lambda i: (i, 0)),
        ],
        out_specs=pl.BlockSpec((TM, QKV_DIM), lambda i: (i, 0)),
        out_shape=jax.ShapeDtypeStruct((s, QKV_DIM), BF16),
        scratch_shapes=[pltpu.VMEM((TM, D_MODEL), BF16),
                        pltpu.VMEM((TM, MXU_COLS), F32),
                        pltpu.VMEM((TM, MXU_COLS), F32)],
        compiler_params=_params("arbitrary"),
        name="qkv_proj",
    )(x, g, w_all, q_gain, k_gain, cos_t, sin_t)


def _attn_kernel(q_ref, k_ref, v_ref, o_ref, vt_ref, acc_ref, sa_ref, sb_ref, pa_ref, pb_ref, state_ref):
    n_chunks = k_ref.shape[0] // TK
    i = pl.program_id(1)
    last_tile = pl.num_programs(1) - 1
    heads = range(GQA_GROUP)

    def score_stage(u, st_ref):
        k0 = (u % n_chunks) * TK
        k0 = k0 if isinstance(k0, int) else pl.multiple_of(k0, TK)
        q0 = pl.multiple_of(jnp.minimum(u // n_chunks, last_tile) * TQ, TQ)
        kc = k_ref[pl.ds(k0, TK), :]
        chunk_max = []
        for g in heads:
            q = q_ref[pl.ds(q0, TQ), g * HEAD_DIM:(g + 1) * HEAD_DIM]
            st = lax.dot_general(kc, q, (((1,), (1,)), ((), ())), preferred_element_type=F32)
            st_ref[g] = st
            chunk_max.append(jnp.max(st, axis=0, keepdims=True))
        return tuple(chunk_max)

    def exp_stage(u, st_ref, pt_ref, ms, chunk_max):
        first = u % n_chunks == 0
        new_ms, alphas = [], []
        for g in heads:
            m_old = jnp.where(first, -jnp.inf, ms[g])
            m_new = jnp.maximum(m_old, chunk_max[g])
            for r in range(0, TK, EXP_ROWS):
                pt_ref[g, r:r + EXP_ROWS, :] = jnp.exp2(st_ref[g, r:r + EXP_ROWS, :] - m_new).astype(BF16)
            new_ms.append(m_new)
            alphas.append(jnp.exp2(m_old - m_new))
        return tuple(new_ms), tuple(alphas)

    def out_stage(c, pt_ref, alphas):
        for g in heads:
            acc_ref[g] = alphas[g] * acc_ref[g] + jnp.dot(vt_ref[c], pt_ref[g], preferred_element_type=F32)

    def save_state(ms, alphas, chunk_max):
        for k, rows in enumerate((ms, alphas, chunk_max)):
            for g in heads:
                state_ref[k, g] = rows[g]

    @pl.when(i == 0)
    def _():
        for c in range(n_chunks):
            vc = v_ref[c * TK:(c + 1) * TK, :].astype(F32)
            vt_ref[c, :HEAD_DIM, :] = vc.T.astype(BF16)
            vt_ref[c, HEAD_DIM:, :] = jnp.ones((VT_ROWS - HEAD_DIM, TK), BF16)
        acc_ref[...] = jnp.zeros_like(acc_ref)
        ms = (jnp.full((1, TQ), -jnp.inf, F32),) * GQA_GROUP
        chunk_max = score_stage(0, sa_ref)
        ms, alphas = exp_stage(0, sa_ref, pa_ref, ms, chunk_max)
        save_state(ms, alphas, score_stage(1, sb_ref))

    def trip(c, carry, s_cur, p_cur, s_nxt, p_nxt):
        u = i * n_chunks + c
        ms, alphas, chunk_max = carry
        out_stage(c, p_cur, alphas)
        ms, alphas = exp_stage(u + 1, s_nxt, p_nxt, ms, chunk_max)
        chunk_max = score_stage(u + 2, s_cur)
        return ms, alphas, chunk_max

    def step(c, carry):
        return lax.cond(c % 2 == 0,
                        lambda cr: trip(c, cr, sa_ref, pa_ref, sb_ref, pb_ref),
                        lambda cr: trip(c, cr, sb_ref, pb_ref, sa_ref, pa_ref),
                        carry)

    carry = tuple(tuple(state_ref[k, g] for g in heads) for k in range(3))
    save_state(*lax.fori_loop(0, n_chunks, step, carry))
    for g in heads:
        acc = acc_ref[g]
        out_t = acc[:HEAD_DIM, :] / acc[HEAD_DIM:HEAD_DIM + 1, :]
        o_ref[:, g * HEAD_DIM:(g + 1) * HEAD_DIM] = out_t.T.astype(BF16)


def _attention(qkv):
    s = qkv.shape[0]
    assert (s // TK) % 2 == 0, "the a / b buffer parity of the attention pipeline needs an even chunk count"
    group_w = GQA_GROUP * HEAD_DIM
    return pl.pallas_call(
        _attn_kernel,
        grid=(N_KV_HEADS, s // TQ),
        in_specs=[
            pl.BlockSpec((s, group_w), lambda h, i: (0, h), pipeline_mode=pl.Buffered(1)),
            pl.BlockSpec((s, HEAD_DIM), lambda h, i: (0, N_Q_HEADS + h)),
            pl.BlockSpec((s, HEAD_DIM), lambda h, i: (0, N_Q_HEADS + N_KV_HEADS + h)),
        ],
        out_specs=pl.BlockSpec((TQ, group_w), lambda h, i: (i, h)),
        out_shape=jax.ShapeDtypeStruct((s, N_Q_HEADS * HEAD_DIM), BF16),
        scratch_shapes=[pltpu.VMEM((s // TK, VT_ROWS, TK), BF16),
                        pltpu.VMEM((GQA_GROUP, VT_ROWS, TQ), F32),
                        pltpu.VMEM((GQA_GROUP, TK, TQ), F32),
                        pltpu.VMEM((GQA_GROUP, TK, TQ), F32),
                        pltpu.VMEM((GQA_GROUP, TK, TQ), BF16),
                        pltpu.VMEM((GQA_GROUP, TK, TQ), BF16),
                        pltpu.VMEM((3, GQA_GROUP, 1, TQ), F32)],
        compiler_params=_params("arbitrary", "arbitrary"),
        name="attention",
    )(qkv, qkv, qkv)


def _proj_res_kernel(a_ref, w_ref, x_ref, o_ref, w16_ref):
    @pl.when(pl.program_id(0) == 0)
    def _():
        w16_ref[...] = w_ref[...].astype(BF16)

    o_ref[...] = x_ref[...] + jnp.dot(a_ref[...], w16_ref[...], preferred_element_type=F32)


def _proj_residual(a, w_all, layer, x):
    s, k = a.shape
    return pl.pallas_call(
        _proj_res_kernel,
        grid=(s // TM,),
        in_specs=[
            pl.BlockSpec((TM, k), lambda i: (i, 0)),
            pl.BlockSpec((None, k, D_MODEL), lambda i: (layer, 0, 0), pipeline_mode=pl.Buffered(1)),
            pl.BlockSpec((TM, D_MODEL), lambda i: (i, 0)),
        ],
        out_specs=pl.BlockSpec((TM, D_MODEL), lambda i: (i, 0)),
        out_shape=jax.ShapeDtypeStruct((s, D_MODEL), F32),
        scratch_shapes=[pltpu.VMEM((k, D_MODEL), BF16)],
        compiler_params=_params("arbitrary"),
        name="proj_residual",
    )(a, w_all, x)


def _rnn_in_kernel(x_ref, g_ref, w_ref, wg_ref, xb_ref, wg16_ref, w16_ref):
    @pl.when(pl.program_id(0) == 0)
    def _():
        w16_ref[...] = w_ref[...].astype(BF16)

    hn = _rms_normed(x_ref[...], g_ref[...]).astype(BF16)
    xb_ref[...] = jnp.dot(hn, w16_ref[...], preferred_element_type=F32)
    wg16_ref[...] = wg_ref[...].astype(BF16)


def _rnn_in_proj(x, g, w_all, layer):
    s = x.shape[0]
    steps = s // TM
    gate_tn = D_RNN // steps
    assert gate_tn * steps == D_RNN and gate_tn % LANES == 0
    return pl.pallas_call(
        _rnn_in_kernel,
        grid=(steps,),
        in_specs=[
            pl.BlockSpec((TM, D_MODEL), lambda i: (i, 0)),
            pl.BlockSpec((1, D_MODEL), lambda i: (0, 0)),
            pl.BlockSpec((None, D_MODEL, D_RNN), lambda i: (layer, 0, 0), pipeline_mode=pl.Buffered(1)),
            pl.BlockSpec((None, D_MODEL, gate_tn), lambda i: (layer, 0, steps + i)),
        ],
        out_specs=[pl.BlockSpec((TM, D_RNN), lambda i: (i, 0)),
                   pl.BlockSpec((D_MODEL, gate_tn), lambda i: (0, i))],
        out_shape=[jax.ShapeDtypeStruct((s, D_RNN), F32),
                   jax.ShapeDtypeStruct((D_MODEL, D_RNN), BF16)],
        scratch_shapes=[pltpu.VMEM((D_MODEL, D_RNN), BF16)],
        compiler_params=_params("arbitrary"),
        name="rnn_in_proj",
    )(x, g, w_all, w_all)


def _softplus(z):
    return jnp.maximum(z, 0.0) + jnp.log1p(jnp.exp(-jnp.abs(z)))


def _lru_inputs(d, cur_ref, prev_ref, next_ref, first, last, cw_ref, cb_ref, wa_ref, ba_ref, wi_ref,
                bi_ref, lam_ref, ext_ref, a_ref, u_ref):
    ext_ref[d, 0:SUBLANES, :] = jnp.where(first, 0.0, prev_ref[...])
    ext_ref[d, SUBLANES:SUBLANES + TT, :] = cur_ref[...]
    ext_ref[d, SUBLANES + TT:2 * SUBLANES + TT, :] = jnp.where(last, 0.0, next_ref[...])
    ext = ext_ref[d]
    xc = cb_ref[...]
    for k in range(CONV_W):
        shift = (CONV_LEFT - k) % ext.shape[0]
        tap = ext if shift == 0 else pltpu.roll(ext, shift, 0)
        xc = xc + cw_ref[k:k + 1, :] * tap[SUBLANES:SUBLANES + TT, :]
    xb16 = xc.astype(BF16)
    ta = jnp.tanh(jnp.dot(xb16, wa_ref[d, 0], preferred_element_type=F32) + 0.5 * ba_ref[d:d + 1, :])
    ti = jnp.tanh(jnp.dot(xb16, wi_ref[d, 0], preferred_element_type=F32) + 0.5 * bi_ref[d:d + 1, :])
    i = 0.5 * ti + 0.5
    half_c = (-0.5 * LRU_C) * _softplus(-lam_ref[d:d + 1, :])
    log_a = ta * half_c + half_c
    a = jnp.exp(log_a)
    one_minus_a2 = -jnp.tanh(log_a) * (a * a + 1.0)
    root = jnp.where(one_minus_a2 > 0.0, one_minus_a2 * lax.rsqrt(one_minus_a2), 0.0)
    u = root * (i * xc)
    for s in range(SCAN_BLOCKS):
        for l in range(LANE_SLABS):
            rows = slice(s * SCAN_ROWS, (s + 1) * SCAN_ROWS)
            lanes = slice(l * LANES, (l + 1) * LANES)
            a_ref[d, l, s * SCAN_PITCH:s * SCAN_PITCH + SCAN_ROWS, :] = a[rows, lanes]
            u_ref[d, l, s * SCAN_PITCH:s * SCAN_PITCH + SCAN_ROWS, :] = u[rows, lanes]


def _rnn_core_kernel(fc_ref, fp_ref, fn_ref, bc_ref, bp_ref, bn_ref, cw_ref, cb_ref, wa_ref, ba_ref,
                     wi_ref, bi_ref, lam_ref, hf_ref, hb_ref, ext_ref, a_ref, u_ref, hl_ref, p_ref,
                     carry_ref):
    i = pl.program_id(1)
    n = pl.num_programs(1)

    @pl.when(i == 0)
    def _():
        carry_ref[...] = jnp.zeros_like(carry_ref)

    shared = (cw_ref, cb_ref, wa_ref, ba_ref, wi_ref, bi_ref, lam_ref, ext_ref, a_ref, u_ref)
    _lru_inputs(0, fc_ref, fp_ref, fn_ref, i == 0, i == n - 1, *shared)
    _lru_inputs(1, bc_ref, bp_ref, bn_ref, i == n - 1, i == 0, *shared)

    pairs = [(d, l) for d in range(2) for l in range(LANE_SLABS)]

    def body(jj, carry):
        out = []
        for (d, l), (h, p) in zip(pairs, carry):
            j = jj if d == 0 else SCAN_ROWS - 1 - jj
            rows = pl.ds(j, SCAN_BLOCKS, stride=SCAN_PITCH)
            a = a_ref[d, l, rows, :]
            h = a * h + u_ref[d, l, rows, :]
            p = a * p
            hl_ref[d, l, rows, :] = h
            p_ref[d, l, rows, :] = p
            out.append((h, p))
        return tuple(out)

    zero = jnp.zeros((SCAN_BLOCKS, LANES), F32)
    ends = ((zero, zero + 1.0),) * len(pairs)
    for jj in range(SCAN_ROWS):
        ends = body(jj, ends)

    for (d, l), (h_end, p_end) in zip(pairs, ends):
        lanes = slice(l * LANES, (l + 1) * LANES)
        out_ref = hf_ref if d == 0 else hb_ref
        c = carry_ref[d, 0:1, lanes]
        for s in (range(SCAN_BLOCKS) if d == 0 else reversed(range(SCAN_BLOCKS))):
            blk = slice(s * SCAN_PITCH, s * SCAN_PITCH + SCAN_ROWS)
            out_ref[s * SCAN_ROWS:(s + 1) * SCAN_ROWS, lanes] = hl_ref[d, l, blk, :] + p_ref[d, l, blk, :] * c
            c = h_end[s:s + 1, :] + p_end[s:s + 1, :] * c
        carry_ref[d, :, lanes] = jnp.broadcast_to(c, (SUBLANES, LANES))


def _rnn_core(xb, conv_w, conv_b, w_a, b_a, w_i, b_i, lam):
    s = xb.shape[0]
    n = s // TT
    halo = TT // SUBLANES
    last_halo = s // SUBLANES - 1
    cur = lambda t: pl.BlockSpec((TT, RNN_BLOCK_W), lambda c, i: (t(i, n), c))
    prv = lambda t: pl.BlockSpec((SUBLANES, RNN_BLOCK_W),
                                 lambda c, i: (jnp.maximum(t(i, n) * halo - 1, 0), c))
    nxt = lambda t: pl.BlockSpec((SUBLANES, RNN_BLOCK_W),
                                 lambda c, i: (jnp.minimum((t(i, n) + 1) * halo, last_halo), c))
    fwd = lambda i, n: i
    bwd = lambda i, n: n - 1 - i
    chan = lambda rows: pl.BlockSpec((rows, RNN_BLOCK_W), lambda c, i: (0, c))
    gate_w = pl.BlockSpec((2, 1, RNN_BLOCK_W, RNN_BLOCK_W), lambda c, i: (0, c, 0, 0))
    return pl.pallas_call(
        _rnn_core_kernel,
        grid=(RNN_BLOCKS, n),
        in_specs=[cur(fwd), prv(fwd), nxt(fwd), cur(bwd), prv(bwd), nxt(bwd),
                  chan(CONV_W), chan(1), gate_w, chan(2), gate_w, chan(2), chan(2)],
        out_specs=[pl.BlockSpec((TT, RNN_BLOCK_W), lambda c, i: (i, c)),
                   pl.BlockSpec((TT, RNN_BLOCK_W), lambda c, i: (n - 1 - i, c))],
        out_shape=[jax.ShapeDtypeStruct((s, D_RNN), F32), jax.ShapeDtypeStruct((s, D_RNN), F32)],
        scratch_shapes=[
            pltpu.VMEM((2, TT + 2 * SUBLANES, RNN_BLOCK_W), F32),
            pltpu.VMEM((2, LANE_SLABS, SCAN_BLOCKS * SCAN_PITCH, LANES), F32),
            pltpu.VMEM((2, LANE_SLABS, SCAN_BLOCKS * SCAN_PITCH, LANES), F32),
            pltpu.VMEM((2, LANE_SLABS, SCAN_BLOCKS * SCAN_PITCH, LANES), F32),
            pltpu.VMEM((2, LANE_SLABS, SCAN_BLOCKS * SCAN_PITCH, LANES), F32),
            pltpu.VMEM((2, SUBLANES, RNN_BLOCK_W), F32),
        ],
        compiler_params=_params("arbitrary", "arbitrary"),
        name="rnn_core",
    )(xb, xb, xb, xb, xb, xb, conv_w, conv_b, w_a, b_a, w_i, b_i, lam)


def _rnn_out_kernel(x_ref, g_ref, hf_ref, hb_ref, wy_ref, wo_ref, o_ref, wo16_ref):
    @pl.when(pl.program_id(0) == 0)
    def _():
        wo16_ref[...] = wo_ref[...].astype(BF16)

    x = x_ref[...]
    hn = _rms_normed(x, g_ref[...]).astype(BF16)
    gate = jax.nn.gelu(jnp.dot(hn, wy_ref[...], preferred_element_type=F32), approximate=True)
    a = ((hf_ref[...] + hb_ref[...]) * gate).astype(BF16)
    o_ref[...] = x + jnp.dot(a, wo16_ref[...], preferred_element_type=F32)


def _rnn_out_proj(x, g, hf, hb, w_gate, w_out_all, layer):
    s = x.shape[0]
    row = pl.BlockSpec((TM_OUT, D_MODEL), lambda i: (i, 0))
    return pl.pallas_call(
        _rnn_out_kernel,
        grid=(s // TM_OUT,),
        in_specs=[row,
                  pl.BlockSpec((1, D_MODEL), lambda i: (0, 0)),
                  row, row,
                  pl.BlockSpec((D_MODEL, D_RNN), lambda i: (0, 0), pipeline_mode=pl.Buffered(1)),
                  pl.BlockSpec((None, D_RNN, D_MODEL), lambda i: (layer, 0, 0),
                               pipeline_mode=pl.Buffered(1))],
        out_specs=row,
        out_shape=jax.ShapeDtypeStruct((s, D_MODEL), F32),
        scratch_shapes=[pltpu.VMEM((D_RNN, D_MODEL), BF16)],
        compiler_params=_params("arbitrary"),
        name="rnn_out_proj",
    )(x, g, hf, hb, w_gate, w_out_all)


def _ffn_kernel(x_ref, g_ref, wg_ref, wu_ref, wd_ref, o_ref, hn_ref):
    @pl.when(pl.program_id(1) == 0)
    def _():
        x = x_ref[...]
        hn_ref[...] = _rms_normed(x, g_ref[...]).astype(BF16)
        o_ref[...] = x

    hn = hn_ref[...]
    gate = jnp.dot(hn, wg_ref[...].astype(BF16), preferred_element_type=F32)
    up = jnp.dot(hn, wu_ref[...].astype(BF16), preferred_element_type=F32)
    act = (jax.nn.silu(gate) * up).astype(BF16)
    o_ref[...] += jnp.dot(act, wd_ref[...].astype(BF16), preferred_element_type=F32)


def _ffn(x, g, w_gate, w_up, w_down, layer):
    s = x.shape[0]
    return pl.pallas_call(
        _ffn_kernel,
        grid=(s // TM_FFN, D_FF // TF),
        in_specs=[
            pl.BlockSpec((TM_FFN, D_MODEL), lambda i, f: (i, 0)),
            pl.BlockSpec((1, D_MODEL), lambda i, f: (0, 0)),
            pl.BlockSpec((None, D_MODEL, TF), lambda i, f: (layer, 0, f)),
            pl.BlockSpec((None, D_MODEL, TF), lambda i, f: (layer, 0, f)),
            pl.BlockSpec((None, TF, D_MODEL), lambda i, f: (layer, f, 0)),
        ],
        out_specs=pl.BlockSpec((TM_FFN, D_MODEL), lambda i, f: (i, 0)),
        out_shape=jax.ShapeDtypeStruct((s, D_MODEL), F32),
        scratch_shapes=[pltpu.VMEM((TM_FFN, D_MODEL), BF16)],
        compiler_params=_params("arbitrary", "arbitrary"),
        name="ffn",
    )(x, g, w_gate, w_up, w_down)


def _rope_tables(seq_len):
    freqs = ROPE_THETA ** (-jnp.arange(ROPE_FREQS, dtype=F32) / ROPE_FREQS)
    token = jnp.arange(seq_len, dtype=jnp.int32)[:, None]
    lane = jnp.arange(HEAD_DIM, dtype=jnp.int32)[None, :]
    pos = jnp.where(lane < 2 * ROPE_FREQS, token // GRID_W, token % GRID_W).astype(F32)
    ang = pos * jnp.tile(freqs, HEAD_DIM // ROPE_FREQS)[None, :]
    sign = jnp.where(lane % (2 * ROPE_FREQS) < ROPE_FREQS, -1.0, 1.0).astype(F32)
    return jnp.cos(ang), sign * jnp.sin(ang)


def kernel(x, norm_mix, norm_ffn, attn_w_qkv, attn_q_gain, attn_k_gain, attn_w_o, rnn_w_in, rnn_conv_w, rnn_conv_b, rnn_w_a, rnn_b_a, rnn_w_i, rnn_b_i, rnn_lambda, rnn_w_out, ffn_w_gate, ffn_w_up, ffn_w_down):
    b, s, d = x.shape
    assert (b, d) == (1, D_MODEL) and s % max(TM, TQ, TK, TT) == 0
    cos_t, sin_t = _rope_tables(s)
    w_a, w_i = (0.5 * rnn_w_a).astype(BF16), (0.5 * rnn_w_i).astype(BF16)
    h = x.reshape(s, d)
    for i in range(DEPTH):
        j = i // N_MIXERS
        g_mix = norm_mix[i].reshape(1, d)
        if i % N_MIXERS == 0:
            qkv = _qkv_proj(h, g_mix, attn_w_qkv, j,
                            attn_q_gain[j].reshape(1, HEAD_DIM), attn_k_gain[j].reshape(1, HEAD_DIM),
                            cos_t, sin_t)
            o = _attention(qkv)
            h = _proj_residual(o, attn_w_o, j, h)
        else:
            xb, w_gate_rnn = _rnn_in_proj(h, g_mix, rnn_w_in, j)
            hf, hb = _rnn_core(xb, rnn_conv_w[j], rnn_conv_b[j].reshape(1, D_RNN),
                               w_a[j], rnn_b_a[j], w_i[j], rnn_b_i[j], rnn_lambda[j])
            h = _rnn_out_proj(h, g_mix, hf, hb, w_gate_rnn, rnn_w_out, j)
        h = _ffn(h, norm_ffn[i].reshape(1, d), ffn_w_gate, ffn_w_up, ffn_w_down, i)
    return h.reshape(b, s, d)
```

```python
import jax
import jax.numpy as jnp
from jax import lax
from jax.experimental import pallas as pl
from jax.experimental.pallas import tpu as pltpu

F32 = jnp.float32
BF16 = jnp.bfloat16

D_MODEL = 2048
DEPTH = 4
N_MIXERS = 2
GRID_W = 64
ROPE_THETA = 10000.0
HEAD_DIM = 128
N_Q_HEADS = 16
N_KV_HEADS = 4
GQA_GROUP = N_Q_HEADS // N_KV_HEADS
ROPE_FREQS = HEAD_DIM // 4
QKV_DIM = (N_Q_HEADS + 2 * N_KV_HEADS) * HEAD_DIM
D_RNN = D_MODEL
RNN_BLOCK_W = 256
RNN_BLOCKS = D_RNN // RNN_BLOCK_W
CONV_W = 4
CONV_LEFT = 2
LRU_C = 8.0
D_FF = 5632
EPS = 1e-6
Q_SCALE = HEAD_DIM ** -0.5 * 1.4426950408889634

SUBLANES = 8
BF16_SUBLANES = 16
LANES = 128
MXU_COLS = 256
VMEM_LIMIT_BYTES = 56 * 1024 * 1024

TM = 512
TM_OUT = 256
HEADS_PER_SLAB = MXU_COLS // HEAD_DIM
QKV_SLABS = QKV_DIM // MXU_COLS
TM_FFN = 1024
TF = 256
TQ = 512
TK = 1024
TT = 512
SCAN_BLOCKS = SUBLANES
SCAN_ROWS = TT // SCAN_BLOCKS
SCAN_PITCH = SCAN_ROWS + SUBLANES
LANE_SLABS = RNN_BLOCK_W // LANES
VT_ROWS = HEAD_DIM + BF16_SUBLANES
EXP_ROWS = 32

def _params(*sem):
    return pltpu.CompilerParams(dimension_semantics=sem, vmem_limit_bytes=VMEM_LIMIT_BYTES)


def _rms_normed(x, g):
    ms = jnp.mean(x * x, axis=-1, keepdims=True)
    return x * lax.rsqrt(ms + EPS) * g


def _qkv_kernel(x_ref, g_ref, w_ref, qg_ref, kg_ref, cos_ref, sin_ref, o_ref, hn_ref, ya_ref, yb_ref):
    hn_ref[...] = _rms_normed(x_ref[...], g_ref[...]).astype(BF16)
    q_gain = qg_ref[...] * Q_SCALE
    lane = lax.broadcasted_iota(jnp.int32, (TM, HEAD_DIM), 1)
    first_half = (lane & (2 * ROPE_FREQS - 1)) < ROPE_FREQS

    y_refs = (ya_ref, yb_ref)

    def matmul(s):
        w = w_ref[:, s * MXU_COLS:(s + 1) * MXU_COLS].astype(BF16)
        y_refs[s % 2][...] = jnp.dot(hn_ref[...], w, preferred_element_type=F32)

    def epilogue(s):
        for h in range(s * HEADS_PER_SLAB, (s + 1) * HEADS_PER_SLAB):
            lanes = slice((h % HEADS_PER_SLAB) * HEAD_DIM, (h % HEADS_PER_SLAB + 1) * HEAD_DIM)
            yh = y_refs[s % 2][:, lanes]
            if h >= N_Q_HEADS + N_KV_HEADS:
                out = yh
            else:
                gain = q_gain if h < N_Q_HEADS else kg_ref[...]
                yn = yh * lax.rsqrt(jnp.mean(yh * yh, axis=-1, keepdims=True) + EPS) * gain
                partner = jnp.where(first_half,
                                    pltpu.roll(yn, HEAD_DIM - ROPE_FREQS, 1),
                                    pltpu.roll(yn, ROPE_FREQS, 1))
                out = yn * cos_ref[...] + partner * sin_ref[...]
            o_ref[:, h * HEAD_DIM:(h + 1) * HEAD_DIM] = out.astype(BF16)

    matmul(0)
    for s in range(1, QKV_SLABS):
        epilogue(s - 1)
        matmul(s)
    epilogue(QKV_SLABS - 1)


def _qkv_proj(x, g, w_all, layer, q_gain, k_gain, cos_t, sin_t):
    s = x.shape[0]
    return pl.pallas_call(
        _qkv_kernel,
        grid=(s // TM,),
        in_specs=[
            pl.BlockSpec((TM, D_MODEL), lambda i: (i, 0)),
            pl.BlockSpec((1, D_MODEL), lambda i: (0, 0)),
            pl.BlockSpec((None, D_MODEL, QKV_DIM), lambda i: (layer, 0, 0),
                         pipeline_mode=pl.Buffered(1)),
            pl.BlockSpec((1, HEAD_DIM), lambda i: (0, 0)),
            pl.BlockSpec((1, HEAD_DIM), lambda i: (0, 0)),
            pl.BlockSpec((TM, HEAD_DIM), lambda i: (i, 0)),
            pl.BlockSpec((TM, HEAD_DIM), lambda i: (i, 0)),
        ],
        out_specs=pl.BlockSpec((TM, QKV_DIM), lambda i: (i, 0)),
        out_shape=jax.ShapeDtypeStruct((s, QKV_DIM), BF16),
        scratch_shapes=[pltpu.VMEM((TM, D_MODEL), BF16),
                        pltpu.VMEM((TM, MXU_COLS), F32),
                        pltpu.VMEM((TM, MXU_COLS), F32)],
        compiler_params=_params("arbitrary"),
        name="qkv_proj",
    )(x, g, w_all, q_gain, k_gain, cos_t, sin_t)


def _attn_kernel(q_ref, k_ref, v_ref, o_ref, vt_ref, acc_ref, sa_ref, sb_ref, pa_ref, pb_ref, state_ref):
    n_chunks = k_ref.shape[0] // TK
    i = pl.program_id(1)
    last_tile = pl.num_programs(1) - 1
    heads = range(GQA_GROUP)

    def score_stage(u, st_ref):
        k0 = (u % n_chunks) * TK
        k0 = k0 if isinstance(k0, int) else pl.multiple_of(k0, TK)
        q0 = pl.multiple_of(jnp.minimum(u // n_chunks, last_tile) * TQ, TQ)
        kc = k_ref[pl.ds(k0, TK), :]
        chunk_max = []
        for g in heads:
            q = q_ref[pl.ds(q0, TQ), g * HEAD_DIM:(g + 1) * HEAD_DIM]
            st = lax.dot_general(kc, q, (((1,), (1,)), ((), ())), preferred_element_type=F32)
            st_ref[g] = st
            chunk_max.append(jnp.max(st, axis=0, keepdims=True))
        return tuple(chunk_max)

    def exp_stage(u, st_ref, pt_ref, ms, chunk_max):
        first = u % n_chunks == 0
        new_ms, alphas = [], []
        for g in heads:
            m_old = jnp.where(first, -jnp.inf, ms[g])
            m_new = jnp.maximum(m_old, chunk_max[g])
            for r in range(0, TK, EXP_ROWS):
                pt_ref[g, r:r + EXP_ROWS, :] = jnp.exp2(st_ref[g, r:r + EXP_ROWS, :] - m_new).astype(BF16)
            new_ms.append(m_new)
            alphas.append(jnp.exp2(m_old - m_new))
        return tuple(new_ms), tuple(alphas)

    def out_stage(c, pt_ref, alphas):
        for g in heads:
            acc_ref[g] = alphas[g] * acc_ref[g] + jnp.dot(vt_ref[c], pt_ref[g], preferred_element_type=F32)

    def save_state(ms, alphas, chunk_max):
        for k, rows in enumerate((ms, alphas, chunk_max)):
            for g in heads:
                state_ref[k, g] = rows[g]

    @pl.when(i == 0)
    def _():
        for c in range(n_chunks):
            vc = v_ref[c * TK:(c + 1) * TK, :].astype(F32)
            vt_ref[c, :HEAD_DIM, :] = vc.T.astype(BF16)
            vt_ref[c, HEAD_DIM:, :] = jnp.ones((VT_ROWS - HEAD_DIM, TK), BF16)
        acc_ref[...] = jnp.zeros_like(acc_ref)
        ms = (jnp.full((1, TQ), -jnp.inf, F32),) * GQA_GROUP
        chunk_max = score_stage(0, sa_ref)
        ms, alphas = exp_stage(0, sa_ref, pa_ref, ms, chunk_max)
        save_state(ms, alphas, score_stage(1, sb_ref))

    def trip(c, carry, s_cur, p_cur, s_nxt, p_nxt):
        u = i * n_chunks + c
        ms, alphas, chunk_max = carry
        out_stage(c, p_cur, alphas)
        ms, alphas = exp_stage(u + 1, s_nxt, p_nxt, ms, chunk_max)
        chunk_max = score_stage(u + 2, s_cur)
        return ms, alphas, chunk_max

    def step(c, carry):
        return lax.cond(c % 2 == 0,
                        lambda cr: trip(c, cr, sa_ref, pa_ref, sb_ref, pb_ref),
                        lambda cr: trip(c, cr, sb_ref, pb_ref, sa_ref, pa_ref),
                        carry)

    carry = tuple(tuple(state_ref[k, g] for g in heads) for k in range(3))
    save_state(*lax.fori_loop(0, n_chunks, step, carry))
    for g in heads:
        acc = acc_ref[g]
        out_t = acc[:HEAD_DIM, :] / acc[HEAD_DIM:HEAD_DIM + 1, :]
        o_ref[:, g * HEAD_DIM:(g + 1) * HEAD_DIM] = out_t.T.astype(BF16)


def _attention(qkv):
    s = qkv.shape[0]
    assert (s // TK) % 2 == 0, "the a / b buffer parity of the attention pipeline needs an even chunk count"
    group_w = GQA_GROUP * HEAD_DIM
    return pl.pallas_call(
        _attn_kernel,
        grid=(N_KV_HEADS, s // TQ),
        in_specs=[
            pl.BlockSpec((s, group_w), lambda h, i: (0, h)),
            pl.BlockSpec((s, HEAD_DIM), lambda h, i: (0, N_Q_HEADS + h)),
            pl.BlockSpec((s, HEAD_DIM), lambda h, i: (0, N_Q_HEADS + N_KV_HEADS + h)),
        ],
        out_specs=pl.BlockSpec((TQ, group_w), lambda h, i: (i, h)),
        out_shape=jax.ShapeDtypeStruct((s, N_Q_HEADS * HEAD_DIM), BF16),
        scratch_shapes=[pltpu.VMEM((s // TK, VT_ROWS, TK), BF16),
                        pltpu.VMEM((GQA_GROUP, VT_ROWS, TQ), F32),
                        pltpu.VMEM((GQA_GROUP, TK, TQ), F32),
                        pltpu.VMEM((GQA_GROUP, TK, TQ), F32),
                        pltpu.VMEM((GQA_GROUP, TK, TQ), BF16),
                        pltpu.VMEM((GQA_GROUP, TK, TQ), BF16),
                        pltpu.VMEM((3, GQA_GROUP, 1, TQ), F32)],
        compiler_params=_params("arbitrary", "arbitrary"),
        name="attention",
    )(qkv, qkv, qkv)


def _proj_res_kernel(a_ref, w_ref, x_ref, o_ref, w16_ref):
    @pl.when(pl.program_id(0) == 0)
    def _():
        w16_ref[...] = w_ref[...].astype(BF16)

    o_ref[...] = x_ref[...] + jnp.dot(a_ref[...], w16_ref[...], preferred_element_type=F32)


def _proj_residual(a, w_all, layer, x):
    s, k = a.shape
    return pl.pallas_call(
        _proj_res_kernel,
        grid=(s // TM,),
        in_specs=[
            pl.BlockSpec((TM, k), lambda i: (i, 0)),
            pl.BlockSpec((None, k, D_MODEL), lambda i: (layer, 0, 0), pipeline_mode=pl.Buffered(1)),
            pl.BlockSpec((TM, D_MODEL), lambda i: (i, 0)),
        ],
        out_specs=pl.BlockSpec((TM, D_MODEL), lambda i: (i, 0)),
        out_shape=jax.ShapeDtypeStruct((s, D_MODEL), F32),
        scratch_shapes=[pltpu.VMEM((k, D_MODEL), BF16)],
        compiler_params=_params("arbitrary"),
        name="proj_residual",
    )(a, w_all, x)


def _rnn_in_kernel(x_ref, g_ref, w_ref, wg_ref, xb_ref, wg16_ref, w16_ref):
    @pl.when(pl.program_id(0) == 0)
    def _():
        w16_ref[...] = w_ref[...].astype(BF16)

    hn = _rms_normed(x_ref[...], g_ref[...]).astype(BF16)
    xb_ref[...] = jnp.dot(hn, w16_ref[...], preferred_element_type=F32)
    wg16_ref[...] = wg_ref[...].astype(BF16)


def _rnn_in_proj(x, g, w_all, layer):
    s = x.shape[0]
    steps = s // TM
    gate_tn = D_RNN // steps
    assert gate_tn * steps == D_RNN and gate_tn % LANES == 0
    return pl.pallas_call(
        _rnn_in_kernel,
        grid=(steps,),
        in_specs=[
            pl.BlockSpec((TM, D_MODEL), lambda i: (i, 0)),
            pl.BlockSpec((1, D_MODEL), lambda i: (0, 0)),
            pl.BlockSpec((None, D_MODEL, D_RNN), lambda i: (layer, 0, 0), pipeline_mode=pl.Buffered(1)),
            pl.BlockSpec((None, D_MODEL, gate_tn), lambda i: (layer, 0, steps + i)),
        ],
        out_specs=[pl.BlockSpec((TM, D_RNN), lambda i: (i, 0)),
                   pl.BlockSpec((D_MODEL, gate_tn), lambda i: (0, i))],
        out_shape=[jax.ShapeDtypeStruct((s, D_RNN), F32),
                   jax.ShapeDtypeStruct((D_MODEL, D_RNN), BF16)],
        scratch_shapes=[pltpu.VMEM((D_MODEL, D_RNN), BF16)],
        compiler_params=_params("arbitrary"),
        name="rnn_in_proj",
    )(x, g, w_all, w_all)


def _softplus(z):
    return jnp.maximum(z, 0.0) + jnp.log1p(jnp.exp(-jnp.abs(z)))


def _lru_inputs(d, cur_ref, prev_ref, next_ref, first, last, cw_ref, cb_ref, wa_ref, ba_ref, wi_ref,
                bi_ref, lam_ref, ext_ref, a_ref, u_ref):
    ext_ref[d, 0:SUBLANES, :] = jnp.where(first, 0.0, prev_ref[...])
    ext_ref[d, SUBLANES:SUBLANES + TT, :] = cur_ref[...]
    ext_ref[d, SUBLANES + TT:2 * SUBLANES + TT, :] = jnp.where(last, 0.0, next_ref[...])
    ext = ext_ref[d]
    xc = cb_ref[...]
    for k in range(CONV_W):
        shift = (CONV_LEFT - k) % ext.shape[0]
        tap = ext if shift == 0 else pltpu.roll(ext, shift, 0)
        xc = xc + cw_ref[k:k + 1, :] * tap[SUBLANES:SUBLANES + TT, :]
    xb16 = xc.astype(BF16)
    ta = jnp.tanh(jnp.dot(xb16, wa_ref[d, 0], preferred_element_type=F32) + 0.5 * ba_ref[d:d + 1, :])
    ti = jnp.tanh(jnp.dot(xb16, wi_ref[d, 0], preferred_element_type=F32) + 0.5 * bi_ref[d:d + 1, :])
    i = 0.5 * ti + 0.5
    half_c = (-0.5 * LRU_C) * _softplus(-lam_ref[d:d + 1, :])
    log_a = ta * half_c + half_c
    a = jnp.exp(log_a)
    one_minus_a2 = -jnp.tanh(log_a) * (a * a + 1.0)
    root = jnp.where(one_minus_a2 > 0.0, one_minus_a2 * lax.rsqrt(one_minus_a2), 0.0)
    u = root * (i * xc)
    for s in range(SCAN_BLOCKS):
        for l in range(LANE_SLABS):
            rows = slice(s * SCAN_ROWS, (s + 1) * SCAN_ROWS)
            lanes = slice(l * LANES, (l + 1) * LANES)
            a_ref[d, l, s * SCAN_PITCH:s * SCAN_PITCH + SCAN_ROWS, :] = a[rows, lanes]
            u_ref[d, l, s * SCAN_PITCH:s * SCAN_PITCH + SCAN_ROWS, :] = u[rows, lanes]


def _rnn_core_kernel(fc_ref, fp_ref, fn_ref, bc_ref, bp_ref, bn_ref, cw_ref, cb_ref, wa_ref, ba_ref,
                     wi_ref, bi_ref, lam_ref, hf_ref, hb_ref, ext_ref, a_ref, u_ref, hl_ref, p_ref,
                     carry_ref):
    i = pl.program_id(1)
    n = pl.num_programs(1)

    @pl.when(i == 0)
    def _():
        carry_ref[...] = jnp.zeros_like(carry_ref)

    shared = (cw_ref, cb_ref, wa_ref, ba_ref, wi_ref, bi_ref, lam_ref, ext_ref, a_ref, u_ref)
    _lru_inputs(0, fc_ref, fp_ref, fn_ref, i == 0, i == n - 1, *shared)
    _lru_inputs(1, bc_ref, bp_ref, bn_ref, i == n - 1, i == 0, *shared)

    pairs = [(d, l) for d in range(2) for l in range(LANE_SLABS)]

    def body(jj, carry):
        out = []
        for (d, l), (h, p) in zip(pairs, carry):
            j = jj if d == 0 else SCAN_ROWS - 1 - jj
            rows = pl.ds(j, SCAN_BLOCKS, stride=SCAN_PITCH)
            a = a_ref[d, l, rows, :]
            h = a * h + u_ref[d, l, rows, :]
            p = a * p
            hl_ref[d, l, rows, :] = h
            p_ref[d, l, rows, :] = p
            out.append((h, p))
        return tuple(out)

    zero = jnp.zeros((SCAN_BLOCKS, LANES), F32)
    ends = ((zero, zero + 1.0),) * len(pairs)
    for jj in range(SCAN_ROWS):
        ends = body(jj, ends)

    for (d, l), (h_end, p_end) in zip(pairs, ends):
        lanes = slice(l * LANES, (l + 1) * LANES)
        out_ref = hf_ref if d == 0 else hb_ref
        c = carry_ref[d, 0:1, lanes]
        for s in (range(SCAN_BLOCKS) if d == 0 else reversed(range(SCAN_BLOCKS))):
            blk = slice(s * SCAN_PITCH, s * SCAN_PITCH + SCAN_ROWS)
            out_ref[s * SCAN_ROWS:(s + 1) * SCAN_ROWS, lanes] = hl_ref[d, l, blk, :] + p_ref[d, l, blk, :] * c
            c = h_end[s:s + 1, :] + p_end[s:s + 1, :] * c
        carry_ref[d, :, lanes] = jnp.broadcast_to(c, (SUBLANES, LANES))


def _rnn_core(xb, conv_w, conv_b, w_a, b_a, w_i, b_i, lam):
    s = xb.shape[0]
    n = s // TT
    halo = TT // SUBLANES
    last_halo = s // SUBLANES - 1
    cur = lambda t: pl.BlockSpec((TT, RNN_BLOCK_W), lambda c, i: (t(i, n), c))
    prv = lambda t: pl.BlockSpec((SUBLANES, RNN_BLOCK_W),
                                 lambda c, i: (jnp.maximum(t(i, n) * halo - 1, 0), c))
    nxt = lambda t: pl.BlockSpec((SUBLANES, RNN_BLOCK_W),
                                 lambda c, i: (jnp.minimum((t(i, n) + 1) * halo, last_halo), c))
    fwd = lambda i, n: i
    bwd = lambda i, n: n - 1 - i
    chan = lambda rows: pl.BlockSpec((rows, RNN_BLOCK_W), lambda c, i: (0, c))
    gate_w = pl.BlockSpec((2, 1, RNN_BLOCK_W, RNN_BLOCK_W), lambda c, i: (0, c, 0, 0))
    return pl.pallas_call(
        _rnn_core_kernel,
        grid=(RNN_BLOCKS, n),
        in_specs=[cur(fwd), prv(fwd), nxt(fwd), cur(bwd), prv(bwd), nxt(bwd),
                  chan(CONV_W), chan(1), gate_w, chan(2), gate_w, chan(2), chan(2)],
        out_specs=[pl.BlockSpec((TT, RNN_BLOCK_W), lambda c, i: (i, c)),
                   pl.BlockSpec((TT, RNN_BLOCK_W), lambda c, i: (n - 1 - i, c))],
        out_shape=[jax.ShapeDtypeStruct((s, D_RNN), F32), jax.ShapeDtypeStruct((s, D_RNN), F32)],
        scratch_shapes=[
            pltpu.VMEM((2, TT + 2 * SUBLANES, RNN_BLOCK_W), F32),
            pltpu.VMEM((2, LANE_SLABS, SCAN_BLOCKS * SCAN_PITCH, LANES), F32),
            pltpu.VMEM((2, LANE_SLABS, SCAN_BLOCKS * SCAN_PITCH, LANES), F32),
            pltpu.VMEM((2, LANE_SLABS, SCAN_BLOCKS * SCAN_PITCH, LANES), F32),
            pltpu.VMEM((2, LANE_SLABS, SCAN_BLOCKS * SCAN_PITCH, LANES), F32),
            pltpu.VMEM((2, SUBLANES, RNN_BLOCK_W), F32),
        ],
        compiler_params=_params("arbitrary", "arbitrary"),
        name="rnn_core",
    )(xb, xb, xb, xb, xb, xb, conv_w, conv_b, w_a, b_a, w_i, b_i, lam)


def _rnn_out_kernel(x_ref, g_ref, hf_ref, hb_ref, wy_ref, wo_ref, o_ref, wo16_ref):
    @pl.when(pl.program_id(0) == 0)
    def _():
        wo16_ref[...] = wo_ref[...].astype(BF16)

    x = x_ref[...]
    hn = _rms_normed(x, g_ref[...]).astype(BF16)
    gate = jax.nn.gelu(jnp.dot(hn, wy_ref[...], preferred_element_type=F32), approximate=True)
    a = ((hf_ref[...] + hb_ref[...]) * gate).astype(BF16)
    o_ref[...] = x + jnp.dot(a, wo16_ref[...], preferred_element_type=F32)


def _rnn_out_proj(x, g, hf, hb, w_gate, w_out_all, layer):
    s = x.shape[0]
    row = pl.BlockSpec((TM_OUT, D_MODEL), lambda i: (i, 0))
    return pl.pallas_call(
        _rnn_out_kernel,
        grid=(s // TM_OUT,),
        in_specs=[row,
                  pl.BlockSpec((1, D_MODEL), lambda i: (0, 0)),
                  row, row,
                  pl.BlockSpec((D_MODEL, D_RNN), lambda i: (0, 0), pipeline_mode=pl.Buffered(1)),
                  pl.BlockSpec((None, D_RNN, D_MODEL), lambda i: (layer, 0, 0),
                               pipeline_mode=pl.Buffered(1))],
        out_specs=row,
        out_shape=jax.ShapeDtypeStruct((s, D_MODEL), F32),
        scratch_shapes=[pltpu.VMEM((D_RNN, D_MODEL), BF16)],
        compiler_params=_params("arbitrary"),
        name="rnn_out_proj",
    )(x, g, hf, hb, w_gate, w_out_all)


def _ffn_kernel(x_ref, g_ref, wg_ref, wu_ref, wd_ref, o_ref, hn_ref):
    @pl.when(pl.program_id(1) == 0)
    def _():
        x = x_ref[...]
        hn_ref[...] = _rms_normed(x, g_ref[...]).astype(BF16)
        o_ref[...] = x

    hn = hn_ref[...]
    gate = jnp.dot(hn, wg_ref[...].astype(BF16), preferred_element_type=F32)
    up = jnp.dot(hn, wu_ref[...].astype(BF16), preferred_element_type=F32)
    act = (jax.nn.silu(gate) * up).astype(BF16)
    o_ref[...] += jnp.dot(act, wd_ref[...].astype(BF16), preferred_element_type=F32)


def _ffn(x, g, w_gate, w_up, w_down, layer):
    s = x.shape[0]
    return pl.pallas_call(
        _ffn_kernel,
        grid=(s // TM_FFN, D_FF // TF),
        in_specs=[
            pl.BlockSpec((TM_FFN, D_MODEL), lambda i, f: (i, 0)),
            pl.BlockSpec((1, D_MODEL), lambda i, f: (0, 0)),
            pl.BlockSpec((None, D_MODEL, TF), lambda i, f: (layer, 0, f)),
            pl.BlockSpec((None, D_MODEL, TF), lambda i, f: (layer, 0, f)),
            pl.BlockSpec((None, TF, D_MODEL), lambda i, f: (layer, f, 0)),
        ],
        out_specs=pl.BlockSpec((TM_FFN, D_MODEL), lambda i, f: (i, 0)),
        out_shape=jax.ShapeDtypeStruct((s, D_MODEL), F32),
        scratch_shapes=[pltpu.VMEM((TM_FFN, D_MODEL), BF16)],
        compiler_params=_params("arbitrary", "arbitrary"),
        name="ffn",
    )(x, g, w_gate, w_up, w_down)


def _rope_tables(seq_len):
    freqs = ROPE_THETA ** (-jnp.arange(ROPE_FREQS, dtype=F32) / ROPE_FREQS)
    token = jnp.arange(seq_len, dtype=jnp.int32)[:, None]
    lane = jnp.arange(HEAD_DIM, dtype=jnp.int32)[None, :]
    pos = jnp.where(lane < 2 * ROPE_FREQS, token // GRID_W, token % GRID_W).astype(F32)
    ang = pos * jnp.tile(freqs, HEAD_DIM // ROPE_FREQS)[None, :]
    sign = jnp.where(lane % (2 * ROPE_FREQS) < ROPE_FREQS, -1.0, 1.0).astype(F32)
    return jnp.cos(ang), sign * jnp.sin(ang)


def kernel(x, norm_mix, norm_ffn, attn_w_qkv, attn_q_gain, attn_k_gain, attn_w_o, rnn_w_in, rnn_conv_w, rnn_conv_b, rnn_w_a, rnn_b_a, rnn_w_i, rnn_b_i, rnn_lambda, rnn_w_out, ffn_w_gate, ffn_w_up, ffn_w_down):
    b, s, d = x.shape
    assert (b, d) == (1, D_MODEL) and s % max(TM, TQ, TK, TT) == 0
    cos_t, sin_t = _rope_tables(s)
    w_a, w_i = (0.5 * rnn_w_a).astype(BF16), (0.5 * rnn_w_i).astype(BF16)
    h = x.reshape(s, d)
    for i in range(DEPTH):
        j = i // N_MIXERS
        g_mix = norm_mix[i].reshape(1, d)
        if i % N_MIXERS == 0:
            qkv = _qkv_proj(h, g_mix, attn_w_qkv, j,
                            attn_q_gain[j].reshape(1, HEAD_DIM), attn_k_gain[j].reshape(1, HEAD_DIM),
                            cos_t, sin_t)
            o = _attention(qkv)
            h = _proj_residual(o, attn_w_o, j, h)
        else:
            xb, w_gate_rnn = _rnn_in_proj(h, g_mix, rnn_w_in, j)
            hf, hb = _rnn_core(xb, rnn_conv_w[j], rnn_conv_b[j].reshape(1, D_RNN),
                               w_a[j], rnn_b_a[j], w_i[j], rnn_b_i[j], rnn_lambda[j])
            h = _rnn_out_proj(h, g_mix, hf, hb, w_gate_rnn, rnn_w_out, j)
        h = _ffn(h, norm_ffn[i].reshape(1, d), ffn_w_gate, ffn_w_up, ffn_w_down, i)
    return h.reshape(b, s, d)
```

```python
import jax
import jax.numpy as jnp
from jax import lax
from jax.experimental import pallas as pl
from jax.experimental.pallas import tpu as pltpu

F32 = jnp.float32
BF16 = jnp.bfloat16

D_MODEL = 2048
DEPTH = 4
N_MIXERS = 2
GRID_W = 64
ROPE_THETA = 10000.0
HEAD_DIM = 128
N_Q_HEADS = 16
N_KV_HEADS = 4
GQA_GROUP = N_Q_HEADS // N_KV_HEADS
ROPE_FREQS = HEAD_DIM // 4
QKV_DIM = (N_Q_HEADS + 2 * N_KV_HEADS) * HEAD_DIM
D_RNN = D_MODEL
RNN_BLOCK_W = 256
RNN_BLOCKS = D_RNN // RNN_BLOCK_W
CONV_W = 4
CONV_LEFT = 2
LRU_C = 8.0
D_FF = 5632
EPS = 1e-6
Q_SCALE = HEAD_DIM ** -0.5 * 1.4426950408889634

SUBLANES = 8
BF16_SUBLANES = 16
LANES = 128
MXU_COLS = 256
VMEM_LIMIT_BYTES = 56 * 1024 * 1024

TM = 512
TM_OUT = 256
HEADS_PER_SLAB = MXU_COLS // HEAD_DIM
QKV_SLABS = QKV_DIM // MXU_COLS
TM_FFN = 1024
TF = 256
TQ = 512
TK = 1024
TT = 512
SCAN_BLOCKS = SUBLANES
SCAN_ROWS = TT // SCAN_BLOCKS
SCAN_PITCH = SCAN_ROWS + SUBLANES
LANE_SLABS = RNN_BLOCK_W // LANES
VT_ROWS = HEAD_DIM + BF16_SUBLANES
EXP_ROWS = 32

def _params(*sem):
    return pltpu.CompilerParams(dimension_semantics=sem, vmem_limit_bytes=VMEM_LIMIT_BYTES)


def _rms_normed(x, g):
    ms = jnp.mean(x * x, axis=-1, keepdims=True)
    return x * lax.rsqrt(ms + EPS) * g


def _qkv_kernel(x_ref, g_ref, w_ref, qg_ref, kg_ref, cos_ref, sin_ref, o_ref, hn_ref, ya_ref, yb_ref):
    hn_ref[...] = _rms_normed(x_ref[...], g_ref[...]).astype(BF16)
    q_gain = qg_ref[...] * Q_SCALE
    lane = lax.broadcasted_iota(jnp.int32, (TM, HEAD_DIM), 1)
    first_half = (lane & (2 * ROPE_FREQS - 1)) < ROPE_FREQS

    y_refs = (ya_ref, yb_ref)

    def matmul(s):
        w = w_ref[:, s * MXU_COLS:(s + 1) * MXU_COLS].astype(BF16)
        y_refs[s % 2][...] = jnp.dot(hn_ref[...], w, preferred_element_type=F32)

    def epilogue(s):
        for h in range(s * HEADS_PER_SLAB, (s + 1) * HEADS_PER_SLAB):
            lanes = slice((h % HEADS_PER_SLAB) * HEAD_DIM, (h % HEADS_PER_SLAB + 1) * HEAD_DIM)
            yh = y_refs[s % 2][:, lanes]
            if h >= N_Q_HEADS + N_KV_HEADS:
                out = yh
            else:
                gain = q_gain if h < N_Q_HEADS else kg_ref[...]
                yn = yh * lax.rsqrt(jnp.mean(yh * yh, axis=-1, keepdims=True) + EPS) * gain
                yn16 = yn.astype(BF16)
                partner = jnp.where(first_half,
                                    pltpu.roll(yn16, HEAD_DIM - ROPE_FREQS, 1),
                                    pltpu.roll(yn16, ROPE_FREQS, 1)).astype(F32)
                out = yn * cos_ref[...] + partner * sin_ref[...]
            o_ref[:, h * HEAD_DIM:(h + 1) * HEAD_DIM] = out.astype(BF16)

    matmul(0)
    for s in range(1, QKV_SLABS):
        epilogue(s - 1)
        matmul(s)
    epilogue(QKV_SLABS - 1)


def _qkv_proj(x, g, w_all, layer, q_gain, k_gain, cos_t, sin_t):
    s = x.shape[0]
    return pl.pallas_call(
        _qkv_kernel,
        grid=(s // TM,),
        in_specs=[
            pl.BlockSpec((TM, D_MODEL), lambda i: (i, 0)),
            pl.BlockSpec((1, D_MODEL), lambda i: (0, 0)),
            pl.BlockSpec((None, D_MODEL, QKV_DIM), lambda i: (layer, 0, 0),
                         pipeline_mode=pl.Buffered(1)),
            pl.BlockSpec((1, HEAD_DIM), lambda i: (0, 0)),
            pl.BlockSpec((1, HEAD_DIM), lambda i: (0, 0)),
            pl.BlockSpec((TM, HEAD_DIM), lambda i: (i, 0)),
            pl.BlockSpec((TM, HEAD_DIM), lambda i: (i, 0)),
        ],
        out_specs=pl.BlockSpec((TM, QKV_DIM), lambda i: (i, 0)),
        out_shape=jax.ShapeDtypeStruct((s, QKV_DIM), BF16),
        scratch_shapes=[pltpu.VMEM((TM, D_MODEL), BF16),
                        pltpu.VMEM((TM, MXU_COLS), F32),
                        pltpu.VMEM((TM, MXU_COLS), F32)],
        compiler_params=_params("arbitrary"),
        name="qkv_proj",
    )(x, g, w_all, q_gain, k_gain, cos_t, sin_t)


def _attn_kernel(q_ref, k_ref, v_ref, o_ref, vt_ref, acc_ref, sa_ref, sb_ref, pa_ref, pb_ref, state_ref):
    n_chunks = k_ref.shape[0] // TK
    i = pl.program_id(1)
    last_tile = pl.num_programs(1) - 1
    heads = range(GQA_GROUP)

    def score_stage(u, st_ref):
        k0 = (u % n_chunks) * TK
        k0 = k0 if isinstance(k0, int) else pl.multiple_of(k0, TK)
        q0 = pl.multiple_of(jnp.minimum(u // n_chunks, last_tile) * TQ, TQ)
        kc = k_ref[pl.ds(k0, TK), :]
        chunk_max = []
        for g in heads:
            q = q_ref[pl.ds(q0, TQ), g * HEAD_DIM:(g + 1) * HEAD_DIM]
            st = lax.dot_general(kc, q, (((1,), (1,)), ((), ())), preferred_element_type=F32)
            st_ref[g] = st
            chunk_max.append(jnp.max(st, axis=0, keepdims=True))
        return tuple(chunk_max)

    def exp_stage(u, st_ref, pt_ref, ms, chunk_max):
        first = u % n_chunks == 0
        new_ms, alphas = [], []
        for g in heads:
            m_old = jnp.where(first, -jnp.inf, ms[g])
            m_new = jnp.maximum(m_old, chunk_max[g])
            for r in range(0, TK, EXP_ROWS):
                pt_ref[g, r:r + EXP_ROWS, :] = jnp.exp2(st_ref[g, r:r + EXP_ROWS, :] - m_new).astype(BF16)
            new_ms.append(m_new)
            alphas.append(jnp.exp2(m_old - m_new))
        return tuple(new_ms), tuple(alphas)

    def out_stage(c, pt_ref, alphas):
        for g in heads:
            acc_ref[g] = alphas[g] * acc_ref[g] + jnp.dot(vt_ref[c], pt_ref[g], preferred_element_type=F32)

    def save_state(ms, alphas, chunk_max):
        for k, rows in enumerate((ms, alphas, chunk_max)):
            for g in heads:
                state_ref[k, g] = rows[g]

    @pl.when(i == 0)
    def _():
        for c in range(n_chunks):
            vc = v_ref[c * TK:(c + 1) * TK, :].astype(F32)
            vt_ref[c, :HEAD_DIM, :] = vc.T.astype(BF16)
            vt_ref[c, HEAD_DIM:, :] = jnp.ones((VT_ROWS - HEAD_DIM, TK), BF16)
        acc_ref[...] = jnp.zeros_like(acc_ref)
        ms = (jnp.full((1, TQ), -jnp.inf, F32),) * GQA_GROUP
        chunk_max = score_stage(0, sa_ref)
        ms, alphas = exp_stage(0, sa_ref, pa_ref, ms, chunk_max)
        save_state(ms, alphas, score_stage(1, sb_ref))

    def trip(c, carry, s_cur, p_cur, s_nxt, p_nxt):
        u = i * n_chunks + c
        ms, alphas, chunk_max = carry
        out_stage(c, p_cur, alphas)
        ms, alphas = exp_stage(u + 1, s_nxt, p_nxt, ms, chunk_max)
        chunk_max = score_stage(u + 2, s_cur)
        return ms, alphas, chunk_max

    def step(c, carry):
        return lax.cond(c % 2 == 0,
                        lambda cr: trip(c, cr, sa_ref, pa_ref, sb_ref, pb_ref),
                        lambda cr: trip(c, cr, sb_ref, pb_ref, sa_ref, pa_ref),
                        carry)

    carry = tuple(tuple(state_ref[k, g] for g in heads) for k in range(3))
    save_state(*lax.fori_loop(0, n_chunks, step, carry))
    for g in heads:
        acc = acc_ref[g]
        out_t = acc[:HEAD_DIM, :] / acc[HEAD_DIM:HEAD_DIM + 1, :]
        o_ref[:, g * HEAD_DIM:(g + 1) * HEAD_DIM] = out_t.T.astype(BF16)


def _attention(qkv):
    s = qkv.shape[0]
    assert (s // TK) % 2 == 0, "the a / b buffer parity of the attention pipeline needs an even chunk count"
    group_w = GQA_GROUP * HEAD_DIM
    return pl.pallas_call(
        _attn_kernel,
        grid=(N_KV_HEADS, s // TQ),
        in_specs=[
            pl.BlockSpec((s, group_w), lambda h, i: (0, h), pipeline_mode=pl.Buffered(1)),
            pl.BlockSpec((s, HEAD_DIM), lambda h, i: (0, N_Q_HEADS + h)),
            pl.BlockSpec((s, HEAD_DIM), lambda h, i: (0, N_Q_HEADS + N_KV_HEADS + h)),
        ],
        out_specs=pl.BlockSpec((TQ, group_w), lambda h, i: (i, h)),
        out_shape=jax.ShapeDtypeStruct((s, N_Q_HEADS * HEAD_DIM), BF16),
        scratch_shapes=[pltpu.VMEM((s // TK, VT_ROWS, TK), BF16),
                        pltpu.VMEM((GQA_GROUP, VT_ROWS, TQ), F32),
                        pltpu.VMEM((GQA_GROUP, TK, TQ), F32),
                        pltpu.VMEM((GQA_GROUP, TK, TQ), F32),
                        pltpu.VMEM((GQA_GROUP, TK, TQ), BF16),
                        pltpu.VMEM((GQA_GROUP, TK, TQ), BF16),
                        pltpu.VMEM((3, GQA_GROUP, 1, TQ), F32)],
        compiler_params=_params("arbitrary", "arbitrary"),
        name="attention",
    )(qkv, qkv, qkv)


def _proj_res_kernel(a_ref, w_ref, x_ref, o_ref, w16_ref):
    @pl.when(pl.program_id(0) == 0)
    def _():
        w16_ref[...] = w_ref[...].astype(BF16)

    o_ref[...] = x_ref[...] + jnp.dot(a_ref[...], w16_ref[...], preferred_element_type=F32)


def _proj_residual(a, w_all, layer, x):
    s, k = a.shape
    return pl.pallas_call(
        _proj_res_kernel,
        grid=(s // TM,),
        in_specs=[
            pl.BlockSpec((TM, k), lambda i: (i, 0)),
            pl.BlockSpec((None, k, D_MODEL), lambda i: (layer, 0, 0), pipeline_mode=pl.Buffered(1)),
            pl.BlockSpec((TM, D_MODEL), lambda i: (i, 0)),
        ],
        out_specs=pl.BlockSpec((TM, D_MODEL), lambda i: (i, 0)),
        out_shape=jax.ShapeDtypeStruct((s, D_MODEL), F32),
        scratch_shapes=[pltpu.VMEM((k, D_MODEL), BF16)],
        compiler_params=_params("arbitrary"),
        name="proj_residual",
    )(a, w_all, x)


def _rnn_in_kernel(x_ref, g_ref, w_ref, wg_ref, xb_ref, wg16_ref, w16_ref):
    @pl.when(pl.program_id(0) == 0)
    def _():
        w16_ref[...] = w_ref[...].astype(BF16)

    hn = _rms_normed(x_ref[...], g_ref[...]).astype(BF16)
    xb_ref[...] = jnp.dot(hn, w16_ref[...], preferred_element_type=F32)
    wg16_ref[...] = wg_ref[...].astype(BF16)


def _rnn_in_proj(x, g, w_all, layer):
    s = x.shape[0]
    steps = s // TM
    gate_tn = D_RNN // steps
    assert gate_tn * steps == D_RNN and gate_tn % LANES == 0
    return pl.pallas_call(
        _rnn_in_kernel,
        grid=(steps,),
        in_specs=[
            pl.BlockSpec((TM, D_MODEL), lambda i: (i, 0)),
            pl.BlockSpec((1, D_MODEL), lambda i: (0, 0)),
            pl.BlockSpec((None, D_MODEL, D_RNN), lambda i: (layer, 0, 0), pipeline_mode=pl.Buffered(1)),
            pl.BlockSpec((None, D_MODEL, gate_tn), lambda i: (layer, 0, steps + i)),
        ],
        out_specs=[pl.BlockSpec((TM, D_RNN), lambda i: (i, 0)),
                   pl.BlockSpec((D_MODEL, gate_tn), lambda i: (0, i))],
        out_shape=[jax.ShapeDtypeStruct((s, D_RNN), F32),
                   jax.ShapeDtypeStruct((D_MODEL, D_RNN), BF16)],
        scratch_shapes=[pltpu.VMEM((D_MODEL, D_RNN), BF16)],
        compiler_params=_params("arbitrary"),
        name="rnn_in_proj",
    )(x, g, w_all, w_all)


def _softplus(z):
    return jnp.maximum(z, 0.0) + jnp.log1p(jnp.exp(-jnp.abs(z)))


def _lru_inputs(d, cur_ref, prev_ref, next_ref, first, last, cw_ref, cb_ref, wa_ref, ba_ref, wi_ref,
                bi_ref, lam_ref, ext_ref, a_ref, u_ref):
    ext_ref[d, 0:SUBLANES, :] = jnp.where(first, 0.0, prev_ref[...])
    ext_ref[d, SUBLANES:SUBLANES + TT, :] = cur_ref[...]
    ext_ref[d, SUBLANES + TT:2 * SUBLANES + TT, :] = jnp.where(last, 0.0, next_ref[...])
    ext = ext_ref[d]
    xc = cb_ref[...]
    for k in range(CONV_W):
        shift = (CONV_LEFT - k) % ext.shape[0]
        tap = ext if shift == 0 else pltpu.roll(ext, shift, 0)
        xc = xc + cw_ref[k:k + 1, :] * tap[SUBLANES:SUBLANES + TT, :]
    xb16 = xc.astype(BF16)
    ta = jnp.tanh(jnp.dot(xb16, wa_ref[d, 0], preferred_element_type=F32) + 0.5 * ba_ref[d:d + 1, :])
    ti = jnp.tanh(jnp.dot(xb16, wi_ref[d, 0], preferred_element_type=F32) + 0.5 * bi_ref[d:d + 1, :])
    i = 0.5 * ti + 0.5
    half_c = (-0.5 * LRU_C) * _softplus(-lam_ref[d:d + 1, :])
    log_a = ta * half_c + half_c
    a = jnp.exp(log_a)
    one_minus_a2 = -jnp.tanh(log_a) * (a * a + 1.0)
    root = jnp.where(one_minus_a2 > 0.0, one_minus_a2 * lax.rsqrt(one_minus_a2), 0.0)
    u = root * (i * xc)
    for s in range(SCAN_BLOCKS):
        for l in range(LANE_SLABS):
            rows = slice(s * SCAN_ROWS, (s + 1) * SCAN_ROWS)
            lanes = slice(l * LANES, (l + 1) * LANES)
            a_ref[d, l, s * SCAN_PITCH:s * SCAN_PITCH + SCAN_ROWS, :] = a[rows, lanes]
            u_ref[d, l, s * SCAN_PITCH:s * SCAN_PITCH + SCAN_ROWS, :] = u[rows, lanes]


def _rnn_core_kernel(fc_ref, fp_ref, fn_ref, bc_ref, bp_ref, bn_ref, cw_ref, cb_ref, wa_ref, ba_ref,
                     wi_ref, bi_ref, lam_ref, hf_ref, hb_ref, ext_ref, a_ref, u_ref, hl_ref, p_ref,
                     carry_ref):
    i = pl.program_id(1)
    n = pl.num_programs(1)

    @pl.when(i == 0)
    def _():
        carry_ref[...] = jnp.zeros_like(carry_ref)

    shared = (cw_ref, cb_ref, wa_ref, ba_ref, wi_ref, bi_ref, lam_ref, ext_ref, a_ref, u_ref)
    _lru_inputs(0, fc_ref, fp_ref, fn_ref, i == 0, i == n - 1, *shared)
    _lru_inputs(1, bc_ref, bp_ref, bn_ref, i == n - 1, i == 0, *shared)

    pairs = [(d, l) for d in range(2) for l in range(LANE_SLABS)]

    def body(jj, carry):
        out = []
        for (d, l), (h, p) in zip(pairs, carry):
            j = jj if d == 0 else SCAN_ROWS - 1 - jj
            rows = pl.ds(j, SCAN_BLOCKS, stride=SCAN_PITCH)
            a = a_ref[d, l, rows, :]
            h = a * h + u_ref[d, l, rows, :]
            p = a * p
            hl_ref[d, l, rows, :] = h
            p_ref[d, l, rows, :] = p
            out.append((h, p))
        return tuple(out)

    zero = jnp.zeros((SCAN_BLOCKS, LANES), F32)
    ends = ((zero, zero + 1.0),) * len(pairs)
    for jj in range(SCAN_ROWS):
        ends = body(jj, ends)

    for (d, l), (h_end, p_end) in zip(pairs, ends):
        lanes = slice(l * LANES, (l + 1) * LANES)
        out_ref = hf_ref if d == 0 else hb_ref
        c = carry_ref[d, 0:1, lanes]
        for s in (range(SCAN_BLOCKS) if d == 0 else reversed(range(SCAN_BLOCKS))):
            blk = slice(s * SCAN_PITCH, s * SCAN_PITCH + SCAN_ROWS)
            out_ref[s * SCAN_ROWS:(s + 1) * SCAN_ROWS, lanes] = hl_ref[d, l, blk, :] + p_ref[d, l, blk, :] * c
            c = h_end[s:s + 1, :] + p_end[s:s + 1, :] * c
        carry_ref[d, :, lanes] = jnp.broadcast_to(c, (SUBLANES, LANES))


def _rnn_core(xb, conv_w, conv_b, w_a, b_a, w_i, b_i, lam):
    s = xb.shape[0]
    n = s // TT
    halo = TT // SUBLANES
    last_halo = s // SUBLANES - 1
    cur = lambda t: pl.BlockSpec((TT, RNN_BLOCK_W), lambda c, i: (t(i, n), c))
    prv = lambda t: pl.BlockSpec((SUBLANES, RNN_BLOCK_W),
                                 lambda c, i: (jnp.maximum(t(i, n) * halo - 1, 0), c))
    nxt = lambda t: pl.BlockSpec((SUBLANES, RNN_BLOCK_W),
                                 lambda c, i: (jnp.minimum((t(i, n) + 1) * halo, last_halo), c))
    fwd = lambda i, n: i
    bwd = lambda i, n: n - 1 - i
    chan = lambda rows: pl.BlockSpec((rows, RNN_BLOCK_W), lambda c, i: (0, c))
    gate_w = pl.BlockSpec((2, 1, RNN_BLOCK_W, RNN_BLOCK_W), lambda c, i: (0, c, 0, 0))
    return pl.pallas_call(
        _rnn_core_kernel,
        grid=(RNN_BLOCKS, n),
        in_specs=[cur(fwd), prv(fwd), nxt(fwd), cur(bwd), prv(bwd), nxt(bwd),
                  chan(CONV_W), chan(1), gate_w, chan(2), gate_w, chan(2), chan(2)],
        out_specs=[pl.BlockSpec((TT, RNN_BLOCK_W), lambda c, i: (i, c)),
                   pl.BlockSpec((TT, RNN_BLOCK_W), lambda c, i: (n - 1 - i, c))],
        out_shape=[jax.ShapeDtypeStruct((s, D_RNN), F32), jax.ShapeDtypeStruct((s, D_RNN), F32)],
        scratch_shapes=[
            pltpu.VMEM((2, TT + 2 * SUBLANES, RNN_BLOCK_W), F32),
            pltpu.VMEM((2, LANE_SLABS, SCAN_BLOCKS * SCAN_PITCH, LANES), F32),
            pltpu.VMEM((2, LANE_SLABS, SCAN_BLOCKS * SCAN_PITCH, LANES), F32),
            pltpu.VMEM((2, LANE_SLABS, SCAN_BLOCKS * SCAN_PITCH, LANES), F32),
            pltpu.VMEM((2, LANE_SLABS, SCAN_BLOCKS * SCAN_PITCH, LANES), F32),
            pltpu.VMEM((2, SUBLANES, RNN_BLOCK_W), F32),
        ],
        compiler_params=_params("arbitrary", "arbitrary"),
        name="rnn_core",
    )(xb, xb, xb, xb, xb, xb, conv_w, conv_b, w_a, b_a, w_i, b_i, lam)


def _rnn_out_kernel(x_ref, g_ref, hf_ref, hb_ref, wy_ref, wo_ref, o_ref, wo16_ref):
    @pl.when(pl.program_id(0) == 0)
    def _():
        wo16_ref[...] = wo_ref[...].astype(BF16)

    x = x_ref[...]
    hn = _rms_normed(x, g_ref[...]).astype(BF16)
    gate = jax.nn.gelu(jnp.dot(hn, wy_ref[...], preferred_element_type=F32), approximate=True)
    a = ((hf_ref[...] + hb_ref[...]) * gate).astype(BF16)
    o_ref[...] = x + jnp.dot(a, wo16_ref[...], preferred_element_type=F32)


def _rnn_out_proj(x, g, hf, hb, w_gate, w_out_all, layer):
    s = x.shape[0]
    row = pl.BlockSpec((TM_OUT, D_MODEL), lambda i: (i, 0))
    return pl.pallas_call(
        _rnn_out_kernel,
        grid=(s // TM_OUT,),
        in_specs=[row,
                  pl.BlockSpec((1, D_MODEL), lambda i: (0, 0)),
                  row, row,
                  pl.BlockSpec((D_MODEL, D_RNN), lambda i: (0, 0), pipeline_mode=pl.Buffered(1)),
                  pl.BlockSpec((None, D_RNN, D_MODEL), lambda i: (layer, 0, 0),
                               pipeline_mode=pl.Buffered(1))],
        out_specs=row,
        out_shape=jax.ShapeDtypeStruct((s, D_MODEL), F32),
        scratch_shapes=[pltpu.VMEM((D_RNN, D_MODEL), BF16)],
        compiler_params=_params("arbitrary"),
        name="rnn_out_proj",
    )(x, g, hf, hb, w_gate, w_out_all)


def _ffn_kernel(x_ref, g_ref, wg_ref, wu_ref, wd_ref, o_ref, hn_ref):
    @pl.when(pl.program_id(1) == 0)
    def _():
        x = x_ref[...]
        hn_ref[...] = _rms_normed(x, g_ref[...]).astype(BF16)
        o_ref[...] = x

    hn = hn_ref[...]
    gate = jnp.dot(hn, wg_ref[...].astype(BF16), preferred_element_type=F32)
    up = jnp.dot(hn, wu_ref[...].astype(BF16), preferred_element_type=F32)
    act = (jax.nn.silu(gate) * up).astype(BF16)
    o_ref[...] += jnp.dot(act, wd_ref[...].astype(BF16), preferred_element_type=F32)


def _ffn(x, g, w_gate, w_up, w_down, layer):
    s = x.shape[0]
    return pl.pallas_call(
        _ffn_kernel,
        grid=(s // TM_FFN, D_FF // TF),
        in_specs=[
            pl.BlockSpec((TM_FFN, D_MODEL), lambda i, f: (i, 0)),
            pl.BlockSpec((1, D_MODEL), lambda i, f: (0, 0)),
            pl.BlockSpec((None, D_MODEL, TF), lambda i, f: (layer, 0, f)),
            pl.BlockSpec((None, D_MODEL, TF), lambda i, f: (layer, 0, f)),
            pl.BlockSpec((None, TF, D_MODEL), lambda i, f: (layer, f, 0)),
        ],
        out_specs=pl.BlockSpec((TM_FFN, D_MODEL), lambda i, f: (i, 0)),
        out_shape=jax.ShapeDtypeStruct((s, D_MODEL), F32),
        scratch_shapes=[pltpu.VMEM((TM_FFN, D_MODEL), BF16)],
        compiler_params=_params("arbitrary", "arbitrary"),
        name="ffn",
    )(x, g, w_gate, w_up, w_down)


def _rope_tables(seq_len):
    freqs = ROPE_THETA ** (-jnp.arange(ROPE_FREQS, dtype=F32) / ROPE_FREQS)
    token = jnp.arange(seq_len, dtype=jnp.int32)[:, None]
    lane = jnp.arange(HEAD_DIM, dtype=jnp.int32)[None, :]
    pos = jnp.where(lane < 2 * ROPE_FREQS, token // GRID_W, token % GRID_W).astype(F32)
    ang = pos * jnp.tile(freqs, HEAD_DIM // ROPE_FREQS)[None, :]
    sign = jnp.where(lane % (2 * ROPE_FREQS) < ROPE_FREQS, -1.0, 1.0).astype(F32)
    return jnp.cos(ang), sign * jnp.sin(ang)


def kernel(x, norm_mix, norm_ffn, attn_w_qkv, attn_q_gain, attn_k_gain, attn_w_o, rnn_w_in, rnn_conv_w, rnn_conv_b, rnn_w_a, rnn_b_a, rnn_w_i, rnn_b_i, rnn_lambda, rnn_w_out, ffn_w_gate, ffn_w_up, ffn_w_down):
    b, s, d = x.shape
    assert (b, d) == (1, D_MODEL) and s % max(TM, TQ, TK, TT) == 0
    cos_t, sin_t = _rope_tables(s)
    w_a, w_i = (0.5 * rnn_w_a).astype(BF16), (0.5 * rnn_w_i).astype(BF16)
    h = x.reshape(s, d)
    for i in range(DEPTH):
        j = i // N_MIXERS
        g_mix = norm_mix[i].reshape(1, d)
        if i % N_MIXERS == 0:
            qkv = _qkv_proj(h, g_mix, attn_w_qkv, j,
                            attn_q_gain[j].reshape(1, HEAD_DIM), attn_k_gain[j].reshape(1, HEAD_DIM),
                            cos_t, sin_t)
            o = _attention(qkv)
            h = _proj_residual(o, attn_w_o, j, h)
        else:
            xb, w_gate_rnn = _rnn_in_proj(h, g_mix, rnn_w_in, j)
            hf, hb = _rnn_core(xb, rnn_conv_w[j], rnn_conv_b[j].reshape(1, D_RNN),
                               w_a[j], rnn_b_a[j], w_i[j], rnn_b_i[j], rnn_lambda[j])
            h = _rnn_out_proj(h, g_mix, hf, hb, w_gate_rnn, rnn_w_out, j)
        h = _ffn(h, norm_ffn[i].reshape(1, d), ffn_w_gate, ffn_w_up, ffn_w_down, i)
    return h.reshape(b, s, d)
```

```python
import jax
import jax.numpy as jnp
from jax import lax
from jax.experimental import pallas as pl
from jax.experimental.pallas import tpu as pltpu

F32 = jnp.float32
BF16 = jnp.bfloat16

D_MODEL = 2048
DEPTH = 4
N_MIXERS = 2
GRID_W = 64
ROPE_THETA = 10000.0
HEAD_DIM = 128
N_Q_HEADS = 16
N_KV_HEADS = 4
GQA_GROUP = N_Q_HEADS // N_KV_HEADS
ROPE_FREQS = HEAD_DIM // 4
QKV_DIM = (N_Q_HEADS + 2 * N_KV_HEADS) * HEAD_DIM
D_RNN = D_MODEL
RNN_BLOCK_W = 256
RNN_BLOCKS = D_RNN // RNN_BLOCK_W
CONV_W = 4
CONV_LEFT = 2
LRU_C = 8.0
D_FF = 5632
EPS = 1e-6
Q_SCALE = HEAD_DIM ** -0.5 * 1.4426950408889634

SUBLANES = 8
BF16_SUBLANES = 16
LANES = 128
MXU_COLS = 256
VMEM_LIMIT_BYTES = 56 * 1024 * 1024

TM = 512
TM_OUT = 256
HEADS_PER_SLAB = MXU_COLS // HEAD_DIM
QKV_SLABS = QKV_DIM // MXU_COLS
TM_FFN = 1024
TF = 256
FFN_TILES = D_FF // TF
TQ = 512
TK = 1024
TT = 512
SCAN_BLOCKS = SUBLANES
SCAN_ROWS = TT // SCAN_BLOCKS
SCAN_PITCH = SCAN_ROWS + SUBLANES
LANE_SLABS = RNN_BLOCK_W // LANES
VT_ROWS = HEAD_DIM + BF16_SUBLANES
EXP_ROWS = 32

def _params(*sem):
    return pltpu.CompilerParams(dimension_semantics=sem, vmem_limit_bytes=VMEM_LIMIT_BYTES)


def _rms_normed(x, g):
    ms = jnp.mean(x * x, axis=-1, keepdims=True)
    return x * lax.rsqrt(ms + EPS) * g


def _qkv_kernel(x_ref, g_ref, w_ref, qg_ref, kg_ref, cos_ref, sin_ref, o_ref, hn_ref, ya_ref, yb_ref):
    hn_ref[...] = _rms_normed(x_ref[...], g_ref[...]).astype(BF16)
    q_gain = qg_ref[...] * Q_SCALE
    lane = lax.broadcasted_iota(jnp.int32, (TM, HEAD_DIM), 1)
    first_half = (lane & (2 * ROPE_FREQS - 1)) < ROPE_FREQS

    y_refs = (ya_ref, yb_ref)

    def matmul(s):
        w = w_ref[:, s * MXU_COLS:(s + 1) * MXU_COLS].astype(BF16)
        y_refs[s % 2][...] = jnp.dot(hn_ref[...], w, preferred_element_type=F32)

    def epilogue(s):
        for h in range(s * HEADS_PER_SLAB, (s + 1) * HEADS_PER_SLAB):
            lanes = slice((h % HEADS_PER_SLAB) * HEAD_DIM, (h % HEADS_PER_SLAB + 1) * HEAD_DIM)
            yh = y_refs[s % 2][:, lanes]
            if h >= N_Q_HEADS + N_KV_HEADS:
                out = yh
            else:
                gain = q_gain if h < N_Q_HEADS else kg_ref[...]
                yn = yh * lax.rsqrt(jnp.mean(yh * yh, axis=-1, keepdims=True) + EPS) * gain
                yn16 = yn.astype(BF16)
                partner = jnp.where(first_half,
                                    pltpu.roll(yn16, HEAD_DIM - ROPE_FREQS, 1),
                                    pltpu.roll(yn16, ROPE_FREQS, 1)).astype(F32)
                out = yn * cos_ref[...] + partner * sin_ref[...]
            o_ref[:, h * HEAD_DIM:(h + 1) * HEAD_DIM] = out.astype(BF16)

    matmul(0)
    for s in range(1, QKV_SLABS):
        epilogue(s - 1)
        matmul(s)
    epilogue(QKV_SLABS - 1)


def _qkv_proj(x, g, w_all, layer, q_gain, k_gain, cos_t, sin_t):
    s = x.shape[0]
    return pl.pallas_call(
        _qkv_kernel,
        grid=(s // TM,),
        in_specs=[
            pl.BlockSpec((TM, D_MODEL), lambda i: (i, 0)),
            pl.BlockSpec((1, D_MODEL), lambda i: (0, 0)),
            pl.BlockSpec((None, D_MODEL, QKV_DIM), lambda i: (layer, 0, 0),
                         pipeline_mode=pl.Buffered(1)),
            pl.BlockSpec((1, HEAD_DIM), lambda i: (0, 0)),
            pl.BlockSpec((1, HEAD_DIM), lambda i: (0, 0)),
            pl.BlockSpec((TM, HEAD_DIM), lambda i: (i, 0)),
            pl.BlockSpec((TM, HEAD_DIM), lambda i: (i, 0)),
        ],
        out_specs=pl.BlockSpec((TM, QKV_DIM), lambda i: (i, 0)),
        out_shape=jax.ShapeDtypeStruct((s, QKV_DIM), BF16),
        scratch_shapes=[pltpu.VMEM((TM, D_MODEL), BF16),
                        pltpu.VMEM((TM, MXU_COLS), F32),
                        pltpu.VMEM((TM, MXU_COLS), F32)],
        compiler_params=_params("arbitrary"),
        name="qkv_proj",
    )(x, g, w_all, q_gain, k_gain, cos_t, sin_t)


def _attn_kernel(q_ref, k_ref, v_ref, o_ref, vt_ref, acc_ref, sa_ref, sb_ref, pa_ref, pb_ref, state_ref):
    n_chunks = k_ref.shape[0] // TK
    i = pl.program_id(1)
    last_tile = pl.num_programs(1) - 1
    heads = range(GQA_GROUP)

    def score_stage(u, st_ref):
        k0 = (u % n_chunks) * TK
        k0 = k0 if isinstance(k0, int) else pl.multiple_of(k0, TK)
        q0 = pl.multiple_of(jnp.minimum(u // n_chunks, last_tile) * TQ, TQ)
        kc = k_ref[pl.ds(k0, TK), :]
        chunk_max = []
        for g in heads:
            q = q_ref[pl.ds(q0, TQ), g * HEAD_DIM:(g + 1) * HEAD_DIM]
            st = lax.dot_general(kc, q, (((1,), (1,)), ((), ())), preferred_element_type=F32)
            st_ref[g] = st
            chunk_max.append(jnp.max(st, axis=0, keepdims=True))
        return tuple(chunk_max)

    def exp_stage(u, st_ref, pt_ref, ms, chunk_max):
        first = u % n_chunks == 0
        new_ms, alphas = [], []
        for g in heads:
            m_old = jnp.where(first, -jnp.inf, ms[g])
            m_new = jnp.maximum(m_old, chunk_max[g])
            for r in range(0, TK, EXP_ROWS):
                pt_ref[g, r:r + EXP_ROWS, :] = jnp.exp2(st_ref[g, r:r + EXP_ROWS, :] - m_new).astype(BF16)
            new_ms.append(m_new)
            alphas.append(jnp.exp2(m_old - m_new))
        return tuple(new_ms), tuple(alphas)

    def out_stage(c, pt_ref, alphas):
        for g in heads:
            acc_ref[g] = alphas[g] * acc_ref[g] + jnp.dot(vt_ref[c], pt_ref[g], preferred_element_type=F32)

    def save_state(ms, alphas, chunk_max):
        for k, rows in enumerate((ms, alphas, chunk_max)):
            for g in heads:
                state_ref[k, g] = rows[g]

    @pl.when(i == 0)
    def _():
        for c in range(n_chunks):
            vc = v_ref[c * TK:(c + 1) * TK, :].astype(F32)
            vt_ref[c, :HEAD_DIM, :] = vc.T.astype(BF16)
            vt_ref[c, HEAD_DIM:, :] = jnp.ones((VT_ROWS - HEAD_DIM, TK), BF16)
        acc_ref[...] = jnp.zeros_like(acc_ref)
        ms = (jnp.full((1, TQ), -jnp.inf, F32),) * GQA_GROUP
        chunk_max = score_stage(0, sa_ref)
        ms, alphas = exp_stage(0, sa_ref, pa_ref, ms, chunk_max)
        save_state(ms, alphas, score_stage(1, sb_ref))

    def trip(c, carry, s_cur, p_cur, s_nxt, p_nxt):
        u = i * n_chunks + c
        ms, alphas, chunk_max = carry
        out_stage(c, p_cur, alphas)
        ms, alphas = exp_stage(u + 1, s_nxt, p_nxt, ms, chunk_max)
        chunk_max = score_stage(u + 2, s_cur)
        return ms, alphas, chunk_max

    def step(c, carry):
        return lax.cond(c % 2 == 0,
                        lambda cr: trip(c, cr, sa_ref, pa_ref, sb_ref, pb_ref),
                        lambda cr: trip(c, cr, sb_ref, pb_ref, sa_ref, pa_ref),
                        carry)

    carry = tuple(tuple(state_ref[k, g] for g in heads) for k in range(3))
    save_state(*lax.fori_loop(0, n_chunks, step, carry))
    for g in heads:
        acc = acc_ref[g]
        out_t = acc[:HEAD_DIM, :] / acc[HEAD_DIM:HEAD_DIM + 1, :]
        o_ref[:, g * HEAD_DIM:(g + 1) * HEAD_DIM] = out_t.T.astype(BF16)


def _attention(qkv):
    s = qkv.shape[0]
    assert (s // TK) % 2 == 0, "the a / b buffer parity of the attention pipeline needs an even chunk count"
    group_w = GQA_GROUP * HEAD_DIM
    return pl.pallas_call(
        _attn_kernel,
        grid=(N_KV_HEADS, s // TQ),
        in_specs=[
            pl.BlockSpec((s, group_w), lambda h, i: (0, h), pipeline_mode=pl.Buffered(1)),
            pl.BlockSpec((s, HEAD_DIM), lambda h, i: (0, N_Q_HEADS + h)),
            pl.BlockSpec((s, HEAD_DIM), lambda h, i: (0, N_Q_HEADS + N_KV_HEADS + h)),
        ],
        out_specs=pl.BlockSpec((TQ, group_w), lambda h, i: (i, h)),
        out_shape=jax.ShapeDtypeStruct((s, N_Q_HEADS * HEAD_DIM), BF16),
        scratch_shapes=[pltpu.VMEM((s // TK, VT_ROWS, TK), BF16),
                        pltpu.VMEM((GQA_GROUP, VT_ROWS, TQ), F32),
                        pltpu.VMEM((GQA_GROUP, TK, TQ), F32),
                        pltpu.VMEM((GQA_GROUP, TK, TQ), F32),
                        pltpu.VMEM((GQA_GROUP, TK, TQ), BF16),
                        pltpu.VMEM((GQA_GROUP, TK, TQ), BF16),
                        pltpu.VMEM((3, GQA_GROUP, 1, TQ), F32)],
        compiler_params=_params("arbitrary", "arbitrary"),
        name="attention",
    )(qkv, qkv, qkv)


def _proj_res_kernel(a_ref, w_ref, x_ref, o_ref, w16_ref):
    @pl.when(pl.program_id(0) == 0)
    def _():
        w16_ref[...] = w_ref[...].astype(BF16)

    o_ref[...] = x_ref[...] + jnp.dot(a_ref[...], w16_ref[...], preferred_element_type=F32)


def _proj_residual(a, w_all, layer, x):
    s, k = a.shape
    return pl.pallas_call(
        _proj_res_kernel,
        grid=(s // TM,),
        in_specs=[
            pl.BlockSpec((TM, k), lambda i: (i, 0)),
            pl.BlockSpec((None, k, D_MODEL), lambda i: (layer, 0, 0), pipeline_mode=pl.Buffered(1)),
            pl.BlockSpec((TM, D_MODEL), lambda i: (i, 0)),
        ],
        out_specs=pl.BlockSpec((TM, D_MODEL), lambda i: (i, 0)),
        out_shape=jax.ShapeDtypeStruct((s, D_MODEL), F32),
        scratch_shapes=[pltpu.VMEM((k, D_MODEL), BF16)],
        compiler_params=_params("arbitrary"),
        name="proj_residual",
    )(a, w_all, x)


def _rnn_in_kernel(x_ref, g_ref, w_ref, wg_ref, xb_ref, wg16_ref, w16_ref):
    @pl.when(pl.program_id(0) == 0)
    def _():
        w16_ref[...] = w_ref[...].astype(BF16)

    hn = _rms_normed(x_ref[...], g_ref[...]).astype(BF16)
    xb_ref[...] = jnp.dot(hn, w16_ref[...], preferred_element_type=F32)
    wg16_ref[...] = wg_ref[...].astype(BF16)


def _rnn_in_proj(x, g, w_all, layer):
    s = x.shape[0]
    steps = s // TM
    gate_tn = D_RNN // steps
    assert gate_tn * steps == D_RNN and gate_tn % LANES == 0
    return pl.pallas_call(
        _rnn_in_kernel,
        grid=(steps,),
        in_specs=[
            pl.BlockSpec((TM, D_MODEL), lambda i: (i, 0)),
            pl.BlockSpec((1, D_MODEL), lambda i: (0, 0)),
            pl.BlockSpec((None, D_MODEL, D_RNN), lambda i: (layer, 0, 0), pipeline_mode=pl.Buffered(1)),
            pl.BlockSpec((None, D_MODEL, gate_tn), lambda i: (layer, 0, steps + i)),
        ],
        out_specs=[pl.BlockSpec((TM, D_RNN), lambda i: (i, 0)),
                   pl.BlockSpec((D_MODEL, gate_tn), lambda i: (0, i))],
        out_shape=[jax.ShapeDtypeStruct((s, D_RNN), F32),
                   jax.ShapeDtypeStruct((D_MODEL, D_RNN), BF16)],
        scratch_shapes=[pltpu.VMEM((D_MODEL, D_RNN), BF16)],
        compiler_params=_params("arbitrary"),
        name="rnn_in_proj",
    )(x, g, w_all, w_all)


def _softplus(z):
    return jnp.maximum(z, 0.0) + jnp.log1p(jnp.exp(-jnp.abs(z)))


def _lru_inputs(d, cur_ref, prev_ref, next_ref, first, last, cw_ref, cb_ref, wa_ref, ba_ref, wi_ref,
                bi_ref, lam_ref, ext_ref, a_ref, u_ref):
    ext_ref[d, 0:SUBLANES, :] = jnp.where(first, 0.0, prev_ref[...])
    ext_ref[d, SUBLANES:SUBLANES + TT, :] = cur_ref[...]
    ext_ref[d, SUBLANES + TT:2 * SUBLANES + TT, :] = jnp.where(last, 0.0, next_ref[...])
    ext = ext_ref[d]
    xc = cb_ref[...]
    for k in range(CONV_W):
        shift = (CONV_LEFT - k) % ext.shape[0]
        tap = ext if shift == 0 else pltpu.roll(ext, shift, 0)
        xc = xc + cw_ref[k:k + 1, :] * tap[SUBLANES:SUBLANES + TT, :]
    xb16 = xc.astype(BF16)
    ta = jnp.tanh(jnp.dot(xb16, wa_ref[d, 0], preferred_element_type=F32) + 0.5 * ba_ref[d:d + 1, :])
    ti = jnp.tanh(jnp.dot(xb16, wi_ref[d, 0], preferred_element_type=F32) + 0.5 * bi_ref[d:d + 1, :])
    i = 0.5 * ti + 0.5
    half_c = (-0.5 * LRU_C) * _softplus(-lam_ref[d:d + 1, :])
    log_a = ta * half_c + half_c
    a = jnp.exp(log_a)
    one_minus_a2 = -jnp.tanh(log_a) * (a * a + 1.0)
    root = jnp.where(one_minus_a2 > 0.0, one_minus_a2 * lax.rsqrt(one_minus_a2), 0.0)
    u = root * (i * xc)
    for s in range(SCAN_BLOCKS):
        for l in range(LANE_SLABS):
            rows = slice(s * SCAN_ROWS, (s + 1) * SCAN_ROWS)
            lanes = slice(l * LANES, (l + 1) * LANES)
            a_ref[d, l, s * SCAN_PITCH:s * SCAN_PITCH + SCAN_ROWS, :] = a[rows, lanes]
            u_ref[d, l, s * SCAN_PITCH:s * SCAN_PITCH + SCAN_ROWS, :] = u[rows, lanes]


def _rnn_core_kernel(fc_ref, fp_ref, fn_ref, bc_ref, bp_ref, bn_ref, cw_ref, cb_ref, wa_ref, ba_ref,
                     wi_ref, bi_ref, lam_ref, hf_ref, hb_ref, ext_ref, a_ref, u_ref, hl_ref, p_ref,
                     carry_ref):
    i = pl.program_id(1)
    n = pl.num_programs(1)

    @pl.when(i == 0)
    def _():
        carry_ref[...] = jnp.zeros_like(carry_ref)

    shared = (cw_ref, cb_ref, wa_ref, ba_ref, wi_ref, bi_ref, lam_ref, ext_ref, a_ref, u_ref)
    _lru_inputs(0, fc_ref, fp_ref, fn_ref, i == 0, i == n - 1, *shared)
    _lru_inputs(1, bc_ref, bp_ref, bn_ref, i == n - 1, i == 0, *shared)

    pairs = [(d, l) for d in range(2) for l in range(LANE_SLABS)]

    def body(jj, carry):
        out = []
        for (d, l), (h, p) in zip(pairs, carry):
            j = jj if d == 0 else SCAN_ROWS - 1 - jj
            rows = pl.ds(j, SCAN_BLOCKS, stride=SCAN_PITCH)
            a = a_ref[d, l, rows, :]
            h = a * h + u_ref[d, l, rows, :]
            p = a * p
            hl_ref[d, l, rows, :] = h
            p_ref[d, l, rows, :] = p
            out.append((h, p))
        return tuple(out)

    zero = jnp.zeros((SCAN_BLOCKS, LANES), F32)
    ends = ((zero, zero + 1.0),) * len(pairs)
    for jj in range(SCAN_ROWS):
        ends = body(jj, ends)

    for (d, l), (h_end, p_end) in zip(pairs, ends):
        lanes = slice(l * LANES, (l + 1) * LANES)
        out_ref = hf_ref if d == 0 else hb_ref
        c = carry_ref[d, 0:1, lanes]
        for s in (range(SCAN_BLOCKS) if d == 0 else reversed(range(SCAN_BLOCKS))):
            blk = slice(s * SCAN_PITCH, s * SCAN_PITCH + SCAN_ROWS)
            out_ref[s * SCAN_ROWS:(s + 1) * SCAN_ROWS, lanes] = hl_ref[d, l, blk, :] + p_ref[d, l, blk, :] * c
            c = h_end[s:s + 1, :] + p_end[s:s + 1, :] * c
        carry_ref[d, :, lanes] = jnp.broadcast_to(c, (SUBLANES, LANES))


def _rnn_core(xb, conv_w, conv_b, w_a, b_a, w_i, b_i, lam):
    s = xb.shape[0]
    n = s // TT
    halo = TT // SUBLANES
    last_halo = s // SUBLANES - 1
    cur = lambda t: pl.BlockSpec((TT, RNN_BLOCK_W), lambda c, i: (t(i, n), c))
    prv = lambda t: pl.BlockSpec((SUBLANES, RNN_BLOCK_W),
                                 lambda c, i: (jnp.maximum(t(i, n) * halo - 1, 0), c))
    nxt = lambda t: pl.BlockSpec((SUBLANES, RNN_BLOCK_W),
                                 lambda c, i: (jnp.minimum((t(i, n) + 1) * halo, last_halo), c))
    fwd = lambda i, n: i
    bwd = lambda i, n: n - 1 - i
    chan = lambda rows: pl.BlockSpec((rows, RNN_BLOCK_W), lambda c, i: (0, c))
    gate_w = pl.BlockSpec((2, 1, RNN_BLOCK_W, RNN_BLOCK_W), lambda c, i: (0, c, 0, 0))
    return pl.pallas_call(
        _rnn_core_kernel,
        grid=(RNN_BLOCKS, n),
        in_specs=[cur(fwd), prv(fwd), nxt(fwd), cur(bwd), prv(bwd), nxt(bwd),
                  chan(CONV_W), chan(1), gate_w, chan(2), gate_w, chan(2), chan(2)],
        out_specs=[pl.BlockSpec((TT, RNN_BLOCK_W), lambda c, i: (i, c)),
                   pl.BlockSpec((TT, RNN_BLOCK_W), lambda c, i: (n - 1 - i, c))],
        out_shape=[jax.ShapeDtypeStruct((s, D_RNN), F32), jax.ShapeDtypeStruct((s, D_RNN), F32)],
        scratch_shapes=[
            pltpu.VMEM((2, TT + 2 * SUBLANES, RNN_BLOCK_W), F32),
            pltpu.VMEM((2, LANE_SLABS, SCAN_BLOCKS * SCAN_PITCH, LANES), F32),
            pltpu.VMEM((2, LANE_SLABS, SCAN_BLOCKS * SCAN_PITCH, LANES), F32),
            pltpu.VMEM((2, LANE_SLABS, SCAN_BLOCKS * SCAN_PITCH, LANES), F32),
            pltpu.VMEM((2, LANE_SLABS, SCAN_BLOCKS * SCAN_PITCH, LANES), F32),
            pltpu.VMEM((2, SUBLANES, RNN_BLOCK_W), F32),
        ],
        compiler_params=_params("arbitrary", "arbitrary"),
        name="rnn_core",
    )(xb, xb, xb, xb, xb, xb, conv_w, conv_b, w_a, b_a, w_i, b_i, lam)


def _rnn_out_kernel(x_ref, g_ref, hf_ref, hb_ref, wy_ref, wo_ref, o_ref, wo16_ref):
    @pl.when(pl.program_id(0) == 0)
    def _():
        wo16_ref[...] = wo_ref[...].astype(BF16)

    x = x_ref[...]
    hn = _rms_normed(x, g_ref[...]).astype(BF16)
    gate = jax.nn.gelu(jnp.dot(hn, wy_ref[...], preferred_element_type=F32), approximate=True)
    a = ((hf_ref[...] + hb_ref[...]) * gate).astype(BF16)
    o_ref[...] = x + jnp.dot(a, wo16_ref[...], preferred_element_type=F32)


def _rnn_out_proj(x, g, hf, hb, w_gate, w_out_all, layer):
    s = x.shape[0]
    row = pl.BlockSpec((TM_OUT, D_MODEL), lambda i: (i, 0))
    return pl.pallas_call(
        _rnn_out_kernel,
        grid=(s // TM_OUT,),
        in_specs=[row,
                  pl.BlockSpec((1, D_MODEL), lambda i: (0, 0)),
                  row, row,
                  pl.BlockSpec((D_MODEL, D_RNN), lambda i: (0, 0), pipeline_mode=pl.Buffered(1)),
                  pl.BlockSpec((None, D_RNN, D_MODEL), lambda i: (layer, 0, 0),
                               pipeline_mode=pl.Buffered(1))],
        out_specs=row,
        out_shape=jax.ShapeDtypeStruct((s, D_MODEL), F32),
        scratch_shapes=[pltpu.VMEM((D_RNN, D_MODEL), BF16)],
        compiler_params=_params("arbitrary"),
        name="rnn_out_proj",
    )(x, g, hf, hb, w_gate, w_out_all)


def _ffn_kernel(x_ref, g_ref, wg_ref, wu_ref, wd_ref, o_ref, hn_ref, acta_ref, actb_ref):
    u = pl.program_id(0)
    f = u % FFN_TILES

    @pl.when(f == 0)
    def _():
        hn_ref[...] = _rms_normed(x_ref[...], g_ref[...]).astype(BF16)

    @pl.when(u == 0)
    def _():
        actb_ref[...] = jnp.zeros_like(actb_ref)
        o_ref[...] = x_ref[...]

    def trip(act_cur, act_prev):
        base = jnp.where(f == 1, x_ref[...], o_ref[...])
        o_ref[...] = base + jnp.dot(act_prev[...], wd_ref[...].astype(BF16), preferred_element_type=F32)
        hn = hn_ref[...]
        gate = jnp.dot(hn, wg_ref[...].astype(BF16), preferred_element_type=F32)
        up = jnp.dot(hn, wu_ref[...].astype(BF16), preferred_element_type=F32)
        act_cur[...] = (jax.nn.silu(gate) * up).astype(BF16)

    lax.cond(u % 2 == 0, lambda: trip(acta_ref, actb_ref), lambda: trip(actb_ref, acta_ref))


def _ffn(x, g, w_gate, w_up, w_down, layer):
    s = x.shape[0]
    last_unit = (s // TM_FFN) * FFN_TILES - 1
    unit = lambda u: jnp.minimum(u, last_unit)
    prev = lambda u: jnp.maximum(u - 1, 0)
    return pl.pallas_call(
        _ffn_kernel,
        grid=(last_unit + 2,),
        in_specs=[
            pl.BlockSpec((TM_FFN, D_MODEL), lambda u: (unit(u) // FFN_TILES, 0)),
            pl.BlockSpec((1, D_MODEL), lambda u: (0, 0)),
            pl.BlockSpec((None, D_MODEL, TF), lambda u: (layer, 0, unit(u) % FFN_TILES)),
            pl.BlockSpec((None, D_MODEL, TF), lambda u: (layer, 0, unit(u) % FFN_TILES)),
            pl.BlockSpec((None, TF, D_MODEL), lambda u: (layer, prev(u) % FFN_TILES, 0)),
        ],
        out_specs=pl.BlockSpec((TM_FFN, D_MODEL), lambda u: (prev(u) // FFN_TILES, 0)),
        out_shape=jax.ShapeDtypeStruct((s, D_MODEL), F32),
        scratch_shapes=[pltpu.VMEM((TM_FFN, D_MODEL), BF16),
                        pltpu.VMEM((TM_FFN, TF), BF16),
                        pltpu.VMEM((TM_FFN, TF), BF16)],
        compiler_params=_params("arbitrary"),
        name="ffn",
    )(x, g, w_gate, w_up, w_down)


def _rope_tables(seq_len):
    freqs = ROPE_THETA ** (-jnp.arange(ROPE_FREQS, dtype=F32) / ROPE_FREQS)
    token = jnp.arange(seq_len, dtype=jnp.int32)[:, None]
    lane = jnp.arange(HEAD_DIM, dtype=jnp.int32)[None, :]
    pos = jnp.where(lane < 2 * ROPE_FREQS, token // GRID_W, token % GRID_W).astype(F32)
    ang = pos * jnp.tile(freqs, HEAD_DIM // ROPE_FREQS)[None, :]
    sign = jnp.where(lane % (2 * ROPE_FREQS) < ROPE_FREQS, -1.0, 1.0).astype(F32)
    return jnp.cos(ang), sign * jnp.sin(ang)


def kernel(x, norm_mix, norm_ffn, attn_w_qkv, attn_q_gain, attn_k_gain, attn_w_o, rnn_w_in, rnn_conv_w, rnn_conv_b, rnn_w_a, rnn_b_a, rnn_w_i, rnn_b_i, rnn_lambda, rnn_w_out, ffn_w_gate, ffn_w_up, ffn_w_down):
    b, s, d = x.shape
    assert (b, d) == (1, D_MODEL) and s % max(TM, TQ, TK, TT) == 0
    cos_t, sin_t = _rope_tables(s)
    w_a, w_i = (0.5 * rnn_w_a).astype(BF16), (0.5 * rnn_w_i).astype(BF16)
    h = x.reshape(s, d)
    for i in range(DEPTH):
        j = i // N_MIXERS
        g_mix = norm_mix[i].reshape(1, d)
        if i % N_MIXERS == 0:
            qkv = _qkv_proj(h, g_mix, attn_w_qkv, j,
                            attn_q_gain[j].reshape(1, HEAD_DIM), attn_k_gain[j].reshape(1, HEAD_DIM),
                            cos_t, sin_t)
            o = _attention(qkv)
            h = _proj_residual(o, attn_w_o, j, h)
        else:
            xb, w_gate_rnn = _rnn_in_proj(h, g_mix, rnn_w_in, j)
            hf, hb = _rnn_core(xb, rnn_conv_w[j], rnn_conv_b[j].reshape(1, D_RNN),
                               w_a[j], rnn_b_a[j], w_i[j], rnn_b_i[j], rnn_lambda[j])
            h = _rnn_out_proj(h, g_mix, hf, hb, w_gate_rnn, rnn_w_out, j)
        h = _ffn(h, norm_ffn[i].reshape(1, d), ffn_w_gate, ffn_w_up, ffn_w_down, i)
    return h.reshape(b, s, d)
```

```python
import jax
import jax.numpy as jnp
from jax import lax
from jax.experimental import pallas as pl
from jax.experimental.pallas import tpu as pltpu

F32 = jnp.float32
BF16 = jnp.bfloat16

D_MODEL = 2048
DEPTH = 4
N_MIXERS = 2
GRID_W = 64
ROPE_THETA = 10000.0
HEAD_DIM = 128
N_Q_HEADS = 16
N_KV_HEADS = 4
GQA_GROUP = N_Q_HEADS // N_KV_HEADS
ROPE_FREQS = HEAD_DIM // 4
QKV_DIM = (N_Q_HEADS + 2 * N_KV_HEADS) * HEAD_DIM
D_RNN = D_MODEL
RNN_BLOCK_W = 256
RNN_BLOCKS = D_RNN // RNN_BLOCK_W
CONV_W = 4
CONV_LEFT = 2
LRU_C = 8.0
D_FF = 5632
EPS = 1e-6
Q_SCALE = HEAD_DIM ** -0.5 * 1.4426950408889634

SUBLANES = 8
BF16_SUBLANES = 16
LANES = 128
MXU_COLS = 256
VMEM_LIMIT_BYTES = 56 * 1024 * 1024

TM = 512
TM_OUT = 256
HEADS_PER_SLAB = MXU_COLS // HEAD_DIM
QKV_SLABS = QKV_DIM // MXU_COLS
TM_FFN = 1024
TF = 256
NORM_ROWS = 128
TQ = 512
TK = 1024
TT = 512
SCAN_BLOCKS = SUBLANES
SCAN_ROWS = TT // SCAN_BLOCKS
SCAN_PITCH = SCAN_ROWS + SUBLANES
LANE_SLABS = RNN_BLOCK_W // LANES
VT_ROWS = HEAD_DIM + BF16_SUBLANES
EXP_ROWS = 32

def _params(*sem):
    return pltpu.CompilerParams(dimension_semantics=sem, vmem_limit_bytes=VMEM_LIMIT_BYTES)


def _rms_normed(x, g):
    ms = jnp.mean(x * x, axis=-1, keepdims=True)
    return x * lax.rsqrt(ms + EPS) * g


def _qkv_kernel(x_ref, g_ref, w_ref, qg_ref, kg_ref, cos_ref, sin_ref, o_ref, hn_ref, ya_ref, yb_ref):
    hn_ref[...] = _rms_normed(x_ref[...], g_ref[...]).astype(BF16)
    q_gain = qg_ref[...] * Q_SCALE
    lane = lax.broadcasted_iota(jnp.int32, (TM, HEAD_DIM), 1)
    first_half = (lane & (2 * ROPE_FREQS - 1)) < ROPE_FREQS

    y_refs = (ya_ref, yb_ref)

    def matmul(s):
        w = w_ref[:, s * MXU_COLS:(s + 1) * MXU_COLS].astype(BF16)
        y_refs[s % 2][...] = jnp.dot(hn_ref[...], w, preferred_element_type=F32)

    def epilogue(s):
        for h in range(s * HEADS_PER_SLAB, (s + 1) * HEADS_PER_SLAB):
            lanes = slice((h % HEADS_PER_SLAB) * HEAD_DIM, (h % HEADS_PER_SLAB + 1) * HEAD_DIM)
            yh = y_refs[s % 2][:, lanes]
            if h >= N_Q_HEADS + N_KV_HEADS:
                out = yh
            else:
                gain = q_gain if h < N_Q_HEADS else kg_ref[...]
                yn = yh * lax.rsqrt(jnp.mean(yh * yh, axis=-1, keepdims=True) + EPS) * gain
                yn16 = yn.astype(BF16)
                partner = jnp.where(first_half,
                                    pltpu.roll(yn16, HEAD_DIM - ROPE_FREQS, 1),
                                    pltpu.roll(yn16, ROPE_FREQS, 1)).astype(F32)
                out = yn * cos_ref[...] + partner * sin_ref[...]
            o_ref[:, h * HEAD_DIM:(h + 1) * HEAD_DIM] = out.astype(BF16)

    matmul(0)
    for s in range(1, QKV_SLABS):
        epilogue(s - 1)
        matmul(s)
    epilogue(QKV_SLABS - 1)


def _qkv_proj(x, g, w_all, layer, q_gain, k_gain, cos_t, sin_t):
    s = x.shape[0]
    return pl.pallas_call(
        _qkv_kernel,
        grid=(s // TM,),
        in_specs=[
            pl.BlockSpec((TM, D_MODEL), lambda i: (i, 0)),
            pl.BlockSpec((1, D_MODEL), lambda i: (0, 0)),
            pl.BlockSpec((None, D_MODEL, QKV_DIM), lambda i: (layer, 0, 0),
                         pipeline_mode=pl.Buffered(1)),
            pl.BlockSpec((1, HEAD_DIM), lambda i: (0, 0)),
            pl.BlockSpec((1, HEAD_DIM), lambda i: (0, 0)),
            pl.BlockSpec((TM, HEAD_DIM), lambda i: (i, 0)),
            pl.BlockSpec((TM, HEAD_DIM), lambda i: (i, 0)),
        ],
        out_specs=pl.BlockSpec((TM, QKV_DIM), lambda i: (i, 0)),
        out_shape=jax.ShapeDtypeStruct((s, QKV_DIM), BF16),
        scratch_shapes=[pltpu.VMEM((TM, D_MODEL), BF16),
                        pltpu.VMEM((TM, MXU_COLS), F32),
                        pltpu.VMEM((TM, MXU_COLS), F32)],
        compiler_params=_params("arbitrary"),
        name="qkv_proj",
    )(x, g, w_all, q_gain, k_gain, cos_t, sin_t)


def _attn_kernel(q_ref, k_ref, v_ref, o_ref, vt_ref, acc_ref, sa_ref, sb_ref, pa_ref, pb_ref, state_ref):
    n_chunks = k_ref.shape[0] // TK
    i = pl.program_id(1)
    last_tile = pl.num_programs(1) - 1
    heads = range(GQA_GROUP)

    def score_stage(u, st_ref):
        k0 = (u % n_chunks) * TK
        k0 = k0 if isinstance(k0, int) else pl.multiple_of(k0, TK)
        q0 = pl.multiple_of(jnp.minimum(u // n_chunks, last_tile) * TQ, TQ)
        kc = k_ref[pl.ds(k0, TK), :]
        chunk_max = []
        for g in heads:
            q = q_ref[pl.ds(q0, TQ), g * HEAD_DIM:(g + 1) * HEAD_DIM]
            st = lax.dot_general(kc, q, (((1,), (1,)), ((), ())), preferred_element_type=F32)
            st_ref[g] = st
            chunk_max.append(jnp.max(st, axis=0, keepdims=True))
        return tuple(chunk_max)

    def exp_stage(u, st_ref, pt_ref, ms, chunk_max):
        first = u % n_chunks == 0
        new_ms, alphas = [], []
        for g in heads:
            m_old = jnp.where(first, -jnp.inf, ms[g])
            m_new = jnp.maximum(m_old, chunk_max[g])
            for r in range(0, TK, EXP_ROWS):
                pt_ref[g, r:r + EXP_ROWS, :] = jnp.exp2(st_ref[g, r:r + EXP_ROWS, :] - m_new).astype(BF16)
            new_ms.append(m_new)
            alphas.append(jnp.exp2(m_old - m_new))
        return tuple(new_ms), tuple(alphas)

    def out_stage(c, pt_ref, alphas):
        for g in heads:
            acc_ref[g] = alphas[g] * acc_ref[g] + jnp.dot(vt_ref[c], pt_ref[g], preferred_element_type=F32)

    def save_state(ms, alphas, chunk_max):
        for k, rows in enumerate((ms, alphas, chunk_max)):
            for g in heads:
                state_ref[k, g] = rows[g]

    @pl.when(i == 0)
    def _():
        for c in range(n_chunks):
            vc = v_ref[c * TK:(c + 1) * TK, :].astype(F32)
            vt_ref[c, :HEAD_DIM, :] = vc.T.astype(BF16)
            vt_ref[c, HEAD_DIM:, :] = jnp.ones((VT_ROWS - HEAD_DIM, TK), BF16)
        acc_ref[...] = jnp.zeros_like(acc_ref)
        ms = (jnp.full((1, TQ), -jnp.inf, F32),) * GQA_GROUP
        chunk_max = score_stage(0, sa_ref)
        ms, alphas = exp_stage(0, sa_ref, pa_ref, ms, chunk_max)
        save_state(ms, alphas, score_stage(1, sb_ref))

    def trip(c, carry, s_cur, p_cur, s_nxt, p_nxt):
        u = i * n_chunks + c
        ms, alphas, chunk_max = carry
        out_stage(c, p_cur, alphas)
        ms, alphas = exp_stage(u + 1, s_nxt, p_nxt, ms, chunk_max)
        chunk_max = score_stage(u + 2, s_cur)
        return ms, alphas, chunk_max

    def step(c, carry):
        return lax.cond(c % 2 == 0,
                        lambda cr: trip(c, cr, sa_ref, pa_ref, sb_ref, pb_ref),
                        lambda cr: trip(c, cr, sb_ref, pb_ref, sa_ref, pa_ref),
                        carry)

    carry = tuple(tuple(state_ref[k, g] for g in heads) for k in range(3))
    save_state(*lax.fori_loop(0, n_chunks, step, carry))
    for g in heads:
        acc = acc_ref[g]
        out_t = acc[:HEAD_DIM, :] / acc[HEAD_DIM:HEAD_DIM + 1, :]
        o_ref[:, g * HEAD_DIM:(g + 1) * HEAD_DIM] = out_t.T.astype(BF16)


def _attention(qkv):
    s = qkv.shape[0]
    assert (s // TK) % 2 == 0, "the a / b buffer parity of the attention pipeline needs an even chunk count"
    group_w = GQA_GROUP * HEAD_DIM
    return pl.pallas_call(
        _attn_kernel,
        grid=(N_KV_HEADS, s // TQ),
        in_specs=[
            pl.BlockSpec((s, group_w), lambda h, i: (0, h), pipeline_mode=pl.Buffered(1)),
            pl.BlockSpec((s, HEAD_DIM), lambda h, i: (0, N_Q_HEADS + h)),
            pl.BlockSpec((s, HEAD_DIM), lambda h, i: (0, N_Q_HEADS + N_KV_HEADS + h)),
        ],
        out_specs=pl.BlockSpec((TQ, group_w), lambda h, i: (i, h)),
        out_shape=jax.ShapeDtypeStruct((s, N_Q_HEADS * HEAD_DIM), BF16),
        scratch_shapes=[pltpu.VMEM((s // TK, VT_ROWS, TK), BF16),
                        pltpu.VMEM((GQA_GROUP, VT_ROWS, TQ), F32),
                        pltpu.VMEM((GQA_GROUP, TK, TQ), F32),
                        pltpu.VMEM((GQA_GROUP, TK, TQ), F32),
                        pltpu.VMEM((GQA_GROUP, TK, TQ), BF16),
                        pltpu.VMEM((GQA_GROUP, TK, TQ), BF16),
                        pltpu.VMEM((3, GQA_GROUP, 1, TQ), F32)],
        compiler_params=_params("arbitrary", "arbitrary"),
        name="attention",
    )(qkv, qkv, qkv)


def _proj_res_kernel(a_ref, w_ref, x_ref, o_ref, w16_ref):
    @pl.when(pl.program_id(0) == 0)
    def _():
        w16_ref[...] = w_ref[...].astype(BF16)

    o_ref[...] = x_ref[...] + jnp.dot(a_ref[...], w16_ref[...], preferred_element_type=F32)


def _proj_residual(a, w_all, layer, x):
    s, k = a.shape
    return pl.pallas_call(
        _proj_res_kernel,
        grid=(s // TM,),
        in_specs=[
            pl.BlockSpec((TM, k), lambda i: (i, 0)),
            pl.BlockSpec((None, k, D_MODEL), lambda i: (layer, 0, 0), pipeline_mode=pl.Buffered(1)),
            pl.BlockSpec((TM, D_MODEL), lambda i: (i, 0)),
        ],
        out_specs=pl.BlockSpec((TM, D_MODEL), lambda i: (i, 0)),
        out_shape=jax.ShapeDtypeStruct((s, D_MODEL), F32),
        scratch_shapes=[pltpu.VMEM((k, D_MODEL), BF16)],
        compiler_params=_params("arbitrary"),
        name="proj_residual",
    )(a, w_all, x)


def _rnn_in_kernel(x_ref, g_ref, w_ref, wg_ref, xb_ref, wg16_ref, w16_ref):
    @pl.when(pl.program_id(0) == 0)
    def _():
        w16_ref[...] = w_ref[...].astype(BF16)

    hn = _rms_normed(x_ref[...], g_ref[...]).astype(BF16)
    xb_ref[...] = jnp.dot(hn, w16_ref[...], preferred_element_type=F32)
    wg16_ref[...] = wg_ref[...].astype(BF16)


def _rnn_in_proj(x, g, w_all, layer):
    s = x.shape[0]
    steps = s // TM
    gate_tn = D_RNN // steps
    assert gate_tn * steps == D_RNN and gate_tn % LANES == 0
    return pl.pallas_call(
        _rnn_in_kernel,
        grid=(steps,),
        in_specs=[
            pl.BlockSpec((TM, D_MODEL), lambda i: (i, 0)),
            pl.BlockSpec((1, D_MODEL), lambda i: (0, 0)),
            pl.BlockSpec((None, D_MODEL, D_RNN), lambda i: (layer, 0, 0), pipeline_mode=pl.Buffered(1)),
            pl.BlockSpec((None, D_MODEL, gate_tn), lambda i: (layer, 0, steps + i)),
        ],
        out_specs=[pl.BlockSpec((TM, D_RNN), lambda i: (i, 0)),
                   pl.BlockSpec((D_MODEL, gate_tn), lambda i: (0, i))],
        out_shape=[jax.ShapeDtypeStruct((s, D_RNN), F32),
                   jax.ShapeDtypeStruct((D_MODEL, D_RNN), BF16)],
        scratch_shapes=[pltpu.VMEM((D_MODEL, D_RNN), BF16)],
        compiler_params=_params("arbitrary"),
        name="rnn_in_proj",
    )(x, g, w_all, w_all)


def _softplus(z):
    return jnp.maximum(z, 0.0) + jnp.log1p(jnp.exp(-jnp.abs(z)))


def _lru_inputs(d, cur_ref, prev_ref, next_ref, first, last, cw_ref, cb_ref, wa_ref, ba_ref, wi_ref,
                bi_ref, lam_ref, ext_ref, a_ref, u_ref):
    ext_ref[d, 0:SUBLANES, :] = jnp.where(first, 0.0, prev_ref[...])
    ext_ref[d, SUBLANES:SUBLANES + TT, :] = cur_ref[...]
    ext_ref[d, SUBLANES + TT:2 * SUBLANES + TT, :] = jnp.where(last, 0.0, next_ref[...])
    ext = ext_ref[d]
    xc = cb_ref[...]
    for k in range(CONV_W):
        shift = (CONV_LEFT - k) % ext.shape[0]
        tap = ext if shift == 0 else pltpu.roll(ext, shift, 0)
        xc = xc + cw_ref[k:k + 1, :] * tap[SUBLANES:SUBLANES + TT, :]
    xb16 = xc.astype(BF16)
    ta = jnp.tanh(jnp.dot(xb16, wa_ref[d, 0], preferred_element_type=F32) + 0.5 * ba_ref[d:d + 1, :])
    ti = jnp.tanh(jnp.dot(xb16, wi_ref[d, 0], preferred_element_type=F32) + 0.5 * bi_ref[d:d + 1, :])
    i = 0.5 * ti + 0.5
    half_c = (-0.5 * LRU_C) * _softplus(-lam_ref[d:d + 1, :])
    log_a = ta * half_c + half_c
    a = jnp.exp(log_a)
    one_minus_a2 = -jnp.tanh(log_a) * (a * a + 1.0)
    root = jnp.where(one_minus_a2 > 0.0, one_minus_a2 * lax.rsqrt(one_minus_a2), 0.0)
    u = root * (i * xc)
    for s in range(SCAN_BLOCKS):
        for l in range(LANE_SLABS):
            rows = slice(s * SCAN_ROWS, (s + 1) * SCAN_ROWS)
            lanes = slice(l * LANES, (l + 1) * LANES)
            a_ref[d, l, s * SCAN_PITCH:s * SCAN_PITCH + SCAN_ROWS, :] = a[rows, lanes]
            u_ref[d, l, s * SCAN_PITCH:s * SCAN_PITCH + SCAN_ROWS, :] = u[rows, lanes]


def _rnn_core_kernel(fc_ref, fp_ref, fn_ref, bc_ref, bp_ref, bn_ref, cw_ref, cb_ref, wa_ref, ba_ref,
                     wi_ref, bi_ref, lam_ref, hf_ref, hb_ref, ext_ref, a_ref, u_ref, hl_ref, p_ref,
                     carry_ref):
    i = pl.program_id(1)
    n = pl.num_programs(1)

    @pl.when(i == 0)
    def _():
        carry_ref[...] = jnp.zeros_like(carry_ref)

    shared = (cw_ref, cb_ref, wa_ref, ba_ref, wi_ref, bi_ref, lam_ref, ext_ref, a_ref, u_ref)
    _lru_inputs(0, fc_ref, fp_ref, fn_ref, i == 0, i == n - 1, *shared)
    _lru_inputs(1, bc_ref, bp_ref, bn_ref, i == n - 1, i == 0, *shared)

    pairs = [(d, l) for d in range(2) for l in range(LANE_SLABS)]

    def body(jj, carry):
        out = []
        for (d, l), (h, p) in zip(pairs, carry):
            j = jj if d == 0 else SCAN_ROWS - 1 - jj
            rows = pl.ds(j, SCAN_BLOCKS, stride=SCAN_PITCH)
            a = a_ref[d, l, rows, :]
            h = a * h + u_ref[d, l, rows, :]
            p = a * p
            hl_ref[d, l, rows, :] = h
            p_ref[d, l, rows, :] = p
            out.append((h, p))
        return tuple(out)

    zero = jnp.zeros((SCAN_BLOCKS, LANES), F32)
    ends = ((zero, zero + 1.0),) * len(pairs)
    for jj in range(SCAN_ROWS):
        ends = body(jj, ends)

    for (d, l), (h_end, p_end) in zip(pairs, ends):
        lanes = slice(l * LANES, (l + 1) * LANES)
        out_ref = hf_ref if d == 0 else hb_ref
        c = carry_ref[d, 0:1, lanes]
        for s in (range(SCAN_BLOCKS) if d == 0 else reversed(range(SCAN_BLOCKS))):
            blk = slice(s * SCAN_PITCH, s * SCAN_PITCH + SCAN_ROWS)
            out_ref[s * SCAN_ROWS:(s + 1) * SCAN_ROWS, lanes] = hl_ref[d, l, blk, :] + p_ref[d, l, blk, :] * c
            c = h_end[s:s + 1, :] + p_end[s:s + 1, :] * c
        carry_ref[d, :, lanes] = jnp.broadcast_to(c, (SUBLANES, LANES))


def _rnn_core(xb, conv_w, conv_b, w_a, b_a, w_i, b_i, lam):
    s = xb.shape[0]
    n = s // TT
    halo = TT // SUBLANES
    last_halo = s // SUBLANES - 1
    cur = lambda t: pl.BlockSpec((TT, RNN_BLOCK_W), lambda c, i: (t(i, n), c))
    prv = lambda t: pl.BlockSpec((SUBLANES, RNN_BLOCK_W),
                                 lambda c, i: (jnp.maximum(t(i, n) * halo - 1, 0), c))
    nxt = lambda t: pl.BlockSpec((SUBLANES, RNN_BLOCK_W),
                                 lambda c, i: (jnp.minimum((t(i, n) + 1) * halo, last_halo), c))
    fwd = lambda i, n: i
    bwd = lambda i, n: n - 1 - i
    chan = lambda rows: pl.BlockSpec((rows, RNN_BLOCK_W), lambda c, i: (0, c))
    gate_w = pl.BlockSpec((2, 1, RNN_BLOCK_W, RNN_BLOCK_W), lambda c, i: (0, c, 0, 0))
    return pl.pallas_call(
        _rnn_core_kernel,
        grid=(RNN_BLOCKS, n),
        in_specs=[cur(fwd), prv(fwd), nxt(fwd), cur(bwd), prv(bwd), nxt(bwd),
                  chan(CONV_W), chan(1), gate_w, chan(2), gate_w, chan(2), chan(2)],
        out_specs=[pl.BlockSpec((TT, RNN_BLOCK_W), lambda c, i: (i, c)),
                   pl.BlockSpec((TT, RNN_BLOCK_W), lambda c, i: (n - 1 - i, c))],
        out_shape=[jax.ShapeDtypeStruct((s, D_RNN), F32), jax.ShapeDtypeStruct((s, D_RNN), F32)],
        scratch_shapes=[
            pltpu.VMEM((2, TT + 2 * SUBLANES, RNN_BLOCK_W), F32),
            pltpu.VMEM((2, LANE_SLABS, SCAN_BLOCKS * SCAN_PITCH, LANES), F32),
            pltpu.VMEM((2, LANE_SLABS, SCAN_BLOCKS * SCAN_PITCH, LANES), F32),
            pltpu.VMEM((2, LANE_SLABS, SCAN_BLOCKS * SCAN_PITCH, LANES), F32),
            pltpu.VMEM((2, LANE_SLABS, SCAN_BLOCKS * SCAN_PITCH, LANES), F32),
            pltpu.VMEM((2, SUBLANES, RNN_BLOCK_W), F32),
        ],
        compiler_params=_params("arbitrary", "arbitrary"),
        name="rnn_core",
    )(xb, xb, xb, xb, xb, xb, conv_w, conv_b, w_a, b_a, w_i, b_i, lam)


def _rnn_out_kernel(x_ref, g_ref, hf_ref, hb_ref, wy_ref, wo_ref, o_ref, wo16_ref):
    @pl.when(pl.program_id(0) == 0)
    def _():
        wo16_ref[...] = wo_ref[...].astype(BF16)

    x = x_ref[...]
    hn = _rms_normed(x, g_ref[...]).astype(BF16)
    gate = jax.nn.gelu(jnp.dot(hn, wy_ref[...], preferred_element_type=F32), approximate=True)
    a = ((hf_ref[...] + hb_ref[...]) * gate).astype(BF16)
    o_ref[...] = x + jnp.dot(a, wo16_ref[...], preferred_element_type=F32)


def _rnn_out_proj(x, g, hf, hb, w_gate, w_out_all, layer):
    s = x.shape[0]
    row = pl.BlockSpec((TM_OUT, D_MODEL), lambda i: (i, 0))
    return pl.pallas_call(
        _rnn_out_kernel,
        grid=(s // TM_OUT,),
        in_specs=[row,
                  pl.BlockSpec((1, D_MODEL), lambda i: (0, 0)),
                  row, row,
                  pl.BlockSpec((D_MODEL, D_RNN), lambda i: (0, 0), pipeline_mode=pl.Buffered(1)),
                  pl.BlockSpec((None, D_RNN, D_MODEL), lambda i: (layer, 0, 0),
                               pipeline_mode=pl.Buffered(1))],
        out_specs=row,
        out_shape=jax.ShapeDtypeStruct((s, D_MODEL), F32),
        scratch_shapes=[pltpu.VMEM((D_RNN, D_MODEL), BF16)],
        compiler_params=_params("arbitrary"),
        name="rnn_out_proj",
    )(x, g, hf, hb, w_gate, w_out_all)


def _ffn_kernel(x_ref, g_ref, wg_ref, wu_ref, wd_ref, o_ref, hn_ref):
    @pl.when(pl.program_id(1) == 0)
    def _():
        for r in range(0, TM_FFN, NORM_ROWS):
            x = x_ref[r:r + NORM_ROWS, :]
            hn_ref[r:r + NORM_ROWS, :] = _rms_normed(x, g_ref[...]).astype(BF16)
            o_ref[r:r + NORM_ROWS, :] = x

    hn = hn_ref[...]
    gate = jnp.dot(hn, wg_ref[...].astype(BF16), preferred_element_type=F32)
    up = jnp.dot(hn, wu_ref[...].astype(BF16), preferred_element_type=F32)
    act = (jax.nn.silu(gate) * up).astype(BF16)
    o_ref[...] += jnp.dot(act, wd_ref[...].astype(BF16), preferred_element_type=F32)


def _ffn(x, g, w_gate, w_up, w_down, layer):
    s = x.shape[0]
    return pl.pallas_call(
        _ffn_kernel,
        grid=(s // TM_FFN, D_FF // TF),
        in_specs=[
            pl.BlockSpec((TM_FFN, D_MODEL), lambda i, f: (i, 0)),
            pl.BlockSpec((1, D_MODEL), lambda i, f: (0, 0)),
            pl.BlockSpec((None, D_MODEL, TF), lambda i, f: (layer, 0, f)),
            pl.BlockSpec((None, D_MODEL, TF), lambda i, f: (layer, 0, f)),
            pl.BlockSpec((None, TF, D_MODEL), lambda i, f: (layer, f, 0)),
        ],
        out_specs=pl.BlockSpec((TM_FFN, D_MODEL), lambda i, f: (i, 0)),
        out_shape=jax.ShapeDtypeStruct((s, D_MODEL), F32),
        scratch_shapes=[pltpu.VMEM((TM_FFN, D_MODEL), BF16)],
        compiler_params=_params("arbitrary", "arbitrary"),
        name="ffn",
    )(x, g, w_gate, w_up, w_down)


def _rope_tables(seq_len):
    freqs = ROPE_THETA ** (-jnp.arange(ROPE_FREQS, dtype=F32) / ROPE_FREQS)
    token = jnp.arange(seq_len, dtype=jnp.int32)[:, None]
    lane = jnp.arange(HEAD_DIM, dtype=jnp.int32)[None, :]
    pos = jnp.where(lane < 2 * ROPE_FREQS, token // GRID_W, token % GRID_W).astype(F32)
    ang = pos * jnp.tile(freqs, HEAD_DIM // ROPE_FREQS)[None, :]
    sign = jnp.where(lane % (2 * ROPE_FREQS) < ROPE_FREQS, -1.0, 1.0).astype(F32)
    return jnp.cos(ang), sign * jnp.sin(ang)


def kernel(x, norm_mix, norm_ffn, attn_w_qkv, attn_q_gain, attn_k_gain, attn_w_o, rnn_w_in, rnn_conv_w, rnn_conv_b, rnn_w_a, rnn_b_a, rnn_w_i, rnn_b_i, rnn_lambda, rnn_w_out, ffn_w_gate, ffn_w_up, ffn_w_down):
    b, s, d = x.shape
    assert (b, d) == (1, D_MODEL) and s % max(TM, TQ, TK, TT) == 0
    cos_t, sin_t = _rope_tables(s)
    w_a, w_i = (0.5 * rnn_w_a).astype(BF16), (0.5 * rnn_w_i).astype(BF16)
    h = x.reshape(s, d)
    for i in range(DEPTH):
        j = i // N_MIXERS
        g_mix = norm_mix[i].reshape(1, d)
        if i % N_MIXERS == 0:
            qkv = _qkv_proj(h, g_mix, attn_w_qkv, j,
                            attn_q_gain[j].reshape(1, HEAD_DIM), attn_k_gain[j].reshape(1, HEAD_DIM),
                            cos_t, sin_t)
            o = _attention(qkv)
            h = _proj_residual(o, attn_w_o, j, h)
        else:
            xb, w_gate_rnn = _rnn_in_proj(h, g_mix, rnn_w_in, j)
            hf, hb = _rnn_core(xb, rnn_conv_w[j], rnn_conv_b[j].reshape(1, D_RNN),
                               w_a[j], rnn_b_a[j], w_i[j], rnn_b_i[j], rnn_lambda[j])
            h = _rnn_out_proj(h, g_mix, hf, hb, w_gate_rnn, rnn_w_out, j)
        h = _ffn(h, norm_ffn[i].reshape(1, d), ffn_w_gate, ffn_w_up, ffn_w_down, i)
    return h.reshape(b, s, d)
```
